```python
import math
import jax, jax.numpy as jnp
from jax import lax
import numpy as np

D_MODEL = 4096
BATCH = 4
SEQ = 2048
DEPTH = 2
DEC_BATCH = 32
DEC_SEQ = 4
PAST_LEN = 16384
PAGE_SIZE = 128

N_EVEN = (DEPTH + 1) // 2
N_ODD = DEPTH // 2

D_S5 = D_MODEL // 2
S5_GROUP = 16
S5_GROUPS = D_S5 // S5_GROUP
S5_STATE = 64

HEAD_DIM = 64
SWA_HEADS = (D_MODEL // 2) // HEAD_DIM
SWA_KV_HEADS = 8
SWA_GQ = SWA_HEADS // SWA_KV_HEADS
D_SWA_Q = SWA_HEADS * HEAD_DIM
D_SWA_KV = SWA_KV_HEADS * HEAD_DIM
WINDOW = 128
SWA_BLOCK = 128
N_BUCKETS = 32
MAX_DISTANCE = 128

HG_DK = 128
HG_HEADS = D_MODEL // HG_DK
HG_DV = D_MODEL // HG_HEADS
HG_CHUNK = 32

N_MEM = 256
MEM_HEADS = 4
MEM_HEAD_DIM = 128
D_MEM = MEM_HEADS * MEM_HEAD_DIM

D_FF = 256 * ((-(-8 * D_MODEL // 3) + 255) // 256)

ALPHA = (2 * DEPTH) ** 0.25
BETA = (8 * DEPTH) ** -0.25
LN_EPS = 1e-5
RMS_EPS = 1e-6
NEG_BIG = -1e30

kernel_name = "s5_swa_hgrn2_memory_hybrid_step"


def layer_norm(x, g, b):
    xf = x.astype(jnp.float32)
    mu = jnp.mean(xf, axis=-1, keepdims=True)
    var = jnp.mean(jnp.square(xf - mu), axis=-1, keepdims=True)
    return ((xf - mu) * lax.rsqrt(var + LN_EPS) * g.astype(jnp.float32) + b.astype(jnp.float32)).astype(x.dtype)


def rms_norm(x, g):
    xf = x.astype(jnp.float32)
    return xf * lax.rsqrt(jnp.mean(xf * xf, axis=-1, keepdims=True) + RMS_EPS) * g.astype(jnp.float32)


def t5_bucket(dist):
    n = jnp.maximum(dist, 0)
    max_exact = N_BUCKETS // 2
    nf = jnp.maximum(n, 1).astype(jnp.float32)
    large = max_exact + (jnp.log(nf / max_exact) / math.log(MAX_DISTANCE / max_exact)
                         * (N_BUCKETS - max_exact)).astype(jnp.int32)
    large = jnp.minimum(large, N_BUCKETS - 1)
    return jnp.where(n < max_exact, n, large)


def s5_discretize(lam_re, lam_im, log_dt, b_re, b_im):
    f32 = jnp.float32
    lr = jnp.minimum(lam_re.astype(f32), -1e-4)
    li = lam_im.astype(f32)
    dt = jnp.exp(log_dt.astype(f32))[:, None]
    mag = jnp.exp(lr * dt)
    a_re = mag * jnp.cos(li * dt)
    a_im = mag * jnp.sin(li * dt)
    den = lr * lr + li * li
    fr = ((a_re - 1.0) * lr + a_im * li) / den
    fi = (a_im * lr - (a_re - 1.0) * li) / den
    br, bi = b_re.astype(f32), b_im.astype(f32)
    bb_re = fr[..., None] * br - fi[..., None] * bi
    bb_im = fr[..., None] * bi + fi[..., None] * br
    return a_re, a_im, bb_re, bb_im


def _complex_affine_combine(e1, e2):
    a1r, a1i, b1r, b1i = e1
    a2r, a2i, b2r, b2i = e2
    return (a2r * a1r - a2i * a1i,
            a2r * a1i + a2i * a1r,
            a2r * b1r - a2i * b1i + b2r,
            a2r * b1i + a2i * b1r + b2i)


def s5_scan(u, h0_re, h0_im, a_re, a_im, bb_re, bb_im, c_re, c_im, d_skip):
    f32 = jnp.float32
    uf = u.astype(f32)
    br = jnp.einsum('gph,btgh->btgp', bb_re, uf)
    bi = jnp.einsum('gph,btgh->btgp', bb_im, uf)
    h0r, h0i = h0_re.astype(f32), h0_im.astype(f32)
    br = br.at[:, 0].add(a_re * h0r - a_im * h0i)
    bi = bi.at[:, 0].add(a_re * h0i + a_im * h0r)
    t = u.shape[1]
    ar = jnp.broadcast_to(a_re, (1, t) + a_re.shape)
    ai = jnp.broadcast_to(a_im, (1, t) + a_im.shape)
    _, _, hr, hi = lax.associative_scan(_complex_affine_combine, (ar, ai, br, bi), axis=1)
    y = (jnp.einsum('ghp,btgp->btgh', c_re.astype(f32), hr)
         - jnp.einsum('ghp,btgp->btgh', c_im.astype(f32), hi)
         + d_skip.astype(f32) * uf)
    return y, hr[:, -1], hi[:, -1]


def swa_attend(q, k, v, q_pos, k_pos, sinks, rel_bias):
    f32 = jnp.float32
    lead = q.shape[:-3]
    tq, tk = q.shape[-3], k.shape[-3]
    qg = q.reshape(*lead, tq, SWA_KV_HEADS, SWA_GQ, HEAD_DIM)
    s = jnp.einsum('...qhgd,...shd->...hgqs', qg, k).astype(f32) * (HEAD_DIM ** -0.5)
    dist = q_pos[..., :, None] - k_pos[..., None, :]
    valid = (dist >= 0) & (dist < WINDOW) & (k_pos[..., None, :] >= 0)
    bias = jnp.moveaxis(rel_bias[t5_bucket(dist)], -1, -3)
    bias = bias.reshape(*bias.shape[:-3], SWA_KV_HEADS, SWA_GQ, tq, tk)
    s = s + bias.astype(f32)
    s = jnp.where(valid[..., None, None, :, :], s, NEG_BIG)
    sk = sinks.astype(f32).reshape(SWA_KV_HEADS, SWA_GQ)[..., None, None]
    m = jnp.maximum(jnp.max(s, axis=-1, keepdims=True), sk)
    p = jnp.exp(s - m)
    p = p / (jnp.sum(p, axis=-1, keepdims=True) + jnp.exp(sk - m))
    o = jnp.einsum('...hgqs,...shd->...qhgd', p.astype(v.dtype), v)
    return o.reshape(*lead, tq, SWA_HEADS * HEAD_DIM)


def swa_prompt(q, k, v, sinks, rel_bias):
    b, t = q.shape[:2]
    nb = t // SWA_BLOCK
    qb = q.reshape(b, nb, SWA_BLOCK, SWA_HEADS, HEAD_DIM)
    kb = k.reshape(b, nb, SWA_BLOCK, SWA_KV_HEADS, HEAD_DIM)
    vb = v.reshape(b, nb, SWA_BLOCK, SWA_KV_HEADS, HEAD_DIM)
    kk = jnp.concatenate([jnp.concatenate([jnp.zeros_like(kb[:, :1]), kb[:, :-1]], axis=1), kb], axis=2)
    vv = jnp.concatenate([jnp.concatenate([jnp.zeros_like(vb[:, :1]), vb[:, :-1]], axis=1), vb], axis=2)
    pos = jnp.arange(t, dtype=jnp.int32).reshape(nb, SWA_BLOCK)
    k_pos = jnp.concatenate([pos - SWA_BLOCK, pos], axis=1)
    o = swa_attend(qb, kk, vv, pos, k_pos, sinks, rel_bias)
    return o.reshape(b, t, D_SWA_Q)


def swa_sample(q, k, v, buf_k, buf_v, sinks, rel_bias):
    t = q.shape[1]
    w_buf = buf_k.shape[1]
    kk = jnp.concatenate([buf_k.astype(k.dtype), k], axis=1)
    vv = jnp.concatenate([buf_v.astype(v.dtype), v], axis=1)
    q_pos = PAST_LEN + jnp.arange(t, dtype=jnp.int32)
    k_pos = jnp.concatenate([PAST_LEN - w_buf + jnp.arange(w_buf, dtype=jnp.int32), q_pos])
    o = swa_attend(q, kk, vv, q_pos, k_pos, sinks, rel_bias)
    return o, kk[:, -w_buf:], vv[:, -w_buf:]


def even_mixer(x, h0_re, h0_im, buf_k, buf_v, w_in, s5p, sinks, rel_bias, w_out, w_buf):
    b, t, _ = x.shape
    a_re, a_im, bb_re, bb_im, c_re, c_im, d_skip, w_glu = s5p
    h = x @ w_in
    u, q, k, v = jnp.split(h, [D_S5, D_S5 + D_SWA_Q, D_S5 + D_SWA_Q + D_SWA_KV], axis=-1)
    y, hr, hi = s5_scan(u.reshape(b, t, S5_GROUPS, S5_GROUP), h0_re, h0_im,
                        a_re, a_im, bb_re, bb_im, c_re, c_im, d_skip)
    z = jax.nn.gelu(y.reshape(b, t, D_S5)).astype(x.dtype)
    s5_out = z * jax.nn.sigmoid(z @ w_glu)
    q = q.reshape(b, t, SWA_HEADS, HEAD_DIM)
    k = k.reshape(b, t, SWA_KV_HEADS, HEAD_DIM)
    v = v.reshape(b, t, SWA_KV_HEADS, HEAD_DIM)
    if buf_k is None:
        att = swa_prompt(q, k, v, sinks, rel_bias)
        new_k, new_v = k[:, -w_buf:], v[:, -w_buf:]
    else:
        att, new_k, new_v = swa_sample(q, k, v, buf_k, buf_v, sinks, rel_bias)
    out = jnp.concatenate([s5_out, att.astype(x.dtype)], axis=-1) @ w_out
    return out, hr, hi, new_k, new_v


def hgrn2_recurrence(q, k, v, logf, s0):
    b, t = q.shape[:2]
    pad = (-t) % HG_CHUNK
    if pad:
        pw = ((0, 0), (0, pad), (0, 0), (0, 0))
        q, k, v, logf = (jnp.pad(a, pw) for a in (q, k, v, logf))
    nc = (t + pad) // HG_CHUNK

    def chunks(a):
        return a.reshape(b, nc, HG_CHUNK, HG_HEADS, a.shape[-1]).transpose(1, 0, 3, 2, 4)

    q, k, v, logf = chunks(q), chunks(k), chunks(v), chunks(logf)
    cum = jnp.cumsum(logf, axis=-2)
    q_t = q * jnp.exp(cum)
    k_t = k * jnp.exp(-cum)
    k_last = k * jnp.exp(cum[..., -1:, :] - cum)
    decay_last = jnp.exp(cum[..., -1, :])
    mask = jnp.tril(jnp.ones((HG_CHUNK, HG_CHUNK), dtype=bool))
    attn = jnp.where(mask, jnp.einsum('nbhcd,nbhsd->nbhcs', q_t, k_t), 0.0)
    o_intra = jnp.einsum('nbhcs,nbhsv->nbhcv', attn, v)

    def step(s, inp):
        qc, kc, vc, dc = inp
        o_inter = jnp.einsum('bhcd,bhdv->bhcv', qc, s)
        s = dc[..., None] * s + jnp.einsum('bhcd,bhcv->bhdv', kc, vc)
        return s, o_inter

    s_fin, o_inter = lax.scan(step, s0, (q_t, k_last, v, decay_last))
    o = (o_intra + o_inter).transpose(1, 0, 3, 2, 4).reshape(b, nc * HG_CHUNK, HG_HEADS, HG_DV)[:, :t]
    return o, s_fin


def odd_mixer(x, s0, lb, w_in, norm_g, w_out):
    b, t, _ = x.shape
    f32 = jnp.float32
    q, fz, i, g = jnp.split(x @ w_in, 4, axis=-1)
    q = jax.nn.silu(q.astype(f32))
    f = lb + (1.0 - lb) * jax.nn.sigmoid(fz.astype(f32))
    logf = jnp.log(f)
    k = 1.0 - f
    o, s_new = hgrn2_recurrence(q.reshape(b, t, HG_HEADS, HG_DK), k.reshape(b, t, HG_HEADS, HG_DK),
                                i.astype(f32).reshape(b, t, HG_HEADS, HG_DV),
                                logf.reshape(b, t, HG_HEADS, HG_DK), s0.astype(f32))
    o = rms_norm(o.reshape(b, t, D_MODEL), norm_g) * jax.nn.sigmoid(g.astype(f32))
    return o.astype(x.dtype) @ w_out, s_new


def memory_kv(mem, wk, wv):
    b = mem.shape[0]
    return ((mem @ wk).reshape(b, N_MEM, MEM_HEADS, MEM_HEAD_DIM),
            (mem @ wv).reshape(b, N_MEM, MEM_HEADS, MEM_HEAD_DIM))


def cross_attn(x, mk, mv, wq, wo):
    b, t, _ = x.shape
    q = (x @ wq).reshape(b, t, MEM_HEADS, MEM_HEAD_DIM)
    s = jnp.einsum('bthd,bshd->bhts', q, mk.astype(q.dtype)).astype(jnp.float32) * (MEM_HEAD_DIM ** -0.5)
    p = jax.nn.softmax(s, axis=-1)
    o = jnp.einsum('bhts,bshd->bthd', p.astype(x.dtype), mv.astype(x.dtype)).reshape(b, t, D_MEM)
    return o @ wo


def swiglu(x, wg, wu, wd):
    return (jax.nn.silu(x @ wg) * (x @ wu)) @ wd


def setup_inputs(seed: int = 0) -> dict:
    key = jax.random.key(seed)
    keys = iter(jax.random.split(key, 48))
    f32 = jnp.float32

    def nrm(shape, scale):
        return jax.random.normal(next(keys), shape, f32) * scale

    w_buf = min(WINDOW, PAST_LEN)
    d_in_even = D_S5 + D_SWA_Q + 2 * D_SWA_KV
    n_idx = jnp.arange(S5_STATE, dtype=f32)
    return {
        "x_prompt": nrm((BATCH, SEQ, D_MODEL), 1.0),
        "x_sample": nrm((DEC_BATCH, DEC_SEQ, D_MODEL), 1.0),
        "cache_mem_k": nrm((DEPTH, DEC_BATCH, N_MEM, MEM_HEADS, MEM_HEAD_DIM), 1.0),
        "cache_mem_v": nrm((DEPTH, DEC_BATCH, N_MEM, MEM_HEADS, MEM_HEAD_DIM), 1.0),
        "cache_swa_k": nrm((N_EVEN, DEC_BATCH, w_buf, SWA_KV_HEADS, HEAD_DIM), 1.0),
        "cache_swa_v": nrm((N_EVEN, DEC_BATCH, w_buf, SWA_KV_HEADS, HEAD_DIM), 1.0),
        "state_s5_re": nrm((N_EVEN, DEC_BATCH, S5_GROUPS, S5_STATE), 0.5),
        "state_s5_im": nrm((N_EVEN, DEC_BATCH, S5_GROUPS, S5_STATE), 0.5),
        "state_hgrn": nrm((N_ODD, DEC_BATCH, HG_HEADS, HG_DK, HG_DV), 0.5),
        "mem_prompt": nrm((BATCH, N_MEM, D_MODEL), 1.0),
        "rel_bias": nrm((N_BUCKETS, SWA_HEADS), 0.5),
        "w_even_in": nrm((N_EVEN, D_MODEL, d_in_even), D_MODEL ** -0.5),
        "s5_lam_re": -0.5 + nrm((N_EVEN, S5_GROUPS, S5_STATE), 0.01),
        "s5_lam_im": jnp.pi * n_idx + nrm((N_EVEN, S5_GROUPS, S5_STATE), 0.01),
        "s5_log_dt": jax.random.uniform(next(keys), (N_EVEN, S5_GROUPS), f32,
                                        minval=math.log(1e-3), maxval=math.log(1e-1)),
        "s5_b_re": nrm((N_EVEN, S5_GROUPS, S5_STATE, S5_GROUP), (2 * S5_GROUP) ** -0.5),
        "s5_b_im": nrm((N_EVEN, S5_GROUPS, S5_STATE, S5_GROUP), (2 * S5_GROUP) ** -0.5),
        "s5_c_re": nrm((N_EVEN, S5_GROUPS, S5_GROUP, S5_STATE), S5_STATE ** -0.5),
        "s5_c_im": nrm((N_EVEN, S5_GROUPS, S5_GROUP, S5_STATE), S5_STATE ** -0.5),
        "s5_d": nrm((N_EVEN, S5_GROUPS, S5_GROUP), 1.0),
        "s5_w_glu": nrm((N_EVEN, D_S5, D_S5), D_S5 ** -0.5),
        "swa_sinks": nrm((N_EVEN, SWA_HEADS), 1.0),
        "w_even_out": nrm((N_EVEN, D_MODEL, D_MODEL), BETA * D_MODEL ** -0.5),
        "hg_lb_logits": nrm((DEPTH, D_MODEL), 0.5),
        "w_odd_in": nrm((N_ODD, D_MODEL, 4 * D_MODEL), D_MODEL ** -0.5),
        "hg_norm_g": 1.0 + nrm((N_ODD, D_MODEL), 0.02),
        "w_odd_out": nrm((N_ODD, D_MODEL, D_MODEL), BETA * D_MODEL ** -0.5),
        "w_mem_q": nrm((DEPTH, D_MODEL, D_MEM), D_MODEL ** -0.5),
        "w_mem_k": nrm((DEPTH, D_MODEL, D_MEM), D_MODEL ** -0.5),
        "w_mem_v": nrm((DEPTH, D_MODEL, D_MEM), D_MODEL ** -0.5),
        "w_mem_o": nrm((DEPTH, D_MEM, D_MODEL), BETA * D_MEM ** -0.5),
        "w_ffn_gate": nrm((DEPTH, D_MODEL, D_FF), D_MODEL ** -0.5),
        "w_ffn_up": nrm((DEPTH, D_MODEL, D_FF), D_MODEL ** -0.5),
        "w_ffn_down": nrm((DEPTH, D_FF, D_MODEL), BETA * D_FF ** -0.5),
        "ln_g": 1.0 + nrm((DEPTH, 3, D_MODEL), 0.02),
        "ln_b": nrm((DEPTH, 3, D_MODEL), 0.02),
    }


def reference(x_prompt, x_sample, cache_mem_k, cache_mem_v, cache_swa_k, cache_swa_v,
              state_s5_re, state_s5_im, state_hgrn, mem_prompt, rel_bias,
              w_even_in, s5_lam_re, s5_lam_im, s5_log_dt, s5_b_re, s5_b_im, s5_c_re, s5_c_im,
              s5_d, s5_w_glu, swa_sinks, w_even_out, hg_lb_logits, w_odd_in, hg_norm_g, w_odd_out,
              w_mem_q, w_mem_k, w_mem_v, w_mem_o, w_ffn_gate, w_ffn_up, w_ffn_down, ln_g, ln_b):
    f32 = jnp.float32
    w_buf = cache_swa_k.shape[2]
    lb_soft = jax.nn.softmax(hg_lb_logits.astype(f32), axis=0)
    lower_bounds = jnp.cumsum(lb_soft, axis=0) - lb_soft[0]

    yp, ys = x_prompt, x_sample
    p_mem_k, p_mem_v = [], []
    p_swa_k, p_swa_v, p_s5_re, p_s5_im, p_hg = [], [], [], [], []
    s_swa_k, s_swa_v, s_s5_re, s_s5_im, s_hg = [], [], [], [], []

    for l in range(DEPTH):
        j = l // 2
        if l % 2 == 0:
            a_re, a_im, bb_re, bb_im = s5_discretize(s5_lam_re[j], s5_lam_im[j], s5_log_dt[j],
                                                     s5_b_re[j], s5_b_im[j])
            s5p = (a_re, a_im, bb_re, bb_im, s5_c_re[j], s5_c_im[j], s5_d[j], s5_w_glu[j])
            h0 = jnp.zeros((yp.shape[0], S5_GROUPS, S5_STATE), f32)
            mix_p, hr, hi, nk, nv = even_mixer(yp, h0, h0, None, None, w_even_in[j], s5p,
                                               swa_sinks[j], rel_bias, w_even_out[j], w_buf)
            p_s5_re.append(hr); p_s5_im.append(hi); p_swa_k.append(nk); p_swa_v.append(nv)
            mix_s, hr, hi, nk, nv = even_mixer(ys, state_s5_re[j], state_s5_im[j], cache_swa_k[j],
                                               cache_swa_v[j], w_even_in[j], s5p, swa_sinks[j],
                                               rel_bias, w_even_out[j], w_buf)
            s_s5_re.append(hr); s_s5_im.append(hi); s_swa_k.append(nk); s_swa_v.append(nv)
        else:
            s0 = jnp.zeros((yp.shape[0], HG_HEADS, HG_DK, HG_DV), f32)
            mix_p, sp = odd_mixer(yp, s0, lower_bounds[l], w_odd_in[j], hg_norm_g[j], w_odd_out[j])
            p_hg.append(sp)
            mix_s, ss = odd_mixer(ys, state_hgrn[j], lower_bounds[l], w_odd_in[j], hg_norm_g[j], w_odd_out[j])
            s_hg.append(ss)
        yp = layer_norm(ALPHA * yp + mix_p, ln_g[l, 0], ln_b[l, 0])
        ys = layer_norm(ALPHA * ys + mix_s, ln_g[l, 0], ln_b[l, 0])

        mk, mv = memory_kv(mem_prompt, w_mem_k[l], w_mem_v[l])
        p_mem_k.append(mk); p_mem_v.append(mv)
        yp = layer_norm(ALPHA * yp + cross_attn(yp, mk, mv, w_mem_q[l], w_mem_o[l]), ln_g[l, 1], ln_b[l, 1])
        ys = layer_norm(ALPHA * ys + cross_attn(ys, cache_mem_k[l], cache_mem_v[l], w_mem_q[l], w_mem_o[l]),
                        ln_g[l, 1], ln_b[l, 1])

        yp = layer_norm(ALPHA * yp + swiglu(yp, w_ffn_gate[l], w_ffn_up[l], w_ffn_down[l]), ln_g[l, 2], ln_b[l, 2])
        ys = layer_norm(ALPHA * ys + swiglu(ys, w_ffn_gate[l], w_ffn_up[l], w_ffn_down[l]), ln_g[l, 2], ln_b[l, 2])

    return (yp, ys,
            jnp.stack(p_mem_k), jnp.stack(p_mem_v),
            jnp.stack(p_swa_k), jnp.stack(p_swa_v),
            jnp.stack(p_s5_re), jnp.stack(p_s5_im), jnp.stack(p_hg),
            jnp.stack(s_swa_k), jnp.stack(s_swa_v),
            jnp.stack(s_s5_re), jnp.stack(s_s5_im), jnp.stack(s_hg))
```

```python
import functools
import math

import jax
import jax.numpy as jnp
from jax import lax
from jax.experimental import pallas as pl
from jax.experimental.pallas import tpu as pltpu

F32 = jnp.float32
BF16 = jnp.bfloat16
SDS = jax.ShapeDtypeStruct

D_MODEL = 4096
DEPTH = 2
ALPHA = (2 * DEPTH) ** 0.25
LN_EPS = 1e-5
RMS_EPS = 1e-6
NEG_BIG = -1e30

S5_GROUPS, S5_GROUP, S5_STATE = 128, 16, 64
D_S5 = S5_GROUPS * S5_GROUP
S5_GB = 16
S5_NGB = S5_GROUPS // S5_GB
S5_UW = S5_GB * S5_GROUP
S5_HW = S5_GB * S5_STATE

HEAD_DIM, SWA_HEADS, SWA_KV_HEADS, SWA_GQ = 64, 32, 8, 4
D_SWA_Q, D_SWA_KV = 2048, 512
WINDOW = 128
N_BUCKETS, MAX_DISTANCE = 32, 128

HG_DK, HG_HEADS, HG_CHUNK = 128, 32, 32
N_MEM, MEM_HEADS, MEM_HEAD_DIM, D_MEM = 256, 4, 128, 512

VMEM_LIMIT = 56 * 1024 * 1024


def _cparams(n_axes, vmem=VMEM_LIMIT):
    return pltpu.CompilerParams(dimension_semantics=("arbitrary",) * n_axes, vmem_limit_bytes=vmem)


def _dot(a, b):
    return jnp.dot(a, b, preferred_element_type=F32)


def _dot_nt(a, b):
    return lax.dot_general(a, b, (((1,), (1,)), ((), ())), preferred_element_type=F32)


def _dot_tn(a, b):
    return lax.dot_general(a, b, (((0,), (0,)), ((), ())), preferred_element_type=F32)


def _layer_norm_rows(s, g, b):
    mu = jnp.mean(s, axis=-1, keepdims=True)
    c = s - mu
    var = jnp.mean(c * c, axis=-1, keepdims=True)
    return c * lax.rsqrt(var + LN_EPS) * g + b


def _mm_kernel(x_ref, w_ref, o_ref):
    o_ref[...] = _dot(x_ref[...], w_ref[...]).astype(o_ref.dtype)


def matmul(x, w, *, bm, bn, out_dtype=F32):
    m, k = x.shape
    n = w.shape[1]
    return pl.pallas_call(
        _mm_kernel,
        grid=(m // bm, n // bn),
        in_specs=[pl.BlockSpec((bm, k), lambda i, j: (i, 0)), pl.BlockSpec((k, bn), lambda i, j: (0, j))],
        out_specs=pl.BlockSpec((bm, bn), lambda i, j: (i, j)),
        out_shape=SDS((m, n), out_dtype),
        compiler_params=_cparams(2),
        name="matmul",
    )(x, w)


def _glu_kernel(z_ref, zt_ref, w_ref, o_ref):
    a = _dot(z_ref[...], w_ref[...])
    o_ref[...] = (zt_ref[...].astype(F32) * jax.nn.sigmoid(a)).astype(o_ref.dtype)


def glu(z, w, *, bm, bn):
    m, k = z.shape
    n = w.shape[1]
    return pl.pallas_call(
        _glu_kernel,
        grid=(m // bm, n // bn),
        in_specs=[pl.BlockSpec((bm, k), lambda i, j: (i, 0)), pl.BlockSpec((bm, bn), lambda i, j: (i, j)),
                  pl.BlockSpec((k, bn), lambda i, j: (0, j))],
        out_specs=pl.BlockSpec((bm, bn), lambda i, j: (i, j)),
        out_shape=SDS((m, n), BF16),
        compiler_params=_cparams(2),
        name="glu",
    )(z, z, w)


def _proj_ln_kernel(*refs, n_in):
    xs = refs[:n_in]
    w_ref, res_ref, g_ref, b_ref, y_ref, yb_ref = refs[n_in:]
    acc = None
    off = 0
    for x_ref in xs:
        kx = x_ref.shape[1]
        part = _dot(x_ref[...], w_ref[off:off + kx, :])
        acc = part if acc is None else acc + part
        off += kx
    y = _layer_norm_rows(ALPHA * res_ref[...] + acc, g_ref[...], b_ref[...])
    y_ref[...] = y
    yb_ref[...] = y.astype(BF16)


def proj_res_ln(xs, w, res, g, b, *, bm):
    m, d = res.shape
    k = w.shape[0]
    row = lambda i: (i, 0)
    fixed = lambda i: (0, 0)
    return pl.pallas_call(
        functools.partial(_proj_ln_kernel, n_in=len(xs)),
        grid=(m // bm,),
        in_specs=[pl.BlockSpec((bm, x.shape[1]), row) for x in xs] + [
            pl.BlockSpec((k, d), fixed, pipeline_mode=pl.Buffered(1)),
            pl.BlockSpec((bm, d), row),
            pl.BlockSpec((1, d), fixed),
            pl.BlockSpec((1, d), fixed),
        ],
        out_specs=[pl.BlockSpec((bm, d), row), pl.BlockSpec((bm, d), row)],
        out_shape=[SDS((m, d), F32), SDS((m, d), BF16)],
        compiler_params=_cparams(1),
        name="proj_res_ln",
    )(*xs, w, res, g.reshape(1, d), b.reshape(1, d))


def _hg_out_kernel(o_ref, gate_ref, ng_ref, w_ref, res_ref, g_ref, b_ref, y_ref, yb_ref):
    o = o_ref[...]
    xn = o * lax.rsqrt(jnp.mean(o * o, axis=-1, keepdims=True) + RMS_EPS) * ng_ref[...]
    xn = xn * jax.nn.sigmoid(gate_ref[...])
    mix = _dot(xn.astype(BF16), w_ref[...])
    y = _layer_norm_rows(ALPHA * res_ref[...] + mix, g_ref[...], b_ref[...])
    y_ref[...] = y
    yb_ref[...] = y.astype(BF16)


def hg_out_ln(o, h_all, norm_g, w, res, g, b, *, bm):
    m, d = res.shape
    row = lambda i: (i, 0)
    fixed = lambda i: (0, 0)
    return pl.pallas_call(
        _hg_out_kernel,
        grid=(m // bm,),
        in_specs=[
            pl.BlockSpec((bm, d), row),
            pl.BlockSpec((bm, d), lambda i: (i, 3)),
            pl.BlockSpec((1, d), fixed),
            pl.BlockSpec((d, d), fixed, pipeline_mode=pl.Buffered(1)),
            pl.BlockSpec((bm, d), row),
            pl.BlockSpec((1, d), fixed),
            pl.BlockSpec((1, d), fixed),
        ],
        out_specs=[pl.BlockSpec((bm, d), row), pl.BlockSpec((bm, d), row)],
        out_shape=[SDS((m, d), F32), SDS((m, d), BF16)],
        compiler_params=_cparams(1),
        name="hg_out_ln",
    )(o, h_all, norm_g.reshape(1, d), w, res, g.reshape(1, d), b.reshape(1, d))


def _ffn_kernel(x_ref, wg_ref, wu_ref, wd_ref, g_ref, b_ref, y_ref, yb_ref, xb_ref):
    f = pl.program_id(1)

    @pl.when(f == 0)
    def _():
        x = x_ref[...]
        xb_ref[...] = x.astype(BF16)
        y_ref[...] = ALPHA * x

    xb = xb_ref[...]
    gate = _dot(xb, wg_ref[...])
    up = _dot(xb, wu_ref[...])
    h = (gate * jax.nn.sigmoid(gate) * up).astype(BF16)
    y_ref[...] += _dot(h, wd_ref[...])

    @pl.when(f == pl.num_programs(1) - 1)
    def _():
        y = _layer_norm_rows(y_ref[...], g_ref[...], b_ref[...])
        y_ref[...] = y
        yb_ref[...] = y.astype(BF16)


def ffn_ln(x, wg, wu, wd, g, b, *, bm, bf=256):
    m, d = x.shape
    dff = wg.shape[1]
    one = pl.Buffered(1)
    return pl.pallas_call(
        _ffn_kernel,
        grid=(m // bm, dff // bf),
        in_specs=[
            pl.BlockSpec((bm, d), lambda i, f: (i, 0), pipeline_mode=one),
            pl.BlockSpec((d, bf), lambda i, f: (0, f)),
            pl.BlockSpec((d, bf), lambda i, f: (0, f)),
            pl.BlockSpec((bf, d), lambda i, f: (f, 0)),
            pl.BlockSpec((1, d), lambda i, f: (0, 0)),
            pl.BlockSpec((1, d), lambda i, f: (0, 0)),
        ],
        out_specs=[
            pl.BlockSpec((bm, d), lambda i, f: (i, 0), pipeline_mode=one),
            pl.BlockSpec((bm, d), lambda i, f: (i, 0), pipeline_mode=one),
        ],
        out_shape=[SDS((m, d), F32), SDS((m, d), BF16)],
        scratch_shapes=[pltpu.VMEM((bm, d), BF16)],
        compiler_params=_cparams(2),
        name="ffn_ln",
    )(x, wg, wu, wd, g.reshape(1, d), b.reshape(1, d))


def _cmul(ar, ai, br, bi):
    return ar * br - ai * bi, ar * bi + ai * br


def _s5_coefs(a_r, a_i, period):
    l = a_r.shape[1]
    pows = [(a_r, a_i)]
    for _ in range(period - 1):
        pows.append(_cmul(pows[-1][0], pows[-1][1], a_r, a_i))
    t = lax.broadcasted_iota(jnp.int32, (8, l), 0) & (period - 1)
    shifts = []
    k = 1
    while k < period:
        keep = t >= k
        shifts.append((jnp.where(keep, pows[k - 1][0], 0.0), jnp.where(keep, pows[k - 1][1], 0.0)))
        k *= 2
    p_r = jnp.broadcast_to(pows[0][0], (8, l))
    p_i = jnp.broadcast_to(pows[0][1], (8, l))
    for j in range(1, period):
        p_r = jnp.where(t == j, pows[j][0], p_r)
        p_i = jnp.where(t == j, pows[j][1], p_i)
    return shifts, (p_r, p_i)


def _s5_scan_tile(x_r, x_i, shifts):
    k = 1
    for c_r, c_i in shifts:
        s_r = pltpu.roll(x_r, k, 0)
        s_i = pltpu.roll(x_i, k, 0)
        x_r, x_i = x_r + c_r * s_r - c_i * s_i, x_i + c_r * s_i + c_i * s_r
        k *= 2
    return x_r, x_i


def _s5_prompt_kernel(u_ref, bw_ref, cw_ref, ar_ref, ai_ref, d_ref, z_ref, hfin_ref, bu_ref, carry_ref):
    tt = pl.program_id(2)
    n_rows = bu_ref.shape[0]

    @pl.when(tt == 0)
    def _():
        carry_ref[...] = jnp.zeros_like(carry_ref)

    u = u_ref[0]
    bu_ref[...] = _dot(u.astype(BF16), bw_ref[0])
    shifts, (p_r, p_i) = _s5_coefs(ar_ref[0], ai_ref[0], 8)

    def body(i, carry):
        h_r, h_i = carry
        r0 = pl.multiple_of(i * 8, 8)
        x_r, x_i = _s5_scan_tile(bu_ref[pl.ds(r0, 8), 0:S5_HW], bu_ref[pl.ds(r0, 8), S5_HW:2 * S5_HW], shifts)
        x_r, x_i = x_r + p_r * h_r - p_i * h_i, x_i + p_r * h_i + p_i * h_r
        bu_ref[pl.ds(r0, 8), 0:S5_HW] = x_r
        bu_ref[pl.ds(r0, 8), S5_HW:2 * S5_HW] = x_i
        return x_r[7:8], x_i[7:8]

    h_r, h_i = lax.fori_loop(0, n_rows // 8, body, (carry_ref[:, 0:S5_HW], carry_ref[:, S5_HW:2 * S5_HW]))
    carry_ref[:, 0:S5_HW] = h_r
    carry_ref[:, S5_HW:2 * S5_HW] = h_i
    y = _dot(bu_ref[...].astype(BF16), cw_ref[0]) + d_ref[0] * u
    z_ref[0] = jax.nn.gelu(y).astype(z_ref.dtype)

    @pl.when(tt == pl.num_programs(2) - 1)
    def _():
        hfin_ref[0, 0] = carry_ref[...]


def s5_prompt(h_all, bw, cw, a_r, a_i, d_skip, *, tb):
    bsz, t, _ = h_all.shape
    blk = lambda b, g, s: (g, 0, 0)
    return pl.pallas_call(
        _s5_prompt_kernel,
        grid=(bsz, S5_NGB, t // tb),
        in_specs=[
            pl.BlockSpec((1, tb, S5_UW), lambda b, g, s: (b, s, g)),
            pl.BlockSpec((1, S5_UW, 2 * S5_HW), blk),
            pl.BlockSpec((1, 2 * S5_HW, S5_UW), blk),
            pl.BlockSpec((1, 1, S5_HW), blk),
            pl.BlockSpec((1, 1, S5_HW), blk),
            pl.BlockSpec((1, 1, S5_UW), blk),
        ],
        out_specs=[
            pl.BlockSpec((1, tb, S5_UW), lambda b, g, s: (b, s, g)),
            pl.BlockSpec((1, 1, 1, 2 * S5_HW), lambda b, g, s: (b, g, 0, 0)),
        ],
        out_shape=[SDS((bsz, t, D_S5), BF16), SDS((bsz, S5_NGB, 1, 2 * S5_HW), F32)],
        scratch_shapes=[pltpu.VMEM((tb, 2 * S5_HW), F32), pltpu.VMEM((1, 2 * S5_HW), F32)],
        compiler_params=_cparams(3),
        name="s5_prompt",
    )(h_all, bw, cw, a_r, a_i, d_skip)


def _s5_sample_kernel(u_ref, h0_ref, bw_ref, cw_ref, ar_ref, ai_ref, d_ref, z_ref, h_ref, *, period):
    u = u_ref[...]
    bu = _dot(u.astype(BF16), bw_ref[0])
    shifts, (p_r, p_i) = _s5_coefs(ar_ref[0], ai_ref[0], period)
    for i in range(u.shape[0] // 8):
        rows = slice(i * 8, (i + 1) * 8)
        x_r, x_i = _s5_scan_tile(bu[rows, 0:S5_HW], bu[rows, S5_HW:2 * S5_HW], shifts)
        h0_r = h0_ref[rows, 0:S5_HW]
        h0_i = h0_ref[rows, S5_HW:2 * S5_HW]
        h_ref[rows, 0:S5_HW] = x_r + p_r * h0_r - p_i * h0_i
        h_ref[rows, S5_HW:2 * S5_HW] = x_i + p_r * h0_i + p_i * h0_r
    y = _dot(h_ref[...].astype(BF16), cw_ref[0]) + d_ref[0] * u
    z_ref[...] = jax.nn.gelu(y).astype(z_ref.dtype)


def s5_sample(h_all, h0x, bw, cw, a_r, a_i, d_skip, *, period):
    m = h_all.shape[0]
    blk = lambda g: (g, 0, 0)
    return pl.pallas_call(
        functools.partial(_s5_sample_kernel, period=period),
        grid=(S5_NGB,),
        in_specs=[
            pl.BlockSpec((m, S5_UW), lambda g: (0, g)),
            pl.BlockSpec((m, 2 * S5_HW), lambda g: (0, g)),
            pl.BlockSpec((1, S5_UW, 2 * S5_HW), blk),
            pl.BlockSpec((1, 2 * S5_HW, S5_UW), blk),
            pl.BlockSpec((1, 1, S5_HW), blk),
            pl.BlockSpec((1, 1, S5_HW), blk),
            pl.BlockSpec((1, 1, S5_UW), blk),
        ],
        out_specs=[pl.BlockSpec((m, S5_UW), lambda g: (0, g)), pl.BlockSpec((m, 2 * S5_HW), lambda g: (0, g))],
        out_shape=[SDS((m, D_S5), BF16), SDS((m, S5_NGB * 2 * S5_HW), F32)],
        compiler_params=_cparams(1),
        name="s5_sample",
    )(h_all, h0x, bw, cw, a_r, a_i, d_skip)


def _bias_kernel(rb_ref, o_ref):
    h = pl.program_id(0)
    q = lax.broadcasted_iota(jnp.int32, (WINDOW, 2 * WINDOW), 0)
    k = lax.broadcasted_iota(jnp.int32, (WINDOW, 2 * WINDOW), 1)
    n = jnp.maximum(q + WINDOW - k, 0)
    max_exact = N_BUCKETS // 2
    nf = jnp.maximum(n, 1).astype(F32)
    large = max_exact + (jnp.log(nf / max_exact) / math.log(MAX_DISTANCE / max_exact)
                         * (N_BUCKETS - max_exact)).astype(jnp.int32)
    large = jnp.minimum(large, N_BUCKETS - 1)
    bucket = jnp.where(n < max_exact, n, large)
    out = jnp.zeros((WINDOW, 2 * WINDOW), F32)
    for b in range(N_BUCKETS):
        out = jnp.where(bucket == b, rb_ref[b, h], out)
    o_ref[0] = out


def bias_table(rel_bias):
    return pl.pallas_call(
        _bias_kernel,
        grid=(SWA_HEADS,),
        in_specs=[pl.BlockSpec(memory_space=pltpu.SMEM)],
        out_specs=pl.BlockSpec((1, WINDOW, 2 * WINDOW), lambda h: (h, 0, 0)),
        out_shape=SDS((SWA_HEADS, WINDOW, 2 * WINDOW), F32),
        compiler_params=_cparams(1),
        name="bias_table",
    )(rel_bias)


def _swa_softmax(s, valid, sink_col):
    s = jnp.where(valid, s, NEG_BIG)
    m = jnp.maximum(jnp.max(s, axis=-1, keepdims=True), sink_col)
    p = jnp.exp(s - m)
    return p / (jnp.sum(p, axis=-1, keepdims=True) + jnp.exp(sink_col - m))


def _sink_col(sinks_ref, first_head, rows_per_head):
    n = SWA_GQ * rows_per_head
    r = lax.broadcasted_iota(jnp.int32, (n, 1), 0)
    col = jnp.full((n, 1), sinks_ref[first_head], F32)
    for g in range(1, SWA_GQ):
        col = jnp.where(r >= g * rows_per_head, sinks_ref[first_head + g], col)
    return col


def _swa_prompt_kernel(sinks_ref, q_ref, kp_ref, kc_ref, vp_ref, vc_ref, bias_ref, o_ref):
    pair = pl.program_id(0)
    i = pl.program_id(2)
    blk = WINDOW
    q = q_ref[0].astype(BF16)
    kk = jnp.concatenate([kp_ref[0], kc_ref[0]], axis=0).astype(BF16)
    vv = jnp.concatenate([vp_ref[0], vc_ref[0]], axis=0).astype(BF16)
    qi = lax.broadcasted_iota(jnp.int32, (SWA_GQ * blk, 2 * blk), 0) & (blk - 1)
    ki = lax.broadcasted_iota(jnp.int32, (SWA_GQ * blk, 2 * blk), 1)
    valid = (ki > qi) & (ki <= qi + blk) & ((ki >= blk) | (i > 0))
    outs = []
    for j in range(2):
        k_j = kk[:, j * HEAD_DIM:(j + 1) * HEAD_DIM]
        v_j = vv[:, j * HEAD_DIM:(j + 1) * HEAD_DIM]
        q4 = jnp.concatenate([q[:, (j * SWA_GQ + g) * HEAD_DIM:(j * SWA_GQ + g + 1) * HEAD_DIM]
                              for g in range(SWA_GQ)], axis=0)
        s = _dot_nt(q4, k_j) * (HEAD_DIM ** -0.5)
        s = s + bias_ref[j * SWA_GQ:(j + 1) * SWA_GQ].reshape(SWA_GQ * blk, 2 * blk)
        p = _swa_softmax(s, valid, _sink_col(sinks_ref, pair * 2 * SWA_GQ + j * SWA_GQ, blk))
        o4 = _dot(p.astype(BF16), v_j)
        outs += [o4[g * blk:(g + 1) * blk] for g in range(SWA_GQ)]
    o_ref[0] = jnp.concatenate(outs, axis=1).astype(o_ref.dtype)


def swa_prompt(h_all, sinks, table):
    bsz, t, _ = h_all.shape
    nblk = t // WINDOW
    qw = 2 * SWA_GQ * HEAD_DIM
    kw = 2 * HEAD_DIM
    q0, k0, v0 = D_S5 // qw, (D_S5 + D_SWA_Q) // kw, (D_S5 + D_SWA_Q + D_SWA_KV) // kw
    cur = lambda c0: (lambda p, b, i: (b, i, c0 + p))
    prev = lambda c0: (lambda p, b, i: (b, jnp.maximum(i - 1, 0), c0 + p))
    return pl.pallas_call(
        _swa_prompt_kernel,
        grid=(SWA_KV_HEADS // 2, bsz, nblk),
        in_specs=[
            pl.BlockSpec(memory_space=pltpu.SMEM),
            pl.BlockSpec((1, WINDOW, qw), cur(q0)),
            pl.BlockSpec((1, WINDOW, kw), prev(k0)),
            pl.BlockSpec((1, WINDOW, kw), cur(k0)),
            pl.BlockSpec((1, WINDOW, kw), prev(v0)),
            pl.BlockSpec((1, WINDOW, kw), cur(v0)),
            pl.BlockSpec((2 * SWA_GQ, WINDOW, 2 * WINDOW), lambda p, b, i: (p, 0, 0)),
        ],
        out_specs=pl.BlockSpec((1, WINDOW, qw), lambda p, b, i: (b, i, p)),
        out_shape=SDS((bsz, t, D_SWA_Q), BF16),
        compiler_params=_cparams(3),
        name="swa_prompt",
    )(sinks, h_all, h_all, h_all, h_all, h_all, table)


def _swa_sample_kernel(sinks_ref, q_ref, kk_ref, vv_ref, bias_ref, o_ref, *, t_new):
    rows = q_ref.shape[0]
    n_keys = kk_ref.shape[1]
    q = q_ref[...].astype(BF16)
    r = lax.broadcasted_iota(jnp.int32, (SWA_GQ * rows, n_keys), 0)
    ti = r & (t_new - 1)
    ki = lax.broadcasted_iota(jnp.int32, (SWA_GQ * rows, n_keys), 1)
    valid = (ki > ti) & (ki <= ti + WINDOW)
    second = (r & (rows - 1)) >= t_new
    second_o = (lax.broadcasted_iota(jnp.int32, (SWA_GQ * rows, HEAD_DIM), 0) & (rows - 1)) >= t_new
    outs = []
    for j in range(SWA_KV_HEADS):
        cols = slice(j * HEAD_DIM, (j + 1) * HEAD_DIM)
        q4 = jnp.concatenate([q[:, (j * SWA_GQ + g) * HEAD_DIM:(j * SWA_GQ + g + 1) * HEAD_DIM]
                              for g in range(SWA_GQ)], axis=0)
        s = jnp.where(second, _dot_nt(q4, kk_ref[1][:, cols].astype(BF16)),
                      _dot_nt(q4, kk_ref[0][:, cols].astype(BF16))) * (HEAD_DIM ** -0.5)
        s = s + bias_ref[j * SWA_GQ:(j + 1) * SWA_GQ].reshape(SWA_GQ * rows, n_keys)
        p = _swa_softmax(s, valid, _sink_col(sinks_ref, j * SWA_GQ, rows)).astype(BF16)
        o4 = jnp.where(second_o, _dot(p, vv_ref[1][:, cols].astype(BF16)), _dot(p, vv_ref[0][:, cols].astype(BF16)))
        outs += [o4[g * rows:(g + 1) * rows] for g in range(SWA_GQ)]
    o_ref[...] = jnp.concatenate(outs, axis=1).astype(o_ref.dtype)


def swa_sample(h_all, kk, vv, sinks, bias_s, *, t_new):
    m = h_all.shape[0]
    n_keys = kk.shape[1]
    rows = 2 * t_new
    return pl.pallas_call(
        functools.partial(_swa_sample_kernel, t_new=t_new),
        grid=(m // rows,),
        in_specs=[
            pl.BlockSpec(memory_space=pltpu.SMEM),
            pl.BlockSpec((rows, D_SWA_Q), lambda i: (i, D_S5 // D_SWA_Q)),
            pl.BlockSpec((2, n_keys, D_SWA_KV), lambda i: (i, 0, 0)),
            pl.BlockSpec((2, n_keys, D_SWA_KV), lambda i: (i, 0, 0)),
            pl.BlockSpec((SWA_HEADS, rows, n_keys), lambda i: (0, 0, 0)),
        ],
        out_specs=pl.BlockSpec((rows, D_SWA_Q), lambda i: (i, 0)),
        out_shape=SDS((m, D_SWA_Q), BF16),
        compiler_params=_cparams(1),
        name="swa_sample",
    )(sinks, h_all, kk, vv, bias_s)


def _hg_gates(q, fz, lb):
    qs = q * jax.nn.sigmoid(q)
    f = lb + (1.0 - lb) * jax.nn.sigmoid(fz)
    return qs, jnp.log(f), 1.0 - f


def _hgrn_prompt_kernel(q_ref, f_ref, v_ref, lb_ref, o_ref, st_ref, s_scr, *, n_heads):
    tt = pl.program_id(2)
    n_rows = q_ref.shape[1]
    c = HG_CHUNK

    @pl.when(tt == 0)
    def _():
        s_scr[...] = jnp.zeros_like(s_scr)

    ri = lax.broadcasted_iota(jnp.int32, (c, c), 0)
    ci = lax.broadcasted_iota(jnp.int32, (c, c), 1)
    causal = ri >= ci
    tril = causal.astype(F32)

    def body(ch, _):
        r0 = pl.multiple_of(ch * c, c)
        for hd in range(n_heads):
            cols = slice(hd * HG_DK, (hd + 1) * HG_DK)
            qs, lf, k = _hg_gates(q_ref[0, pl.ds(r0, c), cols], f_ref[0, pl.ds(r0, c), cols], lb_ref[:, cols])
            v = v_ref[0, pl.ds(r0, c), cols].astype(BF16)
            cum = jnp.dot(tril, lf, preferred_element_type=F32, precision=lax.Precision.HIGHEST)
            last = cum[c - 1:c]
            qt = (qs * jnp.exp(cum)).astype(BF16)
            kt = (k * jnp.exp(-cum)).astype(BF16)
            kl = (k * jnp.exp(last - cum)).astype(BF16)
            attn = jnp.where(causal, _dot_nt(qt, kt), 0.0).astype(BF16)
            st = s_scr[hd]
            o_ref[0, pl.ds(r0, c), cols] = _dot(attn, v) + _dot_nt(qt, st.astype(BF16))
            s_scr[hd] = st * jnp.exp(last) + _dot_tn(v, kl)
        return 0

    lax.fori_loop(0, n_rows // c, body, 0)

    @pl.when(tt == pl.num_programs(2) - 1)
    def _():
        st_ref[0] = s_scr[...]


def hgrn_prompt(h_all, lb, *, tb, n_heads=2):
    bsz, t, _ = h_all.shape
    hw = n_heads * HG_DK
    nhb = D_MODEL // hw
    col = lambda c0: (lambda b, h, s: (b, s, c0 * nhb + h))
    return pl.pallas_call(
        functools.partial(_hgrn_prompt_kernel, n_heads=n_heads),
        grid=(bsz, nhb, t // tb),
        in_specs=[
            pl.BlockSpec((1, tb, hw), col(0)),
            pl.BlockSpec((1, tb, hw), col(1)),
            pl.BlockSpec((1, tb, hw), col(2)),
            pl.BlockSpec((1, hw), lambda b, h, s: (0, h)),
        ],
        out_specs=[
            pl.BlockSpec((1, tb, hw), lambda b, h, s: (b, s, h)),
            pl.BlockSpec((1, n_heads, HG_DK, HG_DK), lambda b, h, s: (b, h, 0, 0)),
        ],
        out_shape=[SDS((bsz, t, D_MODEL), F32), SDS((bsz, HG_HEADS, HG_DK, HG_DK), F32)],
        scratch_shapes=[pltpu.VMEM((n_heads, HG_DK, HG_DK), F32)],
        compiler_params=_cparams(3),
        name="hgrn_prompt",
    )(h_all, h_all, h_all, lb.reshape(1, D_MODEL))


def _hgrn_sample_kernel(q_ref, f_ref, v_ref, lb_ref, s0_ref, o_ref, s_ref, *, n_heads, t_new):
    rows = q_ref.shape[0]
    ri = lax.broadcasted_iota(jnp.int32, (rows, rows), 0)
    ci = lax.broadcasted_iota(jnp.int32, (rows, rows), 1)
    causal = (ri >= ci) & ((ri >= t_new) == (ci >= t_new))
    tril = causal.astype(F32)
    second = lax.broadcasted_iota(jnp.int32, (rows, HG_DK), 0) >= t_new
    eye = (lax.broadcasted_iota(jnp.int32, (HG_DK, HG_DK), 0)
           == lax.broadcasted_iota(jnp.int32, (HG_DK, HG_DK), 1))
    for hd in range(n_heads):
        cols = slice(hd * HG_DK, (hd + 1) * HG_DK)
        qs, lf, k = _hg_gates(q_ref[:, cols], f_ref[:, cols], lb_ref[:, cols])
        v = v_ref[:, cols].astype(BF16)
        cum = jnp.dot(tril, lf, preferred_element_type=F32, precision=lax.Precision.HIGHEST)
        last0 = cum[t_new - 1:t_new]
        last1 = cum[rows - 1:rows]
        last = jnp.where(second, last1, last0)
        qt = (qs * jnp.exp(cum)).astype(BF16)
        kt = (k * jnp.exp(-cum)).astype(BF16)
        kl = k * jnp.exp(last - cum)
        attn = jnp.where(causal, _dot_nt(qt, kt), 0.0).astype(BF16)
        s0a = s0_ref[0, hd]
        s0b = s0_ref[1, hd]
        o_ref[:, cols] = _dot(attn, v) + jnp.where(second, _dot(qt, s0b.astype(BF16)), _dot(qt, s0a.astype(BF16)))
        for bb, (s0, lst) in enumerate(((s0a, last0), (s0b, last1))):
            decay = jnp.sum(jnp.where(eye, jnp.exp(lst), 0.0), axis=1, keepdims=True)
            kl_b = jnp.where(second == (bb == 1), kl, 0.0).astype(BF16)
            s_ref[bb, hd] = decay * s0 + _dot_tn(kl_b, v)


def hgrn_sample(h_all, lb, s0, *, t_new, n_heads=8):
    m = h_all.shape[0]
    rows = 2 * t_new
    hw = n_heads * HG_DK
    nhb = D_MODEL // hw
    col = lambda c0: (lambda i, h: (i, c0 * nhb + h))
    return pl.pallas_call(
        functools.partial(_hgrn_sample_kernel, n_heads=n_heads, t_new=t_new),
        grid=(m // rows, nhb),
        in_specs=[
            pl.BlockSpec((rows, hw), col(0)),
            pl.BlockSpec((rows, hw), col(1)),
            pl.BlockSpec((rows, hw), col(2)),
            pl.BlockSpec((1, hw), lambda i, h: (0, h)),
            pl.BlockSpec((2, n_heads, HG_DK, HG_DK), lambda i, h: (i, h, 0, 0)),
        ],
        out_specs=[
            pl.BlockSpec((rows, hw), lambda i, h: (i, h)),
            pl.BlockSpec((2, n_heads, HG_DK, HG_DK), lambda i, h: (i, h, 0, 0)),
        ],
        out_shape=[SDS((m, D_MODEL), F32), SDS(s0.shape, F32)],
        compiler_params=_cparams(2),
        name="hgrn_sample",
    )(h_all, h_all, h_all, lb.reshape(1, D_MODEL), s0)


def _xattn_kernel(y_ref, yb_ref, wq_ref, mk_ref, mv_ref, wo_ref, g_ref, b_ref, o_ref, ob_ref, *, n_seq):
    rows = y_ref.shape[0]
    q = _dot(yb_ref[...], wq_ref[...]).astype(BF16)
    second = lax.broadcasted_iota(jnp.int32, (rows, 1), 0) >= rows // 2
    heads = []
    for h in range(MEM_HEADS):
        cols = slice(h * MEM_HEAD_DIM, (h + 1) * MEM_HEAD_DIM)
        qh = q[:, cols]
        s = _dot_nt(qh, mk_ref[0][:, cols].astype(BF16))
        if n_seq == 2:
            s = jnp.where(second, _dot_nt(qh, mk_ref[1][:, cols].astype(BF16)), s)
        s = s * (MEM_HEAD_DIM ** -0.5)
        e = jnp.exp(s - jnp.max(s, axis=-1, keepdims=True))
        p = (e / jnp.sum(e, axis=-1, keepdims=True)).astype(BF16)
        o = _dot(p, mv_ref[0][:, cols].astype(BF16))
        if n_seq == 2:
            o = jnp.where(second, _dot(p, mv_ref[1][:, cols].astype(BF16)), o)
        heads.append(o.astype(BF16))
    mix = _dot(jnp.concatenate(heads, axis=1), wo_ref[...])
    y = _layer_norm_rows(ALPHA * y_ref[...] + mix, g_ref[...], b_ref[...])
    o_ref[...] = y
    ob_ref[...] = y.astype(BF16)


def xattn_ln(y, yb, wq, mk, mv, wo, g, b, *, bm, rows_per_seq):
    m, d = y.shape
    n_seq = 2 if bm == 2 * rows_per_seq else 1
    assert n_seq == 2 or rows_per_seq % bm == 0
    row = lambda i: (i, 0)
    fixed = lambda i: (0, 0)
    mem = (lambda i: (i, 0, 0)) if n_seq == 2 else (lambda i: (i * bm // rows_per_seq, 0, 0))
    return pl.pallas_call(
        functools.partial(_xattn_kernel, n_seq=n_seq),
        grid=(m // bm,),
        in_specs=[
            pl.BlockSpec((bm, d), row),
            pl.BlockSpec((bm, d), row),
            pl.BlockSpec((d, D_MEM), fixed),
            pl.BlockSpec((n_seq, N_MEM, D_MEM), mem),
            pl.BlockSpec((n_seq, N_MEM, D_MEM), mem),
            pl.BlockSpec((D_MEM, d), fixed),
            pl.BlockSpec((1, d), fixed),
            pl.BlockSpec((1, d), fixed),
        ],
        out_specs=[pl.BlockSpec((bm, d), row), pl.BlockSpec((bm, d), row)],
        out_shape=[SDS((m, d), F32), SDS((m, d), BF16)],
        compiler_params=_cparams(1),
        name="xattn_ln",
    )(y, yb, wq, mk, mv, wo, g.reshape(1, d), b.reshape(1, d))


def _s5_discretize(lam_re, lam_im, log_dt, b_re, b_im):
    lr = jnp.minimum(lam_re.astype(F32), -1e-4)
    li = lam_im.astype(F32)
    dt = jnp.exp(log_dt.astype(F32))[:, None]
    mag = jnp.exp(lr * dt)
    a_re = mag * jnp.cos(li * dt)
    a_im = mag * jnp.sin(li * dt)
    den = lr * lr + li * li
    fr = ((a_re - 1.0) * lr + a_im * li) / den
    fi = (a_im * lr - (a_re - 1.0) * li) / den
    br, bi = b_re.astype(F32), b_im.astype(F32)
    bb_re = fr[..., None] * br - fi[..., None] * bi
    bb_im = fr[..., None] * bi + fi[..., None] * br
    return a_re, a_im, bb_re, bb_im


def _s5_block_weights(a_re, a_im, bb_re, bb_im, c_re, c_im, d_skip):
    eye = jnp.eye(S5_GB, dtype=F32)
    shp_b = (S5_NGB, S5_GB, S5_STATE, S5_GROUP)
    shp_c = (S5_NGB, S5_GB, S5_GROUP, S5_STATE)
    blk_b = lambda w: jnp.einsum('bgph,gk->bghkp', w.reshape(shp_b), eye).reshape(S5_NGB, S5_UW, S5_HW)
    blk_c = lambda w: jnp.einsum('bghp,gk->bkpgh', w.reshape(shp_c), eye).reshape(S5_NGB, S5_HW, S5_UW)
    bw = jnp.concatenate([blk_b(bb_re), blk_b(bb_im)], axis=-1).astype(BF16)
    cw = jnp.concatenate([blk_c(c_re.astype(F32)), blk_c(-c_im.astype(F32))], axis=1).astype(BF16)
    return (bw, cw, a_re.reshape(S5_NGB, 1, S5_HW), a_im.reshape(S5_NGB, 1, S5_HW),
            d_skip.astype(F32).reshape(S5_NGB, 1, S5_UW))


def _s5_state_to_blocks(s_re, s_im):
    b = s_re.shape[0]
    return jnp.concatenate([s_re.reshape(b, S5_NGB, S5_HW), s_im.reshape(b, S5_NGB, S5_HW)],
                           axis=-1).reshape(b, S5_NGB * 2 * S5_HW)


def _s5_state_from_blocks(h):
    b = h.shape[0]
    return (h[:, :, :S5_HW].reshape(b, S5_GROUPS, S5_STATE), h[:, :, S5_HW:].reshape(b, S5_GROUPS, S5_STATE))


def kernel(x_prompt, x_sample, cache_mem_k, cache_mem_v, cache_swa_k, cache_swa_v, state_s5_re, state_s5_im, state_hgrn, mem_prompt, rel_bias, w_even_in, s5_lam_re, s5_lam_im, s5_log_dt, s5_b_re, s5_b_im, s5_c_re, s5_c_im, s5_d, s5_w_glu, swa_sinks, w_even_out, hg_lb_logits, w_odd_in, hg_norm_g, w_odd_out, w_mem_q, w_mem_k, w_mem_v, w_mem_o, w_ffn_gate, w_ffn_up, w_ffn_down, ln_g, ln_b):
    bsz, seq, d = x_prompt.shape
    dec_b, dec_t, _ = x_sample.shape
    mp, ms = bsz * seq, dec_b * dec_t
    w_buf = cache_swa_k.shape[2]
    bf = lambda w: w.astype(BF16)

    lb_soft = jax.nn.softmax(hg_lb_logits.astype(F32), axis=0)
    lower_bounds = jnp.cumsum(lb_soft, axis=0) - lb_soft[0]
    table = bias_table(rel_bias.astype(F32))
    bias_s = jnp.tile(table[:, :dec_t], (1, 2, 1))
    mem_b = bf(mem_prompt).reshape(bsz * N_MEM, d)

    yp = x_prompt.reshape(mp, d)
    ys = x_sample.reshape(ms, d)
    ypb, ysb = bf(yp), bf(ys)
    p_mem_k, p_mem_v = [], []
    p_swa_k, p_swa_v, p_s5_re, p_s5_im, p_hg = [], [], [], [], []
    s_swa_k, s_swa_v, s_s5_re, s_s5_im, s_hg = [], [], [], [], []

    for l in range(DEPTH):
        j = l // 2
        g0, b0 = ln_g[l, 0], ln_b[l, 0]
        if l % 2 == 0:
            a_re, a_im, bb_re, bb_im = _s5_discretize(s5_lam_re[j], s5_lam_im[j], s5_log_dt[j], s5_b_re[j], s5_b_im[j])
            s5w = _s5_block_weights(a_re, a_im, bb_re, bb_im, s5_c_re[j], s5_c_im[j], s5_d[j])
            w_in, w_glu, w_out = bf(w_even_in[j]), bf(s5_w_glu[j]), bf(w_even_out[j])
            sinks = swa_sinks[j].astype(F32)
            kv0 = D_S5 + D_SWA_Q

            hp = matmul(ypb, w_in, bm=1024, bn=1024).reshape(bsz, seq, -1)
            z, hfin = s5_prompt(hp, *s5w, tb=512)
            s5_out = glu(z.reshape(mp, D_S5), w_glu, bm=1024, bn=1024)
            att = swa_prompt(hp, sinks, table).reshape(mp, D_SWA_Q)
            hr, hi = _s5_state_from_blocks(hfin[:, :, 0])
            p_s5_re.append(hr); p_s5_im.append(hi)
            p_swa_k.append(hp[:, seq - w_buf:, kv0:kv0 + D_SWA_KV].reshape(bsz, w_buf, SWA_KV_HEADS, HEAD_DIM))
            p_swa_v.append(hp[:, seq - w_buf:, kv0 + D_SWA_KV:].reshape(bsz, w_buf, SWA_KV_HEADS, HEAD_DIM))
            yp, ypb = proj_res_ln([s5_out, att], w_out, yp, g0, b0, bm=128)

            hs = matmul(ysb, w_in, bm=ms, bn=1024)
            h0x = jnp.repeat(_s5_state_to_blocks(state_s5_re[j].astype(F32), state_s5_im[j].astype(F32)), dec_t, axis=0)
            z, h_steps = s5_sample(hs, h0x, *s5w, period=dec_t)
            s5_out = glu(z, w_glu, bm=ms, bn=1024)
            hs3 = hs.reshape(dec_b, dec_t, -1)
            kpad = jnp.zeros((dec_b, 2 * WINDOW - w_buf - dec_t, D_SWA_KV), F32)
            kk = jnp.concatenate([cache_swa_k[j].reshape(dec_b, w_buf, D_SWA_KV).astype(F32),
                                  hs3[:, :, kv0:kv0 + D_SWA_KV], kpad], axis=1)
            vv = jnp.concatenate([cache_swa_v[j].reshape(dec_b, w_buf, D_SWA_KV).astype(F32),
                                  hs3[:, :, kv0 + D_SWA_KV:], kpad], axis=1)
            att = swa_sample(hs, kk, vv, sinks, bias_s, t_new=dec_t)
            hr, hi = _s5_state_from_blocks(h_steps.reshape(dec_b, dec_t, S5_NGB, 2 * S5_HW)[:, dec_t - 1])
            s_s5_re.append(hr); s_s5_im.append(hi)
            s_swa_k.append(kk[:, dec_t:dec_t + w_buf].reshape(dec_b, w_buf, SWA_KV_HEADS, HEAD_DIM))
            s_swa_v.append(vv[:, dec_t:dec_t + w_buf].reshape(dec_b, w_buf, SWA_KV_HEADS, HEAD_DIM))
            ys, ysb = proj_res_ln([s5_out, att], w_out, ys, g0, b0, bm=ms)
        else:
            w_in, w_out = bf(w_odd_in[j]), bf(w_odd_out[j])
            lb = lower_bounds[l]

            hp = matmul(ypb, w_in, bm=1024, bn=1024)
            o, st = hgrn_prompt(hp.reshape(bsz, seq, -1), lb, tb=256)
            p_hg.append(jnp.swapaxes(st, -1, -2))
            yp, ypb = hg_out_ln(o.reshape(mp, d), hp, hg_norm_g[j], w_out, yp, g0, b0, bm=128)

            hs = matmul(ysb, w_in, bm=ms, bn=1024)
            o, s_new = hgrn_sample(hs, lb, state_hgrn[j].astype(F32), t_new=dec_t)
            s_hg.append(s_new)
            ys, ysb = hg_out_ln(o, hs, hg_norm_g[j], w_out, ys, g0, b0, bm=ms)

        wq, wk, wv, wo = bf(w_mem_q[l]), bf(w_mem_k[l]), bf(w_mem_v[l]), bf(w_mem_o[l])
        mk = matmul(mem_b, wk, bm=bsz * N_MEM, bn=D_MEM).reshape(bsz, N_MEM, D_MEM)
        mv = matmul(mem_b, wv, bm=bsz * N_MEM, bn=D_MEM).reshape(bsz, N_MEM, D_MEM)
        p_mem_k.append(mk.reshape(bsz, N_MEM, MEM_HEADS, MEM_HEAD_DIM))
        p_mem_v.append(mv.reshape(bsz, N_MEM, MEM_HEADS, MEM_HEAD_DIM))
        yp, ypb = xattn_ln(yp, ypb, wq, mk, mv, wo, ln_g[l, 1], ln_b[l, 1], bm=256, rows_per_seq=seq)
        ys, ysb = xattn_ln(ys, ysb, wq, cache_mem_k[l].reshape(dec_b, N_MEM, D_MEM).astype(F32),
                           cache_mem_v[l].reshape(dec_b, N_MEM, D_MEM).astype(F32), wo,
                           ln_g[l, 1], ln_b[l, 1], bm=2 * dec_t, rows_per_seq=dec_t)

        wg, wu, wd = bf(w_ffn_gate[l]), bf(w_ffn_up[l]), bf(w_ffn_down[l])
        yp, ypb = ffn_ln(yp, wg, wu, wd, ln_g[l, 2], ln_b[l, 2], bm=512)
        ys, ysb = ffn_ln(ys, wg, wu, wd, ln_g[l, 2], ln_b[l, 2], bm=ms)

    return (yp.reshape(bsz, seq, d), ys.reshape(dec_b, dec_t, d),
            jnp.stack(p_mem_k), jnp.stack(p_mem_v),
            jnp.stack(p_swa_k), jnp.stack(p_swa_v),
            jnp.stack(p_s5_re), jnp.stack(p_s5_im), jnp.stack(p_hg),
            jnp.stack(s_swa_k), jnp.stack(s_swa_v),
            jnp.stack(s_s5_re), jnp.stack(s_s5_im), jnp.stack(s_hg))
```

```python
import functools
import math

import jax
import jax.numpy as jnp
from jax import lax
from jax.experimental import pallas as pl
from jax.experimental.pallas import tpu as pltpu

F32 = jnp.float32
BF16 = jnp.bfloat16
SDS = jax.ShapeDtypeStruct

D_MODEL = 4096
DEPTH = 2
ALPHA = (2 * DEPTH) ** 0.25
LN_EPS = 1e-5
RMS_EPS = 1e-6
NEG_BIG = -1e30

S5_GROUPS, S5_GROUP, S5_STATE = 128, 16, 64
D_S5 = S5_GROUPS * S5_GROUP
S5_GB = 16
S5_NGB = S5_GROUPS // S5_GB
S5_UW = S5_GB * S5_GROUP
S5_HW = S5_GB * S5_STATE

HEAD_DIM, SWA_HEADS, SWA_KV_HEADS, SWA_GQ = 64, 32, 8, 4
D_SWA_Q, D_SWA_KV = 2048, 512
WINDOW = 128
N_BUCKETS, MAX_DISTANCE = 32, 128

HG_DK, HG_HEADS, HG_CHUNK = 128, 32, 32
N_MEM, MEM_HEADS, MEM_HEAD_DIM, D_MEM = 256, 4, 128, 512

VMEM_LIMIT = 56 * 1024 * 1024


def _cparams(n_axes, vmem=VMEM_LIMIT):
    return pltpu.CompilerParams(dimension_semantics=("arbitrary",) * n_axes, vmem_limit_bytes=vmem)


def _dot(a, b):
    return jnp.dot(a, b, preferred_element_type=F32)


def _dot_nt(a, b):
    return lax.dot_general(a, b, (((1,), (1,)), ((), ())), preferred_element_type=F32)


def _dot_tn(a, b):
    return lax.dot_general(a, b, (((0,), (0,)), ((), ())), preferred_element_type=F32)


def _layer_norm_rows(s, g, b):
    mu = jnp.mean(s, axis=-1, keepdims=True)
    c = s - mu
    var = jnp.mean(c * c, axis=-1, keepdims=True)
    return c * lax.rsqrt(var + LN_EPS) * g + b


def _cast_kernel(w_ref, o_ref):
    o_ref[...] = w_ref[0].astype(o_ref.dtype)


CAST_BLOCK_BYTES = 8 * 1024 * 1024


def cast_layer(w, layer):
    _, k, n = w.shape
    rows = next(r for r in (4096, 2048, 1024, 512, 256, 128) if k % r == 0 and r * n * 4 <= CAST_BLOCK_BYTES)
    return pl.pallas_call(
        _cast_kernel,
        grid=(k // rows,),
        in_specs=[pl.BlockSpec((1, rows, n), lambda i: (layer, i, 0))],
        out_specs=pl.BlockSpec((rows, n), lambda i: (i, 0)),
        out_shape=SDS((k, n), BF16),
        compiler_params=_cparams(1),
        name="cast_layer",
    )(w)


def _mm_kernel(x_ref, w_ref, o_ref):
    o_ref[...] = _dot(x_ref[...], w_ref[...]).astype(o_ref.dtype)


def matmul(x, w, *, bm, bn, out_dtype=F32):
    m, k = x.shape
    n = w.shape[1]
    return pl.pallas_call(
        _mm_kernel,
        grid=(m // bm, n // bn),
        in_specs=[pl.BlockSpec((bm, k), lambda i, j: (i, 0)), pl.BlockSpec((k, bn), lambda i, j: (0, j))],
        out_specs=pl.BlockSpec((bm, bn), lambda i, j: (i, j)),
        out_shape=SDS((m, n), out_dtype),
        compiler_params=_cparams(2),
        name="matmul",
    )(x, w)


def _glu_kernel(z_ref, zt_ref, w_ref, o_ref):
    a = _dot(z_ref[...], w_ref[...])
    o_ref[...] = (zt_ref[...].astype(F32) * jax.nn.sigmoid(a)).astype(o_ref.dtype)


def glu(z, w, *, bm, bn):
    m, k = z.shape
    n = w.shape[1]
    return pl.pallas_call(
        _glu_kernel,
        grid=(m // bm, n // bn),
        in_specs=[pl.BlockSpec((bm, k), lambda i, j: (i, 0)), pl.BlockSpec((bm, bn), lambda i, j: (i, j)),
                  pl.BlockSpec((k, bn), lambda i, j: (0, j))],
        out_specs=pl.BlockSpec((bm, bn), lambda i, j: (i, j)),
        out_shape=SDS((m, n), BF16),
        compiler_params=_cparams(2),
        name="glu",
    )(z, z, w)


def _proj_ln_kernel(*refs, n_in):
    xs = refs[:n_in]
    w_ref, res_ref, g_ref, b_ref, y_ref, yb_ref = refs[n_in:]
    acc = None
    off = 0
    for x_ref in xs:
        kx = x_ref.shape[1]
        part = _dot(x_ref[...], w_ref[off:off + kx, :])
        acc = part if acc is None else acc + part
        off += kx
    y = _layer_norm_rows(ALPHA * res_ref[...] + acc, g_ref[...], b_ref[...])
    y_ref[...] = y
    yb_ref[...] = y.astype(BF16)


def proj_res_ln(xs, w, res, g, b, *, bm):
    m, d = res.shape
    k = w.shape[0]
    row = lambda i: (i, 0)
    fixed = lambda i: (0, 0)
    return pl.pallas_call(
        functools.partial(_proj_ln_kernel, n_in=len(xs)),
        grid=(m // bm,),
        in_specs=[pl.BlockSpec((bm, x.shape[1]), row) for x in xs] + [
            pl.BlockSpec((k, d), fixed, pipeline_mode=pl.Buffered(1)),
            pl.BlockSpec((bm, d), row),
            pl.BlockSpec((1, d), fixed),
            pl.BlockSpec((1, d), fixed),
        ],
        out_specs=[pl.BlockSpec((bm, d), row), pl.BlockSpec((bm, d), row)],
        out_shape=[SDS((m, d), F32), SDS((m, d), BF16)],
        compiler_params=_cparams(1),
        name="proj_res_ln",
    )(*xs, w, res, g.reshape(1, d), b.reshape(1, d))


def _hg_out_kernel(o_ref, gate_ref, ng_ref, w_ref, res_ref, g_ref, b_ref, y_ref, yb_ref):
    o = o_ref[...]
    xn = o * lax.rsqrt(jnp.mean(o * o, axis=-1, keepdims=True) + RMS_EPS) * ng_ref[...]
    xn = xn * jax.nn.sigmoid(gate_ref[...])
    mix = _dot(xn.astype(BF16), w_ref[...])
    y = _layer_norm_rows(ALPHA * res_ref[...] + mix, g_ref[...], b_ref[...])
    y_ref[...] = y
    yb_ref[...] = y.astype(BF16)


def hg_out_ln(o, h_all, norm_g, w, res, g, b, *, bm):
    m, d = res.shape
    row = lambda i: (i, 0)
    fixed = lambda i: (0, 0)
    return pl.pallas_call(
        _hg_out_kernel,
        grid=(m // bm,),
        in_specs=[
            pl.BlockSpec((bm, d), row),
            pl.BlockSpec((bm, d), lambda i: (i, 3)),
            pl.BlockSpec((1, d), fixed),
            pl.BlockSpec((d, d), fixed, pipeline_mode=pl.Buffered(1)),
            pl.BlockSpec((bm, d), row),
            pl.BlockSpec((1, d), fixed),
            pl.BlockSpec((1, d), fixed),
        ],
        out_specs=[pl.BlockSpec((bm, d), row), pl.BlockSpec((bm, d), row)],
        out_shape=[SDS((m, d), F32), SDS((m, d), BF16)],
        compiler_params=_cparams(1),
        name="hg_out_ln",
    )(o, h_all, norm_g.reshape(1, d), w, res, g.reshape(1, d), b.reshape(1, d))


def _ffn_kernel(x_ref, wg_ref, wu_ref, wd_ref, g_ref, b_ref, y_ref, yb_ref, xb_ref):
    f = pl.program_id(1)

    @pl.when(f == 0)
    def _():
        x = x_ref[...]
        xb_ref[...] = x.astype(BF16)
        y_ref[...] = ALPHA * x

    xb = xb_ref[...]
    gate = _dot(xb, wg_ref[...])
    up = _dot(xb, wu_ref[...])
    h = (gate * jax.nn.sigmoid(gate) * up).astype(BF16)
    y_ref[...] += _dot(h, wd_ref[...])

    @pl.when(f == pl.num_programs(1) - 1)
    def _():
        y = _layer_norm_rows(y_ref[...], g_ref[...], b_ref[...])
        y_ref[...] = y
        yb_ref[...] = y.astype(BF16)


def ffn_ln(x, wg, wu, wd, g, b, *, bm, bf=256):
    m, d = x.shape
    dff = wg.shape[1]
    one = pl.Buffered(1)
    return pl.pallas_call(
        _ffn_kernel,
        grid=(m // bm, dff // bf),
        in_specs=[
            pl.BlockSpec((bm, d), lambda i, f: (i, 0), pipeline_mode=one),
            pl.BlockSpec((d, bf), lambda i, f: (0, f)),
            pl.BlockSpec((d, bf), lambda i, f: (0, f)),
            pl.BlockSpec((bf, d), lambda i, f: (f, 0)),
            pl.BlockSpec((1, d), lambda i, f: (0, 0)),
            pl.BlockSpec((1, d), lambda i, f: (0, 0)),
        ],
        out_specs=[
            pl.BlockSpec((bm, d), lambda i, f: (i, 0), pipeline_mode=one),
            pl.BlockSpec((bm, d), lambda i, f: (i, 0), pipeline_mode=one),
        ],
        out_shape=[SDS((m, d), F32), SDS((m, d), BF16)],
        scratch_shapes=[pltpu.VMEM((bm, d), BF16)],
        compiler_params=_cparams(2),
        name="ffn_ln",
    )(x, wg, wu, wd, g.reshape(1, d), b.reshape(1, d))


def _cmul(ar, ai, br, bi):
    return ar * br - ai * bi, ar * bi + ai * br


def _s5_coefs(a_r, a_i, period):
    l = a_r.shape[1]
    pows = [(a_r, a_i)]
    for _ in range(period - 1):
        pows.append(_cmul(pows[-1][0], pows[-1][1], a_r, a_i))
    t = lax.broadcasted_iota(jnp.int32, (8, l), 0) & (period - 1)
    shifts = []
    k = 1
    while k < period:
        keep = t >= k
        shifts.append((jnp.where(keep, pows[k - 1][0], 0.0), jnp.where(keep, pows[k - 1][1], 0.0)))
        k *= 2
    p_r = jnp.broadcast_to(pows[0][0], (8, l))
    p_i = jnp.broadcast_to(pows[0][1], (8, l))
    for j in range(1, period):
        p_r = jnp.where(t == j, pows[j][0], p_r)
        p_i = jnp.where(t == j, pows[j][1], p_i)
    return shifts, (p_r, p_i)


def _s5_scan_tile(x_r, x_i, shifts):
    k = 1
    for c_r, c_i in shifts:
        s_r = pltpu.roll(x_r, k, 0)
        s_i = pltpu.roll(x_i, k, 0)
        x_r, x_i = x_r + c_r * s_r - c_i * s_i, x_i + c_r * s_i + c_i * s_r
        k *= 2
    return x_r, x_i


def _s5_prompt_kernel(u_ref, bw_ref, cw_ref, ar_ref, ai_ref, d_ref, z_ref, hfin_ref, bu_ref, carry_ref):
    tt = pl.program_id(2)
    n_rows = bu_ref.shape[0]

    @pl.when(tt == 0)
    def _():
        carry_ref[...] = jnp.zeros_like(carry_ref)

    u = u_ref[0]
    bu_ref[...] = _dot(u.astype(BF16), bw_ref[0])
    shifts, (p_r, p_i) = _s5_coefs(ar_ref[0], ai_ref[0], 8)

    def body(i, carry):
        h_r, h_i = carry
        r0 = pl.multiple_of(i * 8, 8)
        x_r, x_i = _s5_scan_tile(bu_ref[pl.ds(r0, 8), 0:S5_HW], bu_ref[pl.ds(r0, 8), S5_HW:2 * S5_HW], shifts)
        x_r, x_i = x_r + p_r * h_r - p_i * h_i, x_i + p_r * h_i + p_i * h_r
        bu_ref[pl.ds(r0, 8), 0:S5_HW] = x_r
        bu_ref[pl.ds(r0, 8), S5_HW:2 * S5_HW] = x_i
        return x_r[7:8], x_i[7:8]

    h_r, h_i = lax.fori_loop(0, n_rows // 8, body, (carry_ref[:, 0:S5_HW], carry_ref[:, S5_HW:2 * S5_HW]))
    carry_ref[:, 0:S5_HW] = h_r
    carry_ref[:, S5_HW:2 * S5_HW] = h_i
    y = _dot(bu_ref[...].astype(BF16), cw_ref[0]) + d_ref[0] * u
    z_ref[0] = jax.nn.gelu(y).astype(z_ref.dtype)

    @pl.when(tt == pl.num_programs(2) - 1)
    def _():
        hfin_ref[0, 0] = carry_ref[...]


def s5_prompt(h_all, bw, cw, a_r, a_i, d_skip, *, tb):
    bsz, t, _ = h_all.shape
    blk = lambda b, g, s: (g, 0, 0)
    return pl.pallas_call(
        _s5_prompt_kernel,
        grid=(bsz, S5_NGB, t // tb),
        in_specs=[
            pl.BlockSpec((1, tb, S5_UW), lambda b, g, s: (b, s, g)),
            pl.BlockSpec((1, S5_UW, 2 * S5_HW), blk),
            pl.BlockSpec((1, 2 * S5_HW, S5_UW), blk),
            pl.BlockSpec((1, 1, S5_HW), blk),
            pl.BlockSpec((1, 1, S5_HW), blk),
            pl.BlockSpec((1, 1, S5_UW), blk),
        ],
        out_specs=[
            pl.BlockSpec((1, tb, S5_UW), lambda b, g, s: (b, s, g)),
            pl.BlockSpec((1, 1, 1, 2 * S5_HW), lambda b, g, s: (b, g, 0, 0)),
        ],
        out_shape=[SDS((bsz, t, D_S5), BF16), SDS((bsz, S5_NGB, 1, 2 * S5_HW), F32)],
        scratch_shapes=[pltpu.VMEM((tb, 2 * S5_HW), F32), pltpu.VMEM((1, 2 * S5_HW), F32)],
        compiler_params=_cparams(3),
        name="s5_prompt",
    )(h_all, bw, cw, a_r, a_i, d_skip)


def _s5_sample_kernel(u_ref, h0_ref, bw_ref, cw_ref, ar_ref, ai_ref, d_ref, z_ref, h_ref, *, period):
    u = u_ref[...]
    bu = _dot(u.astype(BF16), bw_ref[0])
    shifts, (p_r, p_i) = _s5_coefs(ar_ref[0], ai_ref[0], period)
    for i in range(u.shape[0] // 8):
        rows = slice(i * 8, (i + 1) * 8)
        x_r, x_i = _s5_scan_tile(bu[rows, 0:S5_HW], bu[rows, S5_HW:2 * S5_HW], shifts)
        h0_r = h0_ref[rows, 0:S5_HW]
        h0_i = h0_ref[rows, S5_HW:2 * S5_HW]
        h_ref[rows, 0:S5_HW] = x_r + p_r * h0_r - p_i * h0_i
        h_ref[rows, S5_HW:2 * S5_HW] = x_i + p_r * h0_i + p_i * h0_r
    y = _dot(h_ref[...].astype(BF16), cw_ref[0]) + d_ref[0] * u
    z_ref[...] = jax.nn.gelu(y).astype(z_ref.dtype)


def s5_sample(h_all, h0x, bw, cw, a_r, a_i, d_skip, *, period):
    m = h_all.shape[0]
    blk = lambda g: (g, 0, 0)
    return pl.pallas_call(
        functools.partial(_s5_sample_kernel, period=period),
        grid=(S5_NGB,),
        in_specs=[
            pl.BlockSpec((m, S5_UW), lambda g: (0, g)),
            pl.BlockSpec((m, 2 * S5_HW), lambda g: (0, g)),
            pl.BlockSpec((1, S5_UW, 2 * S5_HW), blk),
            pl.BlockSpec((1, 2 * S5_HW, S5_UW), blk),
            pl.BlockSpec((1, 1, S5_HW), blk),
            pl.BlockSpec((1, 1, S5_HW), blk),
            pl.BlockSpec((1, 1, S5_UW), blk),
        ],
        out_specs=[pl.BlockSpec((m, S5_UW), lambda g: (0, g)), pl.BlockSpec((m, 2 * S5_HW), lambda g: (0, g))],
        out_shape=[SDS((m, D_S5), BF16), SDS((m, S5_NGB * 2 * S5_HW), F32)],
        compiler_params=_cparams(1),
        name="s5_sample",
    )(h_all, h0x, bw, cw, a_r, a_i, d_skip)


def _bias_kernel(rb_ref, o_ref):
    h = pl.program_id(0)
    q = lax.broadcasted_iota(jnp.int32, (WINDOW, 2 * WINDOW), 0)
    k = lax.broadcasted_iota(jnp.int32, (WINDOW, 2 * WINDOW), 1)
    n = jnp.maximum(q + WINDOW - k, 0)
    max_exact = N_BUCKETS // 2
    nf = jnp.maximum(n, 1).astype(F32)
    large = max_exact + (jnp.log(nf / max_exact) / math.log(MAX_DISTANCE / max_exact)
                         * (N_BUCKETS - max_exact)).astype(jnp.int32)
    large = jnp.minimum(large, N_BUCKETS - 1)
    bucket = jnp.where(n < max_exact, n, large)
    out = jnp.zeros((WINDOW, 2 * WINDOW), F32)
    for b in range(N_BUCKETS):
        out = jnp.where(bucket == b, rb_ref[b, h], out)
    o_ref[0] = out


def bias_table(rel_bias):
    return pl.pallas_call(
        _bias_kernel,
        grid=(SWA_HEADS,),
        in_specs=[pl.BlockSpec(memory_space=pltpu.SMEM)],
        out_specs=pl.BlockSpec((1, WINDOW, 2 * WINDOW), lambda h: (h, 0, 0)),
        out_shape=SDS((SWA_HEADS, WINDOW, 2 * WINDOW), F32),
        compiler_params=_cparams(1),
        name="bias_table",
    )(rel_bias)


def _swa_softmax(s, valid, sink_col):
    s = jnp.where(valid, s, NEG_BIG)
    m = jnp.maximum(jnp.max(s, axis=-1, keepdims=True), sink_col)
    p = jnp.exp(s - m)
    return p / (jnp.sum(p, axis=-1, keepdims=True) + jnp.exp(sink_col - m))


def _sink_col(sinks_ref, first_head, rows_per_head):
    n = SWA_GQ * rows_per_head
    r = lax.broadcasted_iota(jnp.int32, (n, 1), 0)
    col = jnp.full((n, 1), sinks_ref[first_head], F32)
    for g in range(1, SWA_GQ):
        col = jnp.where(r >= g * rows_per_head, sinks_ref[first_head + g], col)
    return col


def _swa_prompt_kernel(sinks_ref, q_ref, kp_ref, kc_ref, vp_ref, vc_ref, bias_ref, o_ref):
    pair = pl.program_id(0)
    i = pl.program_id(2)
    blk = WINDOW
    q = q_ref[0].astype(BF16)
    kk = jnp.concatenate([kp_ref[0], kc_ref[0]], axis=0).astype(BF16)
    vv = jnp.concatenate([vp_ref[0], vc_ref[0]], axis=0).astype(BF16)
    qi = lax.broadcasted_iota(jnp.int32, (SWA_GQ * blk, 2 * blk), 0) & (blk - 1)
    ki = lax.broadcasted_iota(jnp.int32, (SWA_GQ * blk, 2 * blk), 1)
    valid = (ki > qi) & (ki <= qi + blk) & ((ki >= blk) | (i > 0))
    outs = []
    for j in range(2):
        k_j = kk[:, j * HEAD_DIM:(j + 1) * HEAD_DIM]
        v_j = vv[:, j * HEAD_DIM:(j + 1) * HEAD_DIM]
        q4 = jnp.concatenate([q[:, (j * SWA_GQ + g) * HEAD_DIM:(j * SWA_GQ + g + 1) * HEAD_DIM]
                              for g in range(SWA_GQ)], axis=0)
        s = _dot_nt(q4, k_j) * (HEAD_DIM ** -0.5)
        s = s + bias_ref[j * SWA_GQ:(j + 1) * SWA_GQ].reshape(SWA_GQ * blk, 2 * blk)
        p = _swa_softmax(s, valid, _sink_col(sinks_ref, pair * 2 * SWA_GQ + j * SWA_GQ, blk))
        o4 = _dot(p.astype(BF16), v_j)
        outs += [o4[g * blk:(g + 1) * blk] for g in range(SWA_GQ)]
    o_ref[0] = jnp.concatenate(outs, axis=1).astype(o_ref.dtype)


def swa_prompt(h_all, sinks, table):
    bsz, t, _ = h_all.shape
    nblk = t // WINDOW
    qw = 2 * SWA_GQ * HEAD_DIM
    kw = 2 * HEAD_DIM
    q0, k0, v0 = D_S5 // qw, (D_S5 + D_SWA_Q) // kw, (D_S5 + D_SWA_Q + D_SWA_KV) // kw
    cur = lambda c0: (lambda p, b, i: (b, i, c0 + p))
    prev = lambda c0: (lambda p, b, i: (b, jnp.maximum(i - 1, 0), c0 + p))
    return pl.pallas_call(
        _swa_prompt_kernel,
        grid=(SWA_KV_HEADS // 2, bsz, nblk),
        in_specs=[
            pl.BlockSpec(memory_space=pltpu.SMEM),
            pl.BlockSpec((1, WINDOW, qw), cur(q0)),
            pl.BlockSpec((1, WINDOW, kw), prev(k0)),
            pl.BlockSpec((1, WINDOW, kw), cur(k0)),
            pl.BlockSpec((1, WINDOW, kw), prev(v0)),
            pl.BlockSpec((1, WINDOW, kw), cur(v0)),
            pl.BlockSpec((2 * SWA_GQ, WINDOW, 2 * WINDOW), lambda p, b, i: (p, 0, 0)),
        ],
        out_specs=pl.BlockSpec((1, WINDOW, qw), lambda p, b, i: (b, i, p)),
        out_shape=SDS((bsz, t, D_SWA_Q), BF16),
        compiler_params=_cparams(3),
        name="swa_prompt",
    )(sinks, h_all, h_all, h_all, h_all, h_all, table)


def _swa_sample_kernel(sinks_ref, q_ref, kk_ref, vv_ref, bias_ref, o_ref, *, t_new):
    rows = q_ref.shape[0]
    n_keys = kk_ref.shape[1]
    q = q_ref[...].astype(BF16)
    r = lax.broadcasted_iota(jnp.int32, (SWA_GQ * rows, n_keys), 0)
    ti = r & (t_new - 1)
    ki = lax.broadcasted_iota(jnp.int32, (SWA_GQ * rows, n_keys), 1)
    valid = (ki > ti) & (ki <= ti + WINDOW)
    second = (r & (rows - 1)) >= t_new
    second_o = (lax.broadcasted_iota(jnp.int32, (SWA_GQ * rows, HEAD_DIM), 0) & (rows - 1)) >= t_new
    outs = []
    for j in range(SWA_KV_HEADS):
        cols = slice(j * HEAD_DIM, (j + 1) * HEAD_DIM)
        q4 = jnp.concatenate([q[:, (j * SWA_GQ + g) * HEAD_DIM:(j * SWA_GQ + g + 1) * HEAD_DIM]
                              for g in range(SWA_GQ)], axis=0)
        s = jnp.where(second, _dot_nt(q4, kk_ref[1][:, cols].astype(BF16)),
                      _dot_nt(q4, kk_ref[0][:, cols].astype(BF16))) * (HEAD_DIM ** -0.5)
        s = s + bias_ref[j * SWA_GQ:(j + 1) * SWA_GQ].reshape(SWA_GQ * rows, n_keys)
        p = _swa_softmax(s, valid, _sink_col(sinks_ref, j * SWA_GQ, rows)).astype(BF16)
        o4 = jnp.where(second_o, _dot(p, vv_ref[1][:, cols].astype(BF16)), _dot(p, vv_ref[0][:, cols].astype(BF16)))
        outs += [o4[g * rows:(g + 1) * rows] for g in range(SWA_GQ)]
    o_ref[...] = jnp.concatenate(outs, axis=1).astype(o_ref.dtype)


def swa_sample(h_all, kk, vv, sinks, bias_s, *, t_new):
    m = h_all.shape[0]
    n_keys = kk.shape[1]
    rows = 2 * t_new
    return pl.pallas_call(
        functools.partial(_swa_sample_kernel, t_new=t_new),
        grid=(m // rows,),
        in_specs=[
            pl.BlockSpec(memory_space=pltpu.SMEM),
            pl.BlockSpec((rows, D_SWA_Q), lambda i: (i, D_S5 // D_SWA_Q)),
            pl.BlockSpec((2, n_keys, D_SWA_KV), lambda i: (i, 0, 0)),
            pl.BlockSpec((2, n_keys, D_SWA_KV), lambda i: (i, 0, 0)),
            pl.BlockSpec((SWA_HEADS, rows, n_keys), lambda i: (0, 0, 0)),
        ],
        out_specs=pl.BlockSpec((rows, D_SWA_Q), lambda i: (i, 0)),
        out_shape=SDS((m, D_SWA_Q), BF16),
        compiler_params=_cparams(1),
        name="swa_sample",
    )(sinks, h_all, kk, vv, bias_s)


def _hg_gates(q, fz, lb):
    qs = q * jax.nn.sigmoid(q)
    f = lb + (1.0 - lb) * jax.nn.sigmoid(fz)
    return qs, jnp.log(f), 1.0 - f


HG_GROUP = 4


def _hgrn_prompt_kernel(q_ref, f_ref, v_ref, lb_ref, o_ref, st_ref, s_scr, *, n_heads):
    tt = pl.program_id(2)
    n_rows = q_ref.shape[1]
    c, ng = HG_CHUNK, HG_GROUP
    gr = c * ng

    @pl.when(tt == 0)
    def _():
        s_scr[...] = jnp.zeros_like(s_scr)

    ri = lax.broadcasted_iota(jnp.int32, (gr, gr), 0)
    ci = lax.broadcasted_iota(jnp.int32, (gr, gr), 1)
    causal = (ri >= ci) & ((ri // c) == (ci // c))
    t_in = lax.broadcasted_iota(jnp.int32, (gr, HG_DK), 0) & (c - 1)
    own = ((lax.broadcasted_iota(jnp.int32, (gr, ng * HG_DK), 0) // c)
           == (lax.broadcasted_iota(jnp.int32, (gr, ng * HG_DK), 1) // HG_DK)).astype(BF16)

    def body(gi, _):
        r0 = pl.multiple_of(gi * gr, gr)
        for hd in range(n_heads):
            cols = slice(hd * HG_DK, (hd + 1) * HG_DK)
            qs, lf, k = _hg_gates(q_ref[0, pl.ds(r0, gr), cols], f_ref[0, pl.ds(r0, gr), cols], lb_ref[:, cols])
            v = v_ref[0, pl.ds(r0, gr), cols].astype(BF16)
            cum = lf
            sh = 1
            while sh < c:
                cum = cum + jnp.where(t_in >= sh, pltpu.roll(cum, sh, 0), 0.0)
                sh *= 2
            last3 = cum.reshape(ng, c, HG_DK)[:, c - 1:c, :]
            last = jnp.broadcast_to(last3, (ng, c, HG_DK)).reshape(gr, HG_DK)
            qt = (qs * jnp.exp(cum)).astype(BF16)
            kt = (k * jnp.exp(-cum)).astype(BF16)
            kl = (k * jnp.exp(last - cum)).astype(BF16)
            decay = jnp.exp(last3)
            attn = jnp.where(causal, _dot_nt(qt, kt), 0.0).astype(BF16)
            kv = _dot_tn(v, jnp.concatenate([kl] * ng, axis=1) * own)
            st = s_scr[hd]
            starts = []
            for j in range(ng):
                starts.append(st)
                st = st * decay[j] + kv[:, j * HG_DK:(j + 1) * HG_DK]
            s_scr[hd] = st
            s_cat = jnp.concatenate(starts, axis=1).astype(BF16)
            o_ref[0, pl.ds(r0, gr), cols] = (_dot(attn, v)
                                            + _dot_nt(jnp.concatenate([qt] * ng, axis=1) * own, s_cat))
        return 0

    lax.fori_loop(0, n_rows // gr, body, 0)

    @pl.when(tt == pl.num_programs(2) - 1)
    def _():
        st_ref[0] = s_scr[...]


def hgrn_prompt(h_all, lb, *, tb, n_heads=2):
    bsz, t, _ = h_all.shape
    hw = n_heads * HG_DK
    nhb = D_MODEL // hw
    col = lambda c0: (lambda b, h, s: (b, s, c0 * nhb + h))
    return pl.pallas_call(
        functools.partial(_hgrn_prompt_kernel, n_heads=n_heads),
        grid=(bsz, nhb, t // tb),
        in_specs=[
            pl.BlockSpec((1, tb, hw), col(0)),
            pl.BlockSpec((1, tb, hw), col(1)),
            pl.BlockSpec((1, tb, hw), col(2)),
            pl.BlockSpec((1, hw), lambda b, h, s: (0, h)),
        ],
        out_specs=[
            pl.BlockSpec((1, tb, hw), lambda b, h, s: (b, s, h)),
            pl.BlockSpec((1, n_heads, HG_DK, HG_DK), lambda b, h, s: (b, h, 0, 0)),
        ],
        out_shape=[SDS((bsz, t, D_MODEL), F32), SDS((bsz, HG_HEADS, HG_DK, HG_DK), F32)],
        scratch_shapes=[pltpu.VMEM((n_heads, HG_DK, HG_DK), F32)],
        compiler_params=_cparams(3),
        name="hgrn_prompt",
    )(h_all, h_all, h_all, lb.reshape(1, D_MODEL))


def _hgrn_sample_kernel(q_ref, f_ref, v_ref, lb_ref, s0_ref, o_ref, s_ref, *, n_heads, t_new):
    rows = q_ref.shape[0]
    ri = lax.broadcasted_iota(jnp.int32, (rows, rows), 0)
    ci = lax.broadcasted_iota(jnp.int32, (rows, rows), 1)
    causal = (ri >= ci) & ((ri >= t_new) == (ci >= t_new))
    tril = causal.astype(F32)
    second = lax.broadcasted_iota(jnp.int32, (rows, HG_DK), 0) >= t_new
    eye = (lax.broadcasted_iota(jnp.int32, (HG_DK, HG_DK), 0)
           == lax.broadcasted_iota(jnp.int32, (HG_DK, HG_DK), 1))
    for hd in range(n_heads):
        cols = slice(hd * HG_DK, (hd + 1) * HG_DK)
        qs, lf, k = _hg_gates(q_ref[:, cols], f_ref[:, cols], lb_ref[:, cols])
        v = v_ref[:, cols].astype(BF16)
        cum = jnp.dot(tril, lf, preferred_element_type=F32, precision=lax.Precision.HIGHEST)
        last0 = cum[t_new - 1:t_new]
        last1 = cum[rows - 1:rows]
        last = jnp.where(second, last1, last0)
        qt = (qs * jnp.exp(cum)).astype(BF16)
        kt = (k * jnp.exp(-cum)).astype(BF16)
        kl = k * jnp.exp(last - cum)
        attn = jnp.where(causal, _dot_nt(qt, kt), 0.0).astype(BF16)
        s0a = s0_ref[0, hd]
        s0b = s0_ref[1, hd]
        o_ref[:, cols] = _dot(attn, v) + jnp.where(second, _dot(qt, s0b.astype(BF16)), _dot(qt, s0a.astype(BF16)))
        for bb, (s0, lst) in enumerate(((s0a, last0), (s0b, last1))):
            decay = jnp.sum(jnp.where(eye, jnp.exp(lst), 0.0), axis=1, keepdims=True)
            kl_b = jnp.where(second == (bb == 1), kl, 0.0).astype(BF16)
            s_ref[bb, hd] = decay * s0 + _dot_tn(kl_b, v)


def hgrn_sample(h_all, lb, s0, *, t_new, n_heads=8):
    m = h_all.shape[0]
    rows = 2 * t_new
    hw = n_heads * HG_DK
    nhb = D_MODEL // hw
    col = lambda c0: (lambda i, h: (i, c0 * nhb + h))
    return pl.pallas_call(
        functools.partial(_hgrn_sample_kernel, n_heads=n_heads, t_new=t_new),
        grid=(m // rows, nhb),
        in_specs=[
            pl.BlockSpec((rows, hw), col(0)),
            pl.BlockSpec((rows, hw), col(1)),
            pl.BlockSpec((rows, hw), col(2)),
            pl.BlockSpec((1, hw), lambda i, h: (0, h)),
            pl.BlockSpec((2, n_heads, HG_DK, HG_DK), lambda i, h: (i, h, 0, 0)),
        ],
        out_specs=[
            pl.BlockSpec((rows, hw), lambda i, h: (i, h)),
            pl.BlockSpec((2, n_heads, HG_DK, HG_DK), lambda i, h: (i, h, 0, 0)),
        ],
        out_shape=[SDS((m, D_MODEL), F32), SDS(s0.shape, F32)],
        compiler_params=_cparams(2),
        name="hgrn_sample",
    )(h_all, h_all, h_all, lb.reshape(1, D_MODEL), s0)


def _xattn_kernel(y_ref, yb_ref, wq_ref, mk_ref, mv_ref, wo_ref, g_ref, b_ref, o_ref, ob_ref, *, n_seq):
    rows = y_ref.shape[0]
    q = _dot(yb_ref[...], wq_ref[...]).astype(BF16)
    second = lax.broadcasted_iota(jnp.int32, (rows, 1), 0) >= rows // 2
    heads = []
    for h in range(MEM_HEADS):
        cols = slice(h * MEM_HEAD_DIM, (h + 1) * MEM_HEAD_DIM)
        qh = q[:, cols]
        s = _dot_nt(qh, mk_ref[0][:, cols].astype(BF16))
        if n_seq == 2:
            s = jnp.where(second, _dot_nt(qh, mk_ref[1][:, cols].astype(BF16)), s)
        s = s * (MEM_HEAD_DIM ** -0.5)
        e = jnp.exp(s - jnp.max(s, axis=-1, keepdims=True))
        p = (e / jnp.sum(e, axis=-1, keepdims=True)).astype(BF16)
        o = _dot(p, mv_ref[0][:, cols].astype(BF16))
        if n_seq == 2:
            o = jnp.where(second, _dot(p, mv_ref[1][:, cols].astype(BF16)), o)
        heads.append(o.astype(BF16))
    mix = _dot(jnp.concatenate(heads, axis=1), wo_ref[...])
    y = _layer_norm_rows(ALPHA * y_ref[...] + mix, g_ref[...], b_ref[...])
    o_ref[...] = y
    ob_ref[...] = y.astype(BF16)


def xattn_ln(y, yb, wq, mk, mv, wo, g, b, *, bm, rows_per_seq):
    m, d = y.shape
    n_seq = 2 if bm == 2 * rows_per_seq else 1
    assert n_seq == 2 or rows_per_seq % bm == 0
    row = lambda i: (i, 0)
    fixed = lambda i: (0, 0)
    mem = (lambda i: (i, 0, 0)) if n_seq == 2 else (lambda i: (i * bm // rows_per_seq, 0, 0))
    return pl.pallas_call(
        functools.partial(_xattn_kernel, n_seq=n_seq),
        grid=(m // bm,),
        in_specs=[
            pl.BlockSpec((bm, d), row),
            pl.BlockSpec((bm, d), row),
            pl.BlockSpec((d, D_MEM), fixed),
            pl.BlockSpec((n_seq, N_MEM, D_MEM), mem),
            pl.BlockSpec((n_seq, N_MEM, D_MEM), mem),
            pl.BlockSpec((D_MEM, d), fixed),
            pl.BlockSpec((1, d), fixed),
            pl.BlockSpec((1, d), fixed),
        ],
        out_specs=[pl.BlockSpec((bm, d), row), pl.BlockSpec((bm, d), row)],
        out_shape=[SDS((m, d), F32), SDS((m, d), BF16)],
        compiler_params=_cparams(1),
        name="xattn_ln",
    )(y, yb, wq, mk, mv, wo, g.reshape(1, d), b.reshape(1, d))


def _s5_discretize(lam_re, lam_im, log_dt, b_re, b_im):
    lr = jnp.minimum(lam_re.astype(F32), -1e-4)
    li = lam_im.astype(F32)
    dt = jnp.exp(log_dt.astype(F32))[:, None]
    mag = jnp.exp(lr * dt)
    a_re = mag * jnp.cos(li * dt)
    a_im = mag * jnp.sin(li * dt)
    den = lr * lr + li * li
    fr = ((a_re - 1.0) * lr + a_im * li) / den
    fi = (a_im * lr - (a_re - 1.0) * li) / den
    br, bi = b_re.astype(F32), b_im.astype(F32)
    bb_re = fr[..., None] * br - fi[..., None] * bi
    bb_im = fr[..., None] * bi + fi[..., None] * br
    return a_re, a_im, bb_re, bb_im


def _s5_block_weights(a_re, a_im, bb_re, bb_im, c_re, c_im, d_skip):
    eye = jnp.eye(S5_GB, dtype=F32)
    shp_b = (S5_NGB, S5_GB, S5_STATE, S5_GROUP)
    shp_c = (S5_NGB, S5_GB, S5_GROUP, S5_STATE)
    blk_b = lambda w: jnp.einsum('bgph,gk->bghkp', w.reshape(shp_b), eye).reshape(S5_NGB, S5_UW, S5_HW)
    blk_c = lambda w: jnp.einsum('bghp,gk->bkpgh', w.reshape(shp_c), eye).reshape(S5_NGB, S5_HW, S5_UW)
    bw = jnp.concatenate([blk_b(bb_re), blk_b(bb_im)], axis=-1).astype(BF16)
    cw = jnp.concatenate([blk_c(c_re.astype(F32)), blk_c(-c_im.astype(F32))], axis=1).astype(BF16)
    return (bw, cw, a_re.reshape(S5_NGB, 1, S5_HW), a_im.reshape(S5_NGB, 1, S5_HW),
            d_skip.astype(F32).reshape(S5_NGB, 1, S5_UW))


def _s5_state_to_blocks(s_re, s_im):
    b = s_re.shape[0]
    return jnp.concatenate([s_re.reshape(b, S5_NGB, S5_HW), s_im.reshape(b, S5_NGB, S5_HW)],
                           axis=-1).reshape(b, S5_NGB * 2 * S5_HW)


def _s5_state_from_blocks(h):
    b = h.shape[0]
    return (h[:, :, :S5_HW].reshape(b, S5_GROUPS, S5_STATE), h[:, :, S5_HW:].reshape(b, S5_GROUPS, S5_STATE))


def kernel(x_prompt, x_sample, cache_mem_k, cache_mem_v, cache_swa_k, cache_swa_v, state_s5_re, state_s5_im, state_hgrn, mem_prompt, rel_bias, w_even_in, s5_lam_re, s5_lam_im, s5_log_dt, s5_b_re, s5_b_im, s5_c_re, s5_c_im, s5_d, s5_w_glu, swa_sinks, w_even_out, hg_lb_logits, w_odd_in, hg_norm_g, w_odd_out, w_mem_q, w_mem_k, w_mem_v, w_mem_o, w_ffn_gate, w_ffn_up, w_ffn_down, ln_g, ln_b):
    bsz, seq, d = x_prompt.shape
    dec_b, dec_t, _ = x_sample.shape
    mp, ms = bsz * seq, dec_b * dec_t
    w_buf = cache_swa_k.shape[2]
    bf = lambda w: w.astype(BF16)

    lb_soft = jax.nn.softmax(hg_lb_logits.astype(F32), axis=0)
    lower_bounds = jnp.cumsum(lb_soft, axis=0) - lb_soft[0]
    table = bias_table(rel_bias.astype(F32))
    bias_s = jnp.tile(table[:, :dec_t], (1, 2, 1))
    mem_b = bf(mem_prompt).reshape(bsz * N_MEM, d)

    yp = x_prompt.reshape(mp, d)
    ys = x_sample.reshape(ms, d)
    ypb, ysb = bf(yp), bf(ys)
    p_mem_k, p_mem_v = [], []
    p_swa_k, p_swa_v, p_s5_re, p_s5_im, p_hg = [], [], [], [], []
    s_swa_k, s_swa_v, s_s5_re, s_s5_im, s_hg = [], [], [], [], []

    for l in range(DEPTH):
        j = l // 2
        g0, b0 = ln_g[l, 0], ln_b[l, 0]
        if l % 2 == 0:
            a_re, a_im, bb_re, bb_im = _s5_discretize(s5_lam_re[j], s5_lam_im[j], s5_log_dt[j], s5_b_re[j], s5_b_im[j])
            s5w = _s5_block_weights(a_re, a_im, bb_re, bb_im, s5_c_re[j], s5_c_im[j], s5_d[j])
            w_in, w_glu, w_out = cast_layer(w_even_in, j), cast_layer(s5_w_glu, j), cast_layer(w_even_out, j)
            sinks = swa_sinks[j].astype(F32)
            kv0 = D_S5 + D_SWA_Q

            hp = matmul(ypb, w_in, bm=1024, bn=1024).reshape(bsz, seq, -1)
            z, hfin = s5_prompt(hp, *s5w, tb=512)
            s5_out = glu(z.reshape(mp, D_S5), w_glu, bm=1024, bn=1024)
            att = swa_prompt(hp, sinks, table).reshape(mp, D_SWA_Q)
            hr, hi = _s5_state_from_blocks(hfin[:, :, 0])
            p_s5_re.append(hr); p_s5_im.append(hi)
            p_swa_k.append(hp[:, seq - w_buf:, kv0:kv0 + D_SWA_KV].reshape(bsz, w_buf, SWA_KV_HEADS, HEAD_DIM))
            p_swa_v.append(hp[:, seq - w_buf:, kv0 + D_SWA_KV:].reshape(bsz, w_buf, SWA_KV_HEADS, HEAD_DIM))
            yp, ypb = proj_res_ln([s5_out, att], w_out, yp, g0, b0, bm=128)

            hs = matmul(ysb, w_in, bm=ms, bn=1024)
            h0x = jnp.repeat(_s5_state_to_blocks(state_s5_re[j].astype(F32), state_s5_im[j].astype(F32)), dec_t, axis=0)
            z, h_steps = s5_sample(hs, h0x, *s5w, period=dec_t)
            s5_out = glu(z, w_glu, bm=ms, bn=1024)
            hs3 = hs.reshape(dec_b, dec_t, -1)
            kpad = jnp.zeros((dec_b, 2 * WINDOW - w_buf - dec_t, D_SWA_KV), F32)
            kk = jnp.concatenate([cache_swa_k[j].reshape(dec_b, w_buf, D_SWA_KV).astype(F32),
                                  hs3[:, :, kv0:kv0 + D_SWA_KV], kpad], axis=1)
            vv = jnp.concatenate([cache_swa_v[j].reshape(dec_b, w_buf, D_SWA_KV).astype(F32),
                                  hs3[:, :, kv0 + D_SWA_KV:], kpad], axis=1)
            att = swa_sample(hs, kk, vv, sinks, bias_s, t_new=dec_t)
            hr, hi = _s5_state_from_blocks(h_steps.reshape(dec_b, dec_t, S5_NGB, 2 * S5_HW)[:, dec_t - 1])
            s_s5_re.append(hr); s_s5_im.append(hi)
            s_swa_k.append(kk[:, dec_t:dec_t + w_buf].reshape(dec_b, w_buf, SWA_KV_HEADS, HEAD_DIM))
            s_swa_v.append(vv[:, dec_t:dec_t + w_buf].reshape(dec_b, w_buf, SWA_KV_HEADS, HEAD_DIM))
            ys, ysb = proj_res_ln([s5_out, att], w_out, ys, g0, b0, bm=ms)
        else:
            w_in, w_out = cast_layer(w_odd_in, j), cast_layer(w_odd_out, j)
            lb = lower_bounds[l]

            hp = matmul(ypb, w_in, bm=1024, bn=1024)
            o, st = hgrn_prompt(hp.reshape(bsz, seq, -1), lb, tb=512, n_heads=4)
            p_hg.append(jnp.swapaxes(st, -1, -2))
            yp, ypb = hg_out_ln(o.reshape(mp, d), hp, hg_norm_g[j], w_out, yp, g0, b0, bm=128)

            hs = matmul(ysb, w_in, bm=ms, bn=1024)
            o, s_new = hgrn_sample(hs, lb, state_hgrn[j].astype(F32), t_new=dec_t)
            s_hg.append(s_new)
            ys, ysb = hg_out_ln(o, hs, hg_norm_g[j], w_out, ys, g0, b0, bm=ms)

        wq, wk, wv, wo = (cast_layer(w, l) for w in (w_mem_q, w_mem_k, w_mem_v, w_mem_o))
        mk = matmul(mem_b, wk, bm=bsz * N_MEM, bn=D_MEM).reshape(bsz, N_MEM, D_MEM)
        mv = matmul(mem_b, wv, bm=bsz * N_MEM, bn=D_MEM).reshape(bsz, N_MEM, D_MEM)
        p_mem_k.append(mk.reshape(bsz, N_MEM, MEM_HEADS, MEM_HEAD_DIM))
        p_mem_v.append(mv.reshape(bsz, N_MEM, MEM_HEADS, MEM_HEAD_DIM))
        yp, ypb = xattn_ln(yp, ypb, wq, mk, mv, wo, ln_g[l, 1], ln_b[l, 1], bm=256, rows_per_seq=seq)
        ys, ysb = xattn_ln(ys, ysb, wq, cache_mem_k[l].reshape(dec_b, N_MEM, D_MEM).astype(F32),
                           cache_mem_v[l].reshape(dec_b, N_MEM, D_MEM).astype(F32), wo,
                           ln_g[l, 1], ln_b[l, 1], bm=2 * dec_t, rows_per_seq=dec_t)

        wg, wu, wd = (cast_layer(w, l) for w in (w_ffn_gate, w_ffn_up, w_ffn_down))
        yp, ypb = ffn_ln(yp, wg, wu, wd, ln_g[l, 2], ln_b[l, 2], bm=512)
        ys, ysb = ffn_ln(ys, wg, wu, wd, ln_g[l, 2], ln_b[l, 2], bm=ms)

    return (yp.reshape(bsz, seq, d), ys.reshape(dec_b, dec_t, d),
            jnp.stack(p_mem_k), jnp.stack(p_mem_v),
            jnp.stack(p_swa_k), jnp.stack(p_swa_v),
            jnp.stack(p_s5_re), jnp.stack(p_s5_im), jnp.stack(p_hg),
            jnp.stack(s_swa_k), jnp.stack(s_swa_v),
            jnp.stack(s_s5_re), jnp.stack(s_s5_im), jnp.stack(s_hg))
```

```python
import functools
import math

import jax
import jax.numpy as jnp
from jax import lax
from jax.experimental import pallas as pl
from jax.experimental.pallas import tpu as pltpu

F32 = jnp.float32
BF16 = jnp.bfloat16
SDS = jax.ShapeDtypeStruct

D_MODEL = 4096
DEPTH = 2
ALPHA = (2 * DEPTH) ** 0.25
LN_EPS = 1e-5
RMS_EPS = 1e-6
NEG_BIG = -1e30

S5_GROUPS, S5_GROUP, S5_STATE = 128, 16, 64
D_S5 = S5_GROUPS * S5_GROUP
S5_GB = 16
S5_NGB = S5_GROUPS // S5_GB
S5_UW = S5_GB * S5_GROUP
S5_HW = S5_GB * S5_STATE
S5_SCAN_CHUNK = 128

HEAD_DIM, SWA_HEADS, SWA_KV_HEADS, SWA_GQ = 64, 32, 8, 4
D_SWA_Q, D_SWA_KV = 2048, 512
WINDOW = 128
N_BUCKETS, MAX_DISTANCE = 32, 128

HG_DK, HG_HEADS, HG_CHUNK = 128, 32, 32
N_MEM, MEM_HEADS, MEM_HEAD_DIM, D_MEM = 256, 4, 128, 512

VMEM_LIMIT = 56 * 1024 * 1024
VMEM_LIMIT_MAX = 60 * 1024 * 1024
LN_ROW_CHUNK = 64


def _cparams(n_axes, vmem=VMEM_LIMIT):
    return pltpu.CompilerParams(dimension_semantics=("arbitrary",) * n_axes, vmem_limit_bytes=vmem)


def _dot(a, b):
    return jnp.dot(a, b, preferred_element_type=F32)


def _dot_nt(a, b):
    return lax.dot_general(a, b, (((1,), (1,)), ((), ())), preferred_element_type=F32)


def _dot_tn(a, b):
    return lax.dot_general(a, b, (((0,), (0,)), ((), ())), preferred_element_type=F32)


def _layer_norm_rows(s, g, b):
    mu = jnp.mean(s, axis=-1, keepdims=True)
    c = s - mu
    var = jnp.mean(c * c, axis=-1, keepdims=True)
    return c * lax.rsqrt(var + LN_EPS) * g + b


def _cast_kernel(w_ref, o_ref):
    o_ref[...] = w_ref[0].astype(o_ref.dtype)


CAST_BLOCK_BYTES = 8 * 1024 * 1024


def cast_layer(w, layer):
    _, k, n = w.shape
    rows = next(r for r in (4096, 2048, 1024, 512, 256, 128) if k % r == 0 and r * n * 4 <= CAST_BLOCK_BYTES)
    return pl.pallas_call(
        _cast_kernel,
        grid=(k // rows,),
        in_specs=[pl.BlockSpec((1, rows, n), lambda i: (layer, i, 0))],
        out_specs=pl.BlockSpec((rows, n), lambda i: (i, 0)),
        out_shape=SDS((k, n), BF16),
        compiler_params=_cparams(1),
        name="cast_layer",
    )(w)


def _mm_kernel(x_ref, w_ref, o_ref):
    o_ref[...] = _dot(x_ref[...], w_ref[...]).astype(o_ref.dtype)


def matmul(x, w, *, bm, bn, out_dtype=F32):
    m, k = x.shape
    n = w.shape[1]
    return pl.pallas_call(
        _mm_kernel,
        grid=(m // bm, n // bn),
        in_specs=[pl.BlockSpec((bm, k), lambda i, j: (i, 0)), pl.BlockSpec((k, bn), lambda i, j: (0, j))],
        out_specs=pl.BlockSpec((bm, bn), lambda i, j: (i, j)),
        out_shape=SDS((m, n), out_dtype),
        compiler_params=_cparams(2),
        name="matmul",
    )(x, w)


def _mm_cast_kernel(x_ref, w_ref, o_ref, wb_ref):
    wb_ref[...] = w_ref[0].astype(BF16)
    o_ref[...] = _dot(x_ref[...], wb_ref[...]).astype(o_ref.dtype)


def matmul_cast(x, w, layer, *, bn, out_dtype=F32):
    m, k = x.shape
    n = w.shape[2]
    return pl.pallas_call(
        _mm_cast_kernel,
        grid=(n // bn,),
        in_specs=[pl.BlockSpec((m, k), lambda j: (0, 0)), pl.BlockSpec((1, k, bn), lambda j: (layer, 0, j))],
        out_specs=[pl.BlockSpec((m, bn), lambda j: (0, j)), pl.BlockSpec((k, bn), lambda j: (0, j))],
        out_shape=[SDS((m, n), out_dtype), SDS((k, n), BF16)],
        compiler_params=_cparams(1),
        name="matmul_cast",
    )(x, w)


def _glu_kernel(z_ref, zt_ref, w_ref, o_ref):
    a = _dot(z_ref[...], w_ref[...])
    o_ref[...] = (zt_ref[...].astype(F32) * jax.nn.sigmoid(a)).astype(o_ref.dtype)


def glu(z, w, *, bm, bn):
    m, k = z.shape
    n = w.shape[1]
    return pl.pallas_call(
        _glu_kernel,
        grid=(m // bm, n // bn),
        in_specs=[pl.BlockSpec((bm, k), lambda i, j: (i, 0)), pl.BlockSpec((bm, bn), lambda i, j: (i, j)),
                  pl.BlockSpec((k, bn), lambda i, j: (0, j))],
        out_specs=pl.BlockSpec((bm, bn), lambda i, j: (i, j)),
        out_shape=SDS((m, n), BF16),
        compiler_params=_cparams(2),
        name="glu",
    )(z, z, w)


def _proj_ln_kernel(*refs, n_in):
    xs = refs[:n_in]
    w_ref, res_ref, g_ref, b_ref, y_ref, yb_ref = refs[n_in:]
    acc = None
    off = 0
    for x_ref in xs:
        kx = x_ref.shape[1]
        part = _dot(x_ref[...], w_ref[off:off + kx, :])
        acc = part if acc is None else acc + part
        off += kx
    y = _layer_norm_rows(ALPHA * res_ref[...] + acc, g_ref[...], b_ref[...])
    y_ref[...] = y
    yb_ref[...] = y.astype(BF16)


def proj_res_ln(xs, w, res, g, b, *, bm):
    m, d = res.shape
    k = w.shape[0]
    row = lambda i: (i, 0)
    fixed = lambda i: (0, 0)
    return pl.pallas_call(
        functools.partial(_proj_ln_kernel, n_in=len(xs)),
        grid=(m // bm,),
        in_specs=[pl.BlockSpec((bm, x.shape[1]), row) for x in xs] + [
            pl.BlockSpec((k, d), fixed, pipeline_mode=pl.Buffered(1)),
            pl.BlockSpec((bm, d), row),
            pl.BlockSpec((1, d), fixed),
            pl.BlockSpec((1, d), fixed),
        ],
        out_specs=[pl.BlockSpec((bm, d), row), pl.BlockSpec((bm, d), row)],
        out_shape=[SDS((m, d), F32), SDS((m, d), BF16)],
        compiler_params=_cparams(1),
        name="proj_res_ln",
    )(*xs, w, res, g.reshape(1, d), b.reshape(1, d))


def _hg_out_kernel(o_ref, gate_ref, ng_ref, w_ref, res_ref, g_ref, b_ref, y_ref, yb_ref):
    o = o_ref[...]
    xn = o * lax.rsqrt(jnp.mean(o * o, axis=-1, keepdims=True) + RMS_EPS) * ng_ref[...]
    xn = xn * jax.nn.sigmoid(gate_ref[...])
    mix = _dot(xn.astype(BF16), w_ref[...])
    y = _layer_norm_rows(ALPHA * res_ref[...] + mix, g_ref[...], b_ref[...])
    y_ref[...] = y
    yb_ref[...] = y.astype(BF16)


def hg_out_ln(o, h_all, norm_g, w, res, g, b, *, bm):
    m, d = res.shape
    row = lambda i: (i, 0)
    fixed = lambda i: (0, 0)
    return pl.pallas_call(
        _hg_out_kernel,
        grid=(m // bm,),
        in_specs=[
            pl.BlockSpec((bm, d), row),
            pl.BlockSpec((bm, d), lambda i: (i, 3)),
            pl.BlockSpec((1, d), fixed),
            pl.BlockSpec((d, d), fixed, pipeline_mode=pl.Buffered(1)),
            pl.BlockSpec((bm, d), row),
            pl.BlockSpec((1, d), fixed),
            pl.BlockSpec((1, d), fixed),
        ],
        out_specs=[pl.BlockSpec((bm, d), row), pl.BlockSpec((bm, d), row)],
        out_shape=[SDS((m, d), F32), SDS((m, d), BF16)],
        compiler_params=_cparams(1),
        name="hg_out_ln",
    )(o, h_all, norm_g.reshape(1, d), w, res, g.reshape(1, d), b.reshape(1, d))


def _ffn_step(f, x_ref, wg, wu, wd, g_ref, b_ref, y_ref, yb_ref):
    @pl.when(f == 0)
    def _():
        x = x_ref[...]
        yb_ref[...] = x.astype(BF16)
        y_ref[...] = ALPHA * x

    xb = yb_ref[...]
    gate = _dot(xb, wg)
    up = _dot(xb, wu)
    h = (gate * jax.nn.sigmoid(gate) * up).astype(BF16)
    y_ref[...] += _dot(h, wd)

    @pl.when(f == pl.num_programs(1) - 1)
    def _():
        g, b = g_ref[...], b_ref[...]
        chunk = math.gcd(y_ref.shape[0], LN_ROW_CHUNK)

        def ln_chunk(i, _):
            rows = pl.ds(pl.multiple_of(i * chunk, chunk), chunk)
            y = _layer_norm_rows(y_ref[rows, :], g, b)
            y_ref[rows, :] = y
            yb_ref[rows, :] = y.astype(BF16)
            return 0

        lax.fori_loop(0, y_ref.shape[0] // chunk, ln_chunk, 0)


def _ffn_kernel(x_ref, wg_ref, wu_ref, wd_ref, g_ref, b_ref, y_ref, yb_ref):
    _ffn_step(pl.program_id(1), x_ref, wg_ref[...], wu_ref[...], wd_ref[...], g_ref, b_ref, y_ref, yb_ref)


def ffn_ln(x, wg, wu, wd, g, b, *, bm, bf=256):
    m, d = x.shape
    dff = wg.shape[1]
    return pl.pallas_call(
        _ffn_kernel,
        grid=(m // bm, dff // bf),
        in_specs=[
            pl.BlockSpec((bm, d), lambda i, f: (i, 0)),
            pl.BlockSpec((d, bf), lambda i, f: (0, f)),
            pl.BlockSpec((d, bf), lambda i, f: (0, f)),
            pl.BlockSpec((bf, d), lambda i, f: (f, 0)),
            pl.BlockSpec((1, d), lambda i, f: (0, 0)),
            pl.BlockSpec((1, d), lambda i, f: (0, 0)),
        ],
        out_specs=[pl.BlockSpec((bm, d), lambda i, f: (i, 0)), pl.BlockSpec((bm, d), lambda i, f: (i, 0))],
        out_shape=[SDS((m, d), F32), SDS((m, d), BF16)],
        compiler_params=_cparams(2, vmem=VMEM_LIMIT_MAX),
        name="ffn_ln",
    )(x, wg, wu, wd, g.reshape(1, d), b.reshape(1, d))


def _ffn_cast_kernel(x_ref, wg_ref, wu_ref, wd_ref, g_ref, b_ref, y_ref, yb_ref, wgb_ref, wub_ref, wdb_ref):
    wgb_ref[...] = wg_ref[0].astype(BF16)
    wub_ref[...] = wu_ref[0].astype(BF16)
    wdb_ref[...] = wd_ref[0].astype(BF16)
    _ffn_step(pl.program_id(1), x_ref, wgb_ref[...], wub_ref[...], wdb_ref[...], g_ref, b_ref, y_ref, yb_ref)


def ffn_ln_cast(x, wg, wu, wd, layer, g, b, *, bf=256):
    m, d = x.shape
    dff = wg.shape[2]
    fixed = lambda i, f: (0, 0)
    return pl.pallas_call(
        _ffn_cast_kernel,
        grid=(1, dff // bf),
        in_specs=[
            pl.BlockSpec((m, d), fixed),
            pl.BlockSpec((1, d, bf), lambda i, f: (layer, 0, f)),
            pl.BlockSpec((1, d, bf), lambda i, f: (layer, 0, f)),
            pl.BlockSpec((1, bf, d), lambda i, f: (layer, f, 0)),
            pl.BlockSpec((1, d), fixed),
            pl.BlockSpec((1, d), fixed),
        ],
        out_specs=[
            pl.BlockSpec((m, d), fixed),
            pl.BlockSpec((m, d), fixed),
            pl.BlockSpec((d, bf), lambda i, f: (0, f)),
            pl.BlockSpec((d, bf), lambda i, f: (0, f)),
            pl.BlockSpec((bf, d), lambda i, f: (f, 0)),
        ],
        out_shape=[SDS((m, d), F32), SDS((m, d), BF16), SDS((d, dff), BF16), SDS((d, dff), BF16),
                   SDS((dff, d), BF16)],
        compiler_params=_cparams(2),
        name="ffn_ln_cast",
    )(x, wg, wu, wd, g.reshape(1, d), b.reshape(1, d))


def _cmul(ar, ai, br, bi):
    return ar * br - ai * bi, ar * bi + ai * br


def _s5_coefs(a_r, a_i, period):
    l = a_r.shape[1]
    pows = [(a_r, a_i)]
    for _ in range(period - 1):
        pows.append(_cmul(pows[-1][0], pows[-1][1], a_r, a_i))
    t = lax.broadcasted_iota(jnp.int32, (8, l), 0) & (period - 1)
    shifts = []
    k = 1
    while k < period:
        keep = t >= k
        shifts.append((jnp.where(keep, pows[k - 1][0], 0.0), jnp.where(keep, pows[k - 1][1], 0.0)))
        k *= 2
    p_r = jnp.broadcast_to(pows[0][0], (8, l))
    p_i = jnp.broadcast_to(pows[0][1], (8, l))
    for j in range(1, period):
        p_r = jnp.where(t == j, pows[j][0], p_r)
        p_i = jnp.where(t == j, pows[j][1], p_i)
    return shifts, (p_r, p_i)


def _s5_scan_tile(x_r, x_i, shifts):
    k = 1
    for c_r, c_i in shifts:
        s_r = pltpu.roll(x_r, k, 0)
        s_i = pltpu.roll(x_i, k, 0)
        x_r, x_i = x_r + c_r * s_r - c_i * s_i, x_i + c_r * s_i + c_i * s_r
        k *= 2
    return x_r, x_i


def _s5_prompt_kernel(u_ref, bw_ref, cw_ref, ar_ref, ai_ref, d_ref, z_ref, hfin_ref, bu_ref, carry_ref):
    tt = pl.program_id(2)
    n_rows = bu_ref.shape[0]

    @pl.when(tt == 0)
    def _():
        carry_ref[...] = jnp.zeros_like(carry_ref)

    bu_ref[...] = _dot(u_ref[0].astype(BF16), bw_ref[0])
    shifts, (p_r, p_i) = _s5_coefs(ar_ref[0], ai_ref[0], 8)

    ch = S5_SCAN_CHUNK
    d_skip = d_ref[0]

    def scan_chunk(c, carry):
        for i in range(ch // 8):
            h_r, h_i = carry
            r0 = pl.multiple_of(c * ch + i * 8, 8)
            x_r, x_i = _s5_scan_tile(bu_ref[pl.ds(r0, 8), 0:S5_HW], bu_ref[pl.ds(r0, 8), S5_HW:2 * S5_HW], shifts)
            x_r, x_i = x_r + p_r * h_r - p_i * h_i, x_i + p_r * h_i + p_i * h_r
            bu_ref[pl.ds(r0, 8), 0:S5_HW] = x_r
            bu_ref[pl.ds(r0, 8), S5_HW:2 * S5_HW] = x_i
            carry = (x_r[7:8], x_i[7:8])
        return carry

    def emit_chunk(c):
        rows = pl.ds(pl.multiple_of(c * ch, ch), ch)
        y = _dot(bu_ref[rows, :].astype(BF16), cw_ref[0]) + d_skip * u_ref[0, rows, :]
        z_ref[0, rows, :] = jax.nn.gelu(y).astype(z_ref.dtype)

    def body(c, carry):
        emit_chunk(c - 1)
        return scan_chunk(c, carry)

    carry = scan_chunk(0, (carry_ref[:, 0:S5_HW], carry_ref[:, S5_HW:2 * S5_HW]))
    h_r, h_i = lax.fori_loop(1, n_rows // ch, body, carry)
    emit_chunk(n_rows // ch - 1)
    carry_ref[:, 0:S5_HW] = h_r
    carry_ref[:, S5_HW:2 * S5_HW] = h_i

    @pl.when(tt == pl.num_programs(2) - 1)
    def _():
        hfin_ref[0, 0] = carry_ref[...]


def s5_prompt(h_all, bw, cw, a_r, a_i, d_skip, *, tb):
    bsz, t, _ = h_all.shape
    blk = lambda b, g, s: (g, 0, 0)
    return pl.pallas_call(
        _s5_prompt_kernel,
        grid=(bsz, S5_NGB, t // tb),
        in_specs=[
            pl.BlockSpec((1, tb, S5_UW), lambda b, g, s: (b, s, g)),
            pl.BlockSpec((1, S5_UW, 2 * S5_HW), blk),
            pl.BlockSpec((1, 2 * S5_HW, S5_UW), blk),
            pl.BlockSpec((1, 1, S5_HW), blk),
            pl.BlockSpec((1, 1, S5_HW), blk),
            pl.BlockSpec((1, 1, S5_UW), blk),
        ],
        out_specs=[
            pl.BlockSpec((1, tb, S5_UW), lambda b, g, s: (b, s, g)),
            pl.BlockSpec((1, 1, 1, 2 * S5_HW), lambda b, g, s: (b, g, 0, 0)),
        ],
        out_shape=[SDS((bsz, t, D_S5), BF16), SDS((bsz, S5_NGB, 1, 2 * S5_HW), F32)],
        scratch_shapes=[pltpu.VMEM((tb, 2 * S5_HW), F32), pltpu.VMEM((1, 2 * S5_HW), F32)],
        compiler_params=_cparams(3),
        name="s5_prompt",
    )(h_all, bw, cw, a_r, a_i, d_skip)


def _s5_sample_kernel(u_ref, h0_ref, bw_ref, cw_ref, ar_ref, ai_ref, d_ref, z_ref, h_ref, *, period):
    u = u_ref[...]
    bu = _dot(u.astype(BF16), bw_ref[0])
    shifts, (p_r, p_i) = _s5_coefs(ar_ref[0], ai_ref[0], period)
    for i in range(u.shape[0] // 8):
        rows = slice(i * 8, (i + 1) * 8)
        x_r, x_i = _s5_scan_tile(bu[rows, 0:S5_HW], bu[rows, S5_HW:2 * S5_HW], shifts)
        h0_r = h0_ref[rows, 0:S5_HW]
        h0_i = h0_ref[rows, S5_HW:2 * S5_HW]
        h_ref[rows, 0:S5_HW] = x_r + p_r * h0_r - p_i * h0_i
        h_ref[rows, S5_HW:2 * S5_HW] = x_i + p_r * h0_i + p_i * h0_r
    y = _dot(h_ref[...].astype(BF16), cw_ref[0]) + d_ref[0] * u
    z_ref[...] = jax.nn.gelu(y).astype(z_ref.dtype)


def s5_sample(h_all, h0x, bw, cw, a_r, a_i, d_skip, *, period):
    m = h_all.shape[0]
    blk = lambda g: (g, 0, 0)
    return pl.pallas_call(
        functools.partial(_s5_sample_kernel, period=period),
        grid=(S5_NGB,),
        in_specs=[
            pl.BlockSpec((m, S5_UW), lambda g: (0, g)),
            pl.BlockSpec((m, 2 * S5_HW), lambda g: (0, g)),
            pl.BlockSpec((1, S5_UW, 2 * S5_HW), blk),
            pl.BlockSpec((1, 2 * S5_HW, S5_UW), blk),
            pl.BlockSpec((1, 1, S5_HW), blk),
            pl.BlockSpec((1, 1, S5_HW), blk),
            pl.BlockSpec((1, 1, S5_UW), blk),
        ],
        out_specs=[pl.BlockSpec((m, S5_UW), lambda g: (0, g)), pl.BlockSpec((m, 2 * S5_HW), lambda g: (0, g))],
        out_shape=[SDS((m, D_S5), BF16), SDS((m, S5_NGB * 2 * S5_HW), F32)],
        compiler_params=_cparams(1),
        name="s5_sample",
    )(h_all, h0x, bw, cw, a_r, a_i, d_skip)


def _bias_kernel(rb_ref, o_ref):
    h = pl.program_id(0)
    q = lax.broadcasted_iota(jnp.int32, (WINDOW, 2 * WINDOW), 0)
    k = lax.broadcasted_iota(jnp.int32, (WINDOW, 2 * WINDOW), 1)
    n = jnp.maximum(q + WINDOW - k, 0)
    max_exact = N_BUCKETS // 2
    nf = jnp.maximum(n, 1).astype(F32)
    large = max_exact + (jnp.log(nf / max_exact) / math.log(MAX_DISTANCE / max_exact)
                         * (N_BUCKETS - max_exact)).astype(jnp.int32)
    large = jnp.minimum(large, N_BUCKETS - 1)
    bucket = jnp.where(n < max_exact, n, large)
    out = jnp.zeros((WINDOW, 2 * WINDOW), F32)
    for b in range(N_BUCKETS):
        out = jnp.where(bucket == b, rb_ref[b, h], out)
    o_ref[0] = out


def bias_table(rel_bias):
    return pl.pallas_call(
        _bias_kernel,
        grid=(SWA_HEADS,),
        in_specs=[pl.BlockSpec(memory_space=pltpu.SMEM)],
        out_specs=pl.BlockSpec((1, WINDOW, 2 * WINDOW), lambda h: (h, 0, 0)),
        out_shape=SDS((SWA_HEADS, WINDOW, 2 * WINDOW), F32),
        compiler_params=_cparams(1),
        name="bias_table",
    )(rel_bias)


def _swa_softmax(s, valid, sink_col):
    s = jnp.where(valid, s, NEG_BIG)
    m = jnp.maximum(jnp.max(s, axis=-1, keepdims=True), sink_col)
    p = jnp.exp(s - m)
    return p / (jnp.sum(p, axis=-1, keepdims=True) + jnp.exp(sink_col - m))


def _sink_col(sinks_ref, first_head, rows_per_head):
    n = SWA_GQ * rows_per_head
    r = lax.broadcasted_iota(jnp.int32, (n, 1), 0)
    col = jnp.full((n, 1), sinks_ref[first_head], F32)
    for g in range(1, SWA_GQ):
        col = jnp.where(r >= g * rows_per_head, sinks_ref[first_head + g], col)
    return col


def _swa_prompt_kernel(sinks_ref, q_ref, kp_ref, kc_ref, vp_ref, vc_ref, bias_ref, o_ref):
    pair = pl.program_id(0)
    i = pl.program_id(2)
    blk = WINDOW
    q = q_ref[0].astype(BF16)
    kk = jnp.concatenate([kp_ref[0], kc_ref[0]], axis=0).astype(BF16)
    vv = jnp.concatenate([vp_ref[0], vc_ref[0]], axis=0).astype(BF16)
    qi = lax.broadcasted_iota(jnp.int32, (SWA_GQ * blk, 2 * blk), 0) & (blk - 1)
    ki = lax.broadcasted_iota(jnp.int32, (SWA_GQ * blk, 2 * blk), 1)
    valid = (ki > qi) & (ki <= qi + blk) & ((ki >= blk) | (i > 0))
    outs = []
    for j in range(2):
        k_j = kk[:, j * HEAD_DIM:(j + 1) * HEAD_DIM]
        v_j = vv[:, j * HEAD_DIM:(j + 1) * HEAD_DIM]
        q4 = jnp.concatenate([q[:, (j * SWA_GQ + g) * HEAD_DIM:(j * SWA_GQ + g + 1) * HEAD_DIM]
                              for g in range(SWA_GQ)], axis=0)
        s = _dot_nt(q4, k_j) * (HEAD_DIM ** -0.5)
        s = s + bias_ref[j * SWA_GQ:(j + 1) * SWA_GQ].reshape(SWA_GQ * blk, 2 * blk)
        p = _swa_softmax(s, valid, _sink_col(sinks_ref, pair * 2 * SWA_GQ + j * SWA_GQ, blk))
        o4 = _dot(p.astype(BF16), v_j)
        outs += [o4[g * blk:(g + 1) * blk] for g in range(SWA_GQ)]
    o_ref[0] = jnp.concatenate(outs, axis=1).astype(o_ref.dtype)


def swa_prompt(h_all, sinks, table):
    bsz, t, _ = h_all.shape
    nblk = t // WINDOW
    qw = 2 * SWA_GQ * HEAD_DIM
    kw = 2 * HEAD_DIM
    q0, k0, v0 = D_S5 // qw, (D_S5 + D_SWA_Q) // kw, (D_S5 + D_SWA_Q + D_SWA_KV) // kw
    cur = lambda c0: (lambda p, b, i: (b, i, c0 + p))
    prev = lambda c0: (lambda p, b, i: (b, jnp.maximum(i - 1, 0), c0 + p))
    return pl.pallas_call(
        _swa_prompt_kernel,
        grid=(SWA_KV_HEADS // 2, bsz, nblk),
        in_specs=[
            pl.BlockSpec(memory_space=pltpu.SMEM),
            pl.BlockSpec((1, WINDOW, qw), cur(q0)),
            pl.BlockSpec((1, WINDOW, kw), prev(k0)),
            pl.BlockSpec((1, WINDOW, kw), cur(k0)),
            pl.BlockSpec((1, WINDOW, kw), prev(v0)),
            pl.BlockSpec((1, WINDOW, kw), cur(v0)),
            pl.BlockSpec((2 * SWA_GQ, WINDOW, 2 * WINDOW), lambda p, b, i: (p, 0, 0)),
        ],
        out_specs=pl.BlockSpec((1, WINDOW, qw), lambda p, b, i: (b, i, p)),
        out_shape=SDS((bsz, t, D_SWA_Q), BF16),
        compiler_params=_cparams(3),
        name="swa_prompt",
    )(sinks, h_all, h_all, h_all, h_all, h_all, table)


def _swa_sample_kernel(sinks_ref, q_ref, kk_ref, vv_ref, bias_ref, o_ref, *, t_new):
    rows = q_ref.shape[0]
    n_keys = kk_ref.shape[1]
    q = q_ref[...].astype(BF16)
    r = lax.broadcasted_iota(jnp.int32, (SWA_GQ * rows, n_keys), 0)
    ti = r & (t_new - 1)
    ki = lax.broadcasted_iota(jnp.int32, (SWA_GQ * rows, n_keys), 1)
    valid = (ki > ti) & (ki <= ti + WINDOW)
    second = (r & (rows - 1)) >= t_new
    second_o = (lax.broadcasted_iota(jnp.int32, (SWA_GQ * rows, HEAD_DIM), 0) & (rows - 1)) >= t_new
    outs = []
    for j in range(SWA_KV_HEADS):
        cols = slice(j * HEAD_DIM, (j + 1) * HEAD_DIM)
        q4 = jnp.concatenate([q[:, (j * SWA_GQ + g) * HEAD_DIM:(j * SWA_GQ + g + 1) * HEAD_DIM]
                              for g in range(SWA_GQ)], axis=0)
        s = jnp.where(second, _dot_nt(q4, kk_ref[1][:, cols].astype(BF16)),
                      _dot_nt(q4, kk_ref[0][:, cols].astype(BF16))) * (HEAD_DIM ** -0.5)
        s = s + bias_ref[j * SWA_GQ:(j + 1) * SWA_GQ].reshape(SWA_GQ * rows, n_keys)
        p = _swa_softmax(s, valid, _sink_col(sinks_ref, j * SWA_GQ, rows)).astype(BF16)
        o4 = jnp.where(second_o, _dot(p, vv_ref[1][:, cols].astype(BF16)), _dot(p, vv_ref[0][:, cols].astype(BF16)))
        outs += [o4[g * rows:(g + 1) * rows] for g in range(SWA_GQ)]
    o_ref[...] = jnp.concatenate(outs, axis=1).astype(o_ref.dtype)


def swa_sample(h_all, kk, vv, sinks, bias_s, *, t_new):
    m = h_all.shape[0]
    n_keys = kk.shape[1]
    rows = 2 * t_new
    return pl.pallas_call(
        functools.partial(_swa_sample_kernel, t_new=t_new),
        grid=(m // rows,),
        in_specs=[
            pl.BlockSpec(memory_space=pltpu.SMEM),
            pl.BlockSpec((rows, D_SWA_Q), lambda i: (i, D_S5 // D_SWA_Q)),
            pl.BlockSpec((2, n_keys, D_SWA_KV), lambda i: (i, 0, 0)),
            pl.BlockSpec((2, n_keys, D_SWA_KV), lambda i: (i, 0, 0)),
            pl.BlockSpec((SWA_HEADS, rows, n_keys), lambda i: (0, 0, 0)),
        ],
        out_specs=pl.BlockSpec((rows, D_SWA_Q), lambda i: (i, 0)),
        out_shape=SDS((m, D_SWA_Q), BF16),
        compiler_params=_cparams(1),
        name="swa_sample",
    )(sinks, h_all, kk, vv, bias_s)


def _hg_gates(q, fz, lb):
    qs = q * jax.nn.sigmoid(q)
    f = lb + (1.0 - lb) * jax.nn.sigmoid(fz)
    return qs, jnp.log(f), 1.0 - f


HG_GROUP = 4


def _hgrn_prompt_kernel(q_ref, f_ref, v_ref, lb_ref, o_ref, st_ref, s_scr, *, n_heads):
    tt = pl.program_id(2)
    n_rows = q_ref.shape[1]
    c, ng = HG_CHUNK, HG_GROUP
    gr = c * ng

    @pl.when(tt == 0)
    def _():
        s_scr[...] = jnp.zeros_like(s_scr)

    ri = lax.broadcasted_iota(jnp.int32, (gr, gr), 0)
    ci = lax.broadcasted_iota(jnp.int32, (gr, gr), 1)
    causal = (ri >= ci) & ((ri // c) == (ci // c))
    t_in = lax.broadcasted_iota(jnp.int32, (gr, HG_DK), 0) & (c - 1)
    own = ((lax.broadcasted_iota(jnp.int32, (gr, ng * HG_DK), 0) // c)
           == (lax.broadcasted_iota(jnp.int32, (gr, ng * HG_DK), 1) // HG_DK)).astype(BF16)

    def body(gi, _):
        r0 = pl.multiple_of(gi * gr, gr)
        for hd in range(n_heads):
            cols = slice(hd * HG_DK, (hd + 1) * HG_DK)
            qs, lf, k = _hg_gates(q_ref[0, pl.ds(r0, gr), cols], f_ref[0, pl.ds(r0, gr), cols], lb_ref[:, cols])
            v = v_ref[0, pl.ds(r0, gr), cols].astype(BF16)
            cum = lf
            sh = 1
            while sh < c:
                cum = cum + jnp.where(t_in >= sh, pltpu.roll(cum, sh, 0), 0.0)
                sh *= 2
            last3 = cum.reshape(ng, c, HG_DK)[:, c - 1:c, :]
            last = jnp.broadcast_to(last3, (ng, c, HG_DK)).reshape(gr, HG_DK)
            qt = (qs * jnp.exp(cum)).astype(BF16)
            kt = (k * jnp.exp(-cum)).astype(BF16)
            kl = (k * jnp.exp(last - cum)).astype(BF16)
            decay = jnp.exp(last3)
            attn = jnp.where(causal, _dot_nt(qt, kt), 0.0).astype(BF16)
            kv = _dot_tn(v, jnp.concatenate([kl] * ng, axis=1) * own)
            st = s_scr[hd]
            starts = []
            for j in range(ng):
                starts.append(st)
                st = st * decay[j] + kv[:, j * HG_DK:(j + 1) * HG_DK]
            s_scr[hd] = st
            s_cat = jnp.concatenate(starts, axis=1).astype(BF16)
            o_ref[0, pl.ds(r0, gr), cols] = (_dot(attn, v)
                                            + _dot_nt(jnp.concatenate([qt] * ng, axis=1) * own, s_cat))
        return 0

    lax.fori_loop(0, n_rows // gr, body, 0, unroll=2)

    @pl.when(tt == pl.num_programs(2) - 1)
    def _():
        st_ref[0] = s_scr[...]


def hgrn_prompt(h_all, lb, *, tb, n_heads=2):
    bsz, t, _ = h_all.shape
    hw = n_heads * HG_DK
    nhb = D_MODEL // hw
    col = lambda c0: (lambda b, h, s: (b, s, c0 * nhb + h))
    return pl.pallas_call(
        functools.partial(_hgrn_prompt_kernel, n_heads=n_heads),
        grid=(bsz, nhb, t // tb),
        in_specs=[
            pl.BlockSpec((1, tb, hw), col(0)),
            pl.BlockSpec((1, tb, hw), col(1)),
            pl.BlockSpec((1, tb, hw), col(2)),
            pl.BlockSpec((1, hw), lambda b, h, s: (0, h)),
        ],
        out_specs=[
            pl.BlockSpec((1, tb, hw), lambda b, h, s: (b, s, h)),
            pl.BlockSpec((1, n_heads, HG_DK, HG_DK), lambda b, h, s: (b, h, 0, 0)),
        ],
        out_shape=[SDS((bsz, t, D_MODEL), F32), SDS((bsz, HG_HEADS, HG_DK, HG_DK), F32)],
        scratch_shapes=[pltpu.VMEM((n_heads, HG_DK, HG_DK), F32)],
        compiler_params=_cparams(3),
        name="hgrn_prompt",
    )(h_all, h_all, h_all, lb.reshape(1, D_MODEL))


def _hgrn_sample_kernel(q_ref, f_ref, v_ref, lb_ref, s0_ref, o_ref, s_ref, *, n_heads, t_new):
    rows = q_ref.shape[0]
    ri = lax.broadcasted_iota(jnp.int32, (rows, rows), 0)
    ci = lax.broadcasted_iota(jnp.int32, (rows, rows), 1)
    causal = (ri >= ci) & ((ri >= t_new) == (ci >= t_new))
    tril = causal.astype(F32)
    second = lax.broadcasted_iota(jnp.int32, (rows, HG_DK), 0) >= t_new
    eye = (lax.broadcasted_iota(jnp.int32, (HG_DK, HG_DK), 0)
           == lax.broadcasted_iota(jnp.int32, (HG_DK, HG_DK), 1))
    for hd in range(n_heads):
        cols = slice(hd * HG_DK, (hd + 1) * HG_DK)
        qs, lf, k = _hg_gates(q_ref[:, cols], f_ref[:, cols], lb_ref[:, cols])
        v = v_ref[:, cols].astype(BF16)
        cum = jnp.dot(tril, lf, preferred_element_type=F32, precision=lax.Precision.HIGHEST)
        last0 = cum[t_new - 1:t_new]
        last1 = cum[rows - 1:rows]
        last = jnp.where(second, last1, last0)
        qt = (qs * jnp.exp(cum)).astype(BF16)
        kt = (k * jnp.exp(-cum)).astype(BF16)
        kl = k * jnp.exp(last - cum)
        attn = jnp.where(causal, _dot_nt(qt, kt), 0.0).astype(BF16)
        s0a = s0_ref[0, hd]
        s0b = s0_ref[1, hd]
        o_ref[:, cols] = _dot(attn, v) + jnp.where(second, _dot(qt, s0b.astype(BF16)), _dot(qt, s0a.astype(BF16)))
        for bb, (s0, lst) in enumerate(((s0a, last0), (s0b, last1))):
            decay = jnp.sum(jnp.where(eye, jnp.exp(lst), 0.0), axis=1, keepdims=True)
            kl_b = jnp.where(second == (bb == 1), kl, 0.0).astype(BF16)
            s_ref[bb, hd] = decay * s0 + _dot_tn(kl_b, v)


def hgrn_sample(h_all, lb, s0, *, t_new, n_heads=8):
    m = h_all.shape[0]
    rows = 2 * t_new
    hw = n_heads * HG_DK
    nhb = D_MODEL // hw
    col = lambda c0: (lambda i, h: (i, c0 * nhb + h))
    return pl.pallas_call(
        functools.partial(_hgrn_sample_kernel, n_heads=n_heads, t_new=t_new),
        grid=(m // rows, nhb),
        in_specs=[
            pl.BlockSpec((rows, hw), col(0)),
            pl.BlockSpec((rows, hw), col(1)),
            pl.BlockSpec((rows, hw), col(2)),
            pl.BlockSpec((1, hw), lambda i, h: (0, h)),
            pl.BlockSpec((2, n_heads, HG_DK, HG_DK), lambda i, h: (i, h, 0, 0)),
        ],
        out_specs=[
            pl.BlockSpec((rows, hw), lambda i, h: (i, h)),
            pl.BlockSpec((2, n_heads, HG_DK, HG_DK), lambda i, h: (i, h, 0, 0)),
        ],
        out_shape=[SDS((m, D_MODEL), F32), SDS(s0.shape, F32)],
        compiler_params=_cparams(2),
        name="hgrn_sample",
    )(h_all, h_all, h_all, lb.reshape(1, D_MODEL), s0)


def _xattn_kernel(y_ref, yb_ref, wq_ref, mk_ref, mv_ref, wo_ref, g_ref, b_ref, o_ref, ob_ref, *, n_seq):
    rows = y_ref.shape[0]
    q = _dot(yb_ref[...], wq_ref[...]).astype(BF16)
    second = lax.broadcasted_iota(jnp.int32, (rows, 1), 0) >= rows // 2
    heads = []
    for h in range(MEM_HEADS):
        cols = slice(h * MEM_HEAD_DIM, (h + 1) * MEM_HEAD_DIM)
        qh = q[:, cols]
        s = _dot_nt(qh, mk_ref[0][:, cols].astype(BF16))
        if n_seq == 2:
            s = jnp.where(second, _dot_nt(qh, mk_ref[1][:, cols].astype(BF16)), s)
        s = s * (MEM_HEAD_DIM ** -0.5)
        e = jnp.exp(s - jnp.max(s, axis=-1, keepdims=True))
        p = (e / jnp.sum(e, axis=-1, keepdims=True)).astype(BF16)
        o = _dot(p, mv_ref[0][:, cols].astype(BF16))
        if n_seq == 2:
            o = jnp.where(second, _dot(p, mv_ref[1][:, cols].astype(BF16)), o)
        heads.append(o.astype(BF16))
    mix = _dot(jnp.concatenate(heads, axis=1), wo_ref[...])
    y = _layer_norm_rows(ALPHA * y_ref[...] + mix, g_ref[...], b_ref[...])
    o_ref[...] = y
    ob_ref[...] = y.astype(BF16)


def xattn_ln(y, yb, wq, mk, mv, wo, g, b, *, bm, rows_per_seq):
    m, d = y.shape
    n_seq = 2 if bm == 2 * rows_per_seq else 1
    assert n_seq == 2 or rows_per_seq % bm == 0
    row = lambda i: (i, 0)
    fixed = lambda i: (0, 0)
    mem = (lambda i: (i, 0, 0)) if n_seq == 2 else (lambda i: (i * bm // rows_per_seq, 0, 0))
    return pl.pallas_call(
        functools.partial(_xattn_kernel, n_seq=n_seq),
        grid=(m // bm,),
        in_specs=[
            pl.BlockSpec((bm, d), row),
            pl.BlockSpec((bm, d), row),
            pl.BlockSpec((d, D_MEM), fixed),
            pl.BlockSpec((n_seq, N_MEM, D_MEM), mem),
            pl.BlockSpec((n_seq, N_MEM, D_MEM), mem),
            pl.BlockSpec((D_MEM, d), fixed),
            pl.BlockSpec((1, d), fixed),
            pl.BlockSpec((1, d), fixed),
        ],
        out_specs=[pl.BlockSpec((bm, d), row), pl.BlockSpec((bm, d), row)],
        out_shape=[SDS((m, d), F32), SDS((m, d), BF16)],
        compiler_params=_cparams(1),
        name="xattn_ln",
    )(y, yb, wq, mk, mv, wo, g.reshape(1, d), b.reshape(1, d))


def _s5_discretize(lam_re, lam_im, log_dt, b_re, b_im):
    lr = jnp.minimum(lam_re.astype(F32), -1e-4)
    li = lam_im.astype(F32)
    dt = jnp.exp(log_dt.astype(F32))[:, None]
    mag = jnp.exp(lr * dt)
    a_re = mag * jnp.cos(li * dt)
    a_im = mag * jnp.sin(li * dt)
    den = lr * lr + li * li
    fr = ((a_re - 1.0) * lr + a_im * li) / den
    fi = (a_im * lr - (a_re - 1.0) * li) / den
    br, bi = b_re.astype(F32), b_im.astype(F32)
    bb_re = fr[..., None] * br - fi[..., None] * bi
    bb_im = fr[..., None] * bi + fi[..., None] * br
    return a_re, a_im, bb_re, bb_im


def _s5_block_weights(a_re, a_im, bb_re, bb_im, c_re, c_im, d_skip):
    eye = jnp.eye(S5_GB, dtype=F32)
    shp_b = (S5_NGB, S5_GB, S5_STATE, S5_GROUP)
    shp_c = (S5_NGB, S5_GB, S5_GROUP, S5_STATE)
    blk_b = lambda w: jnp.einsum('bgph,gk->bghkp', w.reshape(shp_b), eye).reshape(S5_NGB, S5_UW, S5_HW)
    blk_c = lambda w: jnp.einsum('bghp,gk->bkpgh', w.reshape(shp_c), eye).reshape(S5_NGB, S5_HW, S5_UW)
    bw = jnp.concatenate([blk_b(bb_re), blk_b(bb_im)], axis=-1).astype(BF16)
    cw = jnp.concatenate([blk_c(c_re.astype(F32)), blk_c(-c_im.astype(F32))], axis=1).astype(BF16)
    return (bw, cw, a_re.reshape(S5_NGB, 1, S5_HW), a_im.reshape(S5_NGB, 1, S5_HW),
            d_skip.astype(F32).reshape(S5_NGB, 1, S5_UW))


def _s5_state_to_blocks(s_re, s_im):
    b = s_re.shape[0]
    return jnp.concatenate([s_re.reshape(b, S5_NGB, S5_HW), s_im.reshape(b, S5_NGB, S5_HW)],
                           axis=-1).reshape(b, S5_NGB * 2 * S5_HW)


def _s5_state_from_blocks(h):
    b = h.shape[0]
    return (h[:, :, :S5_HW].reshape(b, S5_GROUPS, S5_STATE), h[:, :, S5_HW:].reshape(b, S5_GROUPS, S5_STATE))


def kernel(x_prompt, x_sample, cache_mem_k, cache_mem_v, cache_swa_k, cache_swa_v, state_s5_re, state_s5_im, state_hgrn, mem_prompt, rel_bias, w_even_in, s5_lam_re, s5_lam_im, s5_log_dt, s5_b_re, s5_b_im, s5_c_re, s5_c_im, s5_d, s5_w_glu, swa_sinks, w_even_out, hg_lb_logits, w_odd_in, hg_norm_g, w_odd_out, w_mem_q, w_mem_k, w_mem_v, w_mem_o, w_ffn_gate, w_ffn_up, w_ffn_down, ln_g, ln_b):
    bsz, seq, d = x_prompt.shape
    dec_b, dec_t, _ = x_sample.shape
    mp, ms = bsz * seq, dec_b * dec_t
    w_buf = cache_swa_k.shape[2]
    bf = lambda w: w.astype(BF16)

    lb_soft = jax.nn.softmax(hg_lb_logits.astype(F32), axis=0)
    lower_bounds = jnp.cumsum(lb_soft, axis=0) - lb_soft[0]
    table = bias_table(rel_bias.astype(F32))
    bias_s = jnp.tile(table[:, :dec_t], (1, 2, 1))
    mem_b = bf(mem_prompt).reshape(bsz * N_MEM, d)

    yp = x_prompt.reshape(mp, d)
    ys = x_sample.reshape(ms, d)
    ypb, ysb = bf(yp), bf(ys)
    p_mem_k, p_mem_v = [], []
    p_swa_k, p_swa_v, p_s5_re, p_s5_im, p_hg = [], [], [], [], []
    s_swa_k, s_swa_v, s_s5_re, s_s5_im, s_hg = [], [], [], [], []

    for l in range(DEPTH):
        j = l // 2
        g0, b0 = ln_g[l, 0], ln_b[l, 0]
        if l % 2 == 0:
            a_re, a_im, bb_re, bb_im = _s5_discretize(s5_lam_re[j], s5_lam_im[j], s5_log_dt[j], s5_b_re[j], s5_b_im[j])
            s5w = _s5_block_weights(a_re, a_im, bb_re, bb_im, s5_c_re[j], s5_c_im[j], s5_d[j])
            w_glu, w_out = cast_layer(s5_w_glu, j), cast_layer(w_even_out, j)
            sinks = swa_sinks[j].astype(F32)
            kv0 = D_S5 + D_SWA_Q

            hs, w_in = matmul_cast(ysb, w_even_in, j, bn=512)
            hp = matmul(ypb, w_in, bm=1024, bn=1024).reshape(bsz, seq, -1)
            z, hfin = s5_prompt(hp, *s5w, tb=512)
            s5_out = glu(z.reshape(mp, D_S5), w_glu, bm=1024, bn=1024)
            att = swa_prompt(hp, sinks, table).reshape(mp, D_SWA_Q)
            hr, hi = _s5_state_from_blocks(hfin[:, :, 0])
            p_s5_re.append(hr); p_s5_im.append(hi)
            p_swa_k.append(hp[:, seq - w_buf:, kv0:kv0 + D_SWA_KV].reshape(bsz, w_buf, SWA_KV_HEADS, HEAD_DIM))
            p_swa_v.append(hp[:, seq - w_buf:, kv0 + D_SWA_KV:].reshape(bsz, w_buf, SWA_KV_HEADS, HEAD_DIM))
            yp, ypb = proj_res_ln([s5_out, att], w_out, yp, g0, b0, bm=128)

            h0x = jnp.repeat(_s5_state_to_blocks(state_s5_re[j].astype(F32), state_s5_im[j].astype(F32)), dec_t, axis=0)
            z, h_steps = s5_sample(hs, h0x, *s5w, period=dec_t)
            s5_out = glu(z, w_glu, bm=ms, bn=1024)
            hs3 = hs.reshape(dec_b, dec_t, -1)
            kpad = jnp.zeros((dec_b, 2 * WINDOW - w_buf - dec_t, D_SWA_KV), F32)
            kk = jnp.concatenate([cache_swa_k[j].reshape(dec_b, w_buf, D_SWA_KV).astype(F32),
                                  hs3[:, :, kv0:kv0 + D_SWA_KV], kpad], axis=1)
            vv = jnp.concatenate([cache_swa_v[j].reshape(dec_b, w_buf, D_SWA_KV).astype(F32),
                                  hs3[:, :, kv0 + D_SWA_KV:], kpad], axis=1)
            att = swa_sample(hs, kk, vv, sinks, bias_s, t_new=dec_t)
            hr, hi = _s5_state_from_blocks(h_steps.reshape(dec_b, dec_t, S5_NGB, 2 * S5_HW)[:, dec_t - 1])
            s_s5_re.append(hr); s_s5_im.append(hi)
            s_swa_k.append(kk[:, dec_t:dec_t + w_buf].reshape(dec_b, w_buf, SWA_KV_HEADS, HEAD_DIM))
            s_swa_v.append(vv[:, dec_t:dec_t + w_buf].reshape(dec_b, w_buf, SWA_KV_HEADS, HEAD_DIM))
            ys, ysb = proj_res_ln([s5_out, att], w_out, ys, g0, b0, bm=ms)
        else:
            w_out = cast_layer(w_odd_out, j)
            lb = lower_bounds[l]

            hs, w_in = matmul_cast(ysb, w_odd_in, j, bn=512)
            hp = matmul(ypb, w_in, bm=1024, bn=1024)
            o, st = hgrn_prompt(hp.reshape(bsz, seq, -1), lb, tb=512, n_heads=4)
            p_hg.append(jnp.swapaxes(st, -1, -2))
            yp, ypb = hg_out_ln(o.reshape(mp, d), hp, hg_norm_g[j], w_out, yp, g0, b0, bm=128)

            o, s_new = hgrn_sample(hs, lb, state_hgrn[j].astype(F32), t_new=dec_t)
            s_hg.append(s_new)
            ys, ysb = hg_out_ln(o, hs, hg_norm_g[j], w_out, ys, g0, b0, bm=ms)

        wq, wk, wv, wo = (cast_layer(w, l) for w in (w_mem_q, w_mem_k, w_mem_v, w_mem_o))
        mk = matmul(mem_b, wk, bm=bsz * N_MEM, bn=D_MEM).reshape(bsz, N_MEM, D_MEM)
        mv = matmul(mem_b, wv, bm=bsz * N_MEM, bn=D_MEM).reshape(bsz, N_MEM, D_MEM)
        p_mem_k.append(mk.reshape(bsz, N_MEM, MEM_HEADS, MEM_HEAD_DIM))
        p_mem_v.append(mv.reshape(bsz, N_MEM, MEM_HEADS, MEM_HEAD_DIM))
        yp, ypb = xattn_ln(yp, ypb, wq, mk, mv, wo, ln_g[l, 1], ln_b[l, 1], bm=256, rows_per_seq=seq)
        ys, ysb = xattn_ln(ys, ysb, wq, cache_mem_k[l].reshape(dec_b, N_MEM, D_MEM).astype(F32),
                           cache_mem_v[l].reshape(dec_b, N_MEM, D_MEM).astype(F32), wo,
                           ln_g[l, 1], ln_b[l, 1], bm=2 * dec_t, rows_per_seq=dec_t)

        ys, ysb, wg, wu, wd = ffn_ln_cast(ys, w_ffn_gate, w_ffn_up, w_ffn_down, l, ln_g[l, 2], ln_b[l, 2])
        yp, ypb = ffn_ln(yp, wg, wu, wd, ln_g[l, 2], ln_b[l, 2], bm=512)

    return (yp.reshape(bsz, seq, d), ys.reshape(dec_b, dec_t, d),
            jnp.stack(p_mem_k), jnp.stack(p_mem_v),
            jnp.stack(p_swa_k), jnp.stack(p_swa_v),
            jnp.stack(p_s5_re), jnp.stack(p_s5_im), jnp.stack(p_hg),
            jnp.stack(s_swa_k), jnp.stack(s_swa_v),
            jnp.stack(s_s5_re), jnp.stack(s_s5_im), jnp.stack(s_hg))
```

```python
import functools
import math

import jax
import jax.numpy as jnp
from jax import lax
from jax.experimental import pallas as pl
from jax.experimental.pallas import tpu as pltpu

F32 = jnp.float32
BF16 = jnp.bfloat16
SDS = jax.ShapeDtypeStruct

D_MODEL = 4096
DEPTH = 2
ALPHA = (2 * DEPTH) ** 0.25
LN_EPS = 1e-5
RMS_EPS = 1e-6
NEG_BIG = -1e30

S5_GROUPS, S5_GROUP, S5_STATE = 128, 16, 64
D_S5 = S5_GROUPS * S5_GROUP
S5_GB = 16
S5_NGB = S5_GROUPS // S5_GB
S5_UW = S5_GB * S5_GROUP
S5_HW = S5_GB * S5_STATE
S5_SCAN_CHUNK = 128

HEAD_DIM, SWA_HEADS, SWA_KV_HEADS, SWA_GQ = 64, 32, 8, 4
D_SWA_Q, D_SWA_KV = 2048, 512
WINDOW = 128
N_BUCKETS, MAX_DISTANCE = 32, 128

HG_DK, HG_HEADS, HG_CHUNK = 128, 32, 32
N_MEM, MEM_HEADS, MEM_HEAD_DIM, D_MEM = 256, 4, 128, 512

VMEM_LIMIT = 56 * 1024 * 1024
VMEM_LIMIT_MAX = 60 * 1024 * 1024
PROJ_COL_CHUNKS = 4
LN_ROW_CHUNK = 64


def _cparams(n_axes, vmem=VMEM_LIMIT):
    return pltpu.CompilerParams(dimension_semantics=("arbitrary",) * n_axes, vmem_limit_bytes=vmem)


def _dot(a, b):
    return jnp.dot(a, b, preferred_element_type=F32)


def _dot_nt(a, b):
    return lax.dot_general(a, b, (((1,), (1,)), ((), ())), preferred_element_type=F32)


def _dot_tn(a, b):
    return lax.dot_general(a, b, (((0,), (0,)), ((), ())), preferred_element_type=F32)


def _layer_norm_rows(s, g, b):
    mu = jnp.mean(s, axis=-1, keepdims=True)
    c = s - mu
    var = jnp.mean(c * c, axis=-1, keepdims=True)
    return c * lax.rsqrt(var + LN_EPS) * g + b


def _proj_res_ln_chunked(xs, w_ref, res_ref, g_ref, b_ref, y_ref, yb_ref):
    d = y_ref.shape[1]
    cw = d // PROJ_COL_CHUNKS
    total = None
    for j in range(PROJ_COL_CHUNKS):
        cols = slice(j * cw, (j + 1) * cw)
        s = ALPHA * res_ref[:, cols]
        off = 0
        for x in xs:
            s = s + _dot(x, w_ref[off:off + x.shape[1], cols])
            off += x.shape[1]
        y_ref[:, cols] = s
        part = jnp.sum(s, axis=-1, keepdims=True)
        total = part if total is None else total + part
    mu = total * (1.0 / d)
    sq = None
    for j in range(PROJ_COL_CHUNKS):
        c = y_ref[:, j * cw:(j + 1) * cw] - mu
        part = jnp.sum(c * c, axis=-1, keepdims=True)
        sq = part if sq is None else sq + part
    rstd = lax.rsqrt(sq * (1.0 / d) + LN_EPS)
    for j in range(PROJ_COL_CHUNKS):
        cols = slice(j * cw, (j + 1) * cw)
        y = (y_ref[:, cols] - mu) * rstd * g_ref[:, cols] + b_ref[:, cols]
        y_ref[:, cols] = y
        yb_ref[:, cols] = y.astype(BF16)


def _cast_kernel(w_ref, o_ref):
    o_ref[...] = w_ref[0].astype(o_ref.dtype)


CAST_BLOCK_BYTES = 8 * 1024 * 1024


def cast_layer(w, layer):
    _, k, n = w.shape
    rows = next(r for r in (4096, 2048, 1024, 512, 256, 128) if k % r == 0 and r * n * 4 <= CAST_BLOCK_BYTES)
    return pl.pallas_call(
        _cast_kernel,
        grid=(k // rows,),
        in_specs=[pl.BlockSpec((1, rows, n), lambda i: (layer, i, 0))],
        out_specs=pl.BlockSpec((rows, n), lambda i: (i, 0)),
        out_shape=SDS((k, n), BF16),
        compiler_params=_cparams(1),
        name="cast_layer",
    )(w)


def _mm_kernel(x_ref, w_ref, o_ref):
    o_ref[...] = _dot(x_ref[...], w_ref[...]).astype(o_ref.dtype)


def matmul(x, w, *, bm, bn, out_dtype=F32):
    m, k = x.shape
    n = w.shape[1]
    return pl.pallas_call(
        _mm_kernel,
        grid=(m // bm, n // bn),
        in_specs=[pl.BlockSpec((bm, k), lambda i, j: (i, 0)), pl.BlockSpec((k, bn), lambda i, j: (0, j))],
        out_specs=pl.BlockSpec((bm, bn), lambda i, j: (i, j)),
        out_shape=SDS((m, n), out_dtype),
        compiler_params=_cparams(2),
        name="matmul",
    )(x, w)


def _mm_cast_kernel(x_ref, w_ref, o_ref, wb_ref):
    wb_ref[...] = w_ref[0].astype(BF16)
    o_ref[...] = _dot(x_ref[...], wb_ref[...]).astype(o_ref.dtype)


def matmul_cast(x, w, layer, *, bn, out_dtype=F32):
    m, k = x.shape
    n = w.shape[2]
    return pl.pallas_call(
        _mm_cast_kernel,
        grid=(n // bn,),
        in_specs=[pl.BlockSpec((m, k), lambda j: (0, 0)), pl.BlockSpec((1, k, bn), lambda j: (layer, 0, j))],
        out_specs=[pl.BlockSpec((m, bn), lambda j: (0, j)), pl.BlockSpec((k, bn), lambda j: (0, j))],
        out_shape=[SDS((m, n), out_dtype), SDS((k, n), BF16)],
        compiler_params=_cparams(1),
        name="matmul_cast",
    )(x, w)


def _glu_kernel(z_ref, zt_ref, w_ref, o_ref):
    a = _dot(z_ref[...], w_ref[...])
    o_ref[...] = (zt_ref[...].astype(F32) * jax.nn.sigmoid(a)).astype(o_ref.dtype)


def glu(z, w, *, bm, bn):
    m, k = z.shape
    n = w.shape[1]
    return pl.pallas_call(
        _glu_kernel,
        grid=(m // bm, n // bn),
        in_specs=[pl.BlockSpec((bm, k), lambda i, j: (i, 0)), pl.BlockSpec((bm, bn), lambda i, j: (i, j)),
                  pl.BlockSpec((k, bn), lambda i, j: (0, j))],
        out_specs=pl.BlockSpec((bm, bn), lambda i, j: (i, j)),
        out_shape=SDS((m, n), BF16),
        compiler_params=_cparams(2),
        name="glu",
    )(z, z, w)


def _proj_ln_kernel(*refs, n_in):
    xs = refs[:n_in]
    w_ref, res_ref, g_ref, b_ref, y_ref, yb_ref = refs[n_in:]
    _proj_res_ln_chunked([x_ref[...] for x_ref in xs], w_ref, res_ref, g_ref, b_ref, y_ref, yb_ref)


def proj_res_ln(xs, w, res, g, b, *, bm):
    m, d = res.shape
    k = w.shape[0]
    row = lambda i: (i, 0)
    fixed = lambda i: (0, 0)
    return pl.pallas_call(
        functools.partial(_proj_ln_kernel, n_in=len(xs)),
        grid=(m // bm,),
        in_specs=[pl.BlockSpec((bm, x.shape[1]), row) for x in xs] + [
            pl.BlockSpec((k, d), fixed, pipeline_mode=pl.Buffered(1)),
            pl.BlockSpec((bm, d), row),
            pl.BlockSpec((1, d), fixed),
            pl.BlockSpec((1, d), fixed),
        ],
        out_specs=[pl.BlockSpec((bm, d), row), pl.BlockSpec((bm, d), row)],
        out_shape=[SDS((m, d), F32), SDS((m, d), BF16)],
        compiler_params=_cparams(1),
        name="proj_res_ln",
    )(*xs, w, res, g.reshape(1, d), b.reshape(1, d))


def _hg_out_kernel(o_ref, gate_ref, ng_ref, w_ref, res_ref, g_ref, b_ref, y_ref, yb_ref):
    o = o_ref[...]
    xn = o * lax.rsqrt(jnp.mean(o * o, axis=-1, keepdims=True) + RMS_EPS) * ng_ref[...]
    xn = xn * jax.nn.sigmoid(gate_ref[...])
    _proj_res_ln_chunked([xn.astype(BF16)], w_ref, res_ref, g_ref, b_ref, y_ref, yb_ref)


def hg_out_ln(o, h_all, norm_g, w, res, g, b, *, bm):
    m, d = res.shape
    row = lambda i: (i, 0)
    fixed = lambda i: (0, 0)
    return pl.pallas_call(
        _hg_out_kernel,
        grid=(m // bm,),
        in_specs=[
            pl.BlockSpec((bm, d), row),
            pl.BlockSpec((bm, d), lambda i: (i, 3)),
            pl.BlockSpec((1, d), fixed),
            pl.BlockSpec((d, d), fixed, pipeline_mode=pl.Buffered(1)),
            pl.BlockSpec((bm, d), row),
            pl.BlockSpec((1, d), fixed),
            pl.BlockSpec((1, d), fixed),
        ],
        out_specs=[pl.BlockSpec((bm, d), row), pl.BlockSpec((bm, d), row)],
        out_shape=[SDS((m, d), F32), SDS((m, d), BF16)],
        compiler_params=_cparams(1),
        name="hg_out_ln",
    )(o, h_all, norm_g.reshape(1, d), w, res, g.reshape(1, d), b.reshape(1, d))


def _ffn_step(f, x_ref, wg_ref, wu_ref, wd_ref, g_ref, b_ref, y_ref, yb_ref):
    @pl.when(f == 0)
    def _():
        x = x_ref[...]
        yb_ref[...] = x.astype(BF16)
        y_ref[...] = ALPHA * x

    xb = yb_ref[...]
    gate = _dot(xb, wg_ref[...])
    up = _dot(xb, wu_ref[...])
    h = (gate * jax.nn.sigmoid(gate) * up).astype(BF16)
    y_ref[...] += _dot(h, wd_ref[...])

    @pl.when(f == pl.num_programs(1) - 1)
    def _():
        g, b = g_ref[...], b_ref[...]
        chunk = math.gcd(y_ref.shape[0], LN_ROW_CHUNK)

        def ln_chunk(i, _):
            rows = pl.ds(pl.multiple_of(i * chunk, chunk), chunk)
            y = _layer_norm_rows(y_ref[rows, :], g, b)
            y_ref[rows, :] = y
            yb_ref[rows, :] = y.astype(BF16)
            return 0

        lax.fori_loop(0, y_ref.shape[0] // chunk, ln_chunk, 0)


def _ffn_kernel(x_ref, wg_ref, wu_ref, wd_ref, g_ref, b_ref, y_ref, yb_ref):
    _ffn_step(pl.program_id(1), x_ref, wg_ref.at[0], wu_ref.at[0], wd_ref, g_ref, b_ref, y_ref, yb_ref)


def ffn_ln(x, wg, wu, wd, g, b, *, bm):
    m, d = x.shape
    nf, _, bf = wg.shape
    return pl.pallas_call(
        _ffn_kernel,
        grid=(m // bm, nf),
        in_specs=[
            pl.BlockSpec((bm, d), lambda i, f: (i, 0)),
            pl.BlockSpec((1, d, bf), lambda i, f: (f, 0, 0)),
            pl.BlockSpec((1, d, bf), lambda i, f: (f, 0, 0)),
            pl.BlockSpec((bf, d), lambda i, f: (f, 0)),
            pl.BlockSpec((1, d), lambda i, f: (0, 0)),
            pl.BlockSpec((1, d), lambda i, f: (0, 0)),
        ],
        out_specs=[pl.BlockSpec((bm, d), lambda i, f: (i, 0)), pl.BlockSpec((bm, d), lambda i, f: (i, 0))],
        out_shape=[SDS((m, d), F32), SDS((m, d), BF16)],
        compiler_params=_cparams(2, vmem=VMEM_LIMIT_MAX),
        name="ffn_ln",
    )(x, wg, wu, wd, g.reshape(1, d), b.reshape(1, d))


def _ffn_cast_kernel(x_ref, wg_ref, wu_ref, wd_ref, g_ref, b_ref, y_ref, yb_ref, wgb_ref, wub_ref, wdb_ref):
    wgb_ref[0] = wg_ref[0].astype(BF16)
    wub_ref[0] = wu_ref[0].astype(BF16)
    wdb_ref[...] = wd_ref[0].astype(BF16)
    _ffn_step(pl.program_id(1), x_ref, wgb_ref.at[0], wub_ref.at[0], wdb_ref, g_ref, b_ref, y_ref, yb_ref)


def ffn_ln_cast(x, wg, wu, wd, layer, g, b, *, bf=256):
    m, d = x.shape
    dff = wg.shape[2]
    fixed = lambda i, f: (0, 0)
    return pl.pallas_call(
        _ffn_cast_kernel,
        grid=(1, dff // bf),
        in_specs=[
            pl.BlockSpec((m, d), fixed),
            pl.BlockSpec((1, d, bf), lambda i, f: (layer, 0, f)),
            pl.BlockSpec((1, d, bf), lambda i, f: (layer, 0, f)),
            pl.BlockSpec((1, bf, d), lambda i, f: (layer, f, 0)),
            pl.BlockSpec((1, d), fixed),
            pl.BlockSpec((1, d), fixed),
        ],
        out_specs=[
            pl.BlockSpec((m, d), fixed),
            pl.BlockSpec((m, d), fixed),
            pl.BlockSpec((1, d, bf), lambda i, f: (f, 0, 0)),
            pl.BlockSpec((1, d, bf), lambda i, f: (f, 0, 0)),
            pl.BlockSpec((bf, d), lambda i, f: (f, 0)),
        ],
        out_shape=[SDS((m, d), F32), SDS((m, d), BF16), SDS((dff // bf, d, bf), BF16),
                   SDS((dff // bf, d, bf), BF16), SDS((dff, d), BF16)],
        compiler_params=_cparams(2),
        name="ffn_ln_cast",
    )(x, wg, wu, wd, g.reshape(1, d), b.reshape(1, d))


def _cmul(ar, ai, br, bi):
    return ar * br - ai * bi, ar * bi + ai * br


def _s5_coefs(a_r, a_i, period):
    l = a_r.shape[1]
    pows = [(a_r, a_i)]
    for _ in range(period - 1):
        pows.append(_cmul(pows[-1][0], pows[-1][1], a_r, a_i))
    t = lax.broadcasted_iota(jnp.int32, (8, l), 0) & (period - 1)
    shifts = []
    k = 1
    while k < period:
        keep = t >= k
        shifts.append((jnp.where(keep, pows[k - 1][0], 0.0), jnp.where(keep, pows[k - 1][1], 0.0)))
        k *= 2
    p_r = jnp.broadcast_to(pows[0][0], (8, l))
    p_i = jnp.broadcast_to(pows[0][1], (8, l))
    for j in range(1, period):
        p_r = jnp.where(t == j, pows[j][0], p_r)
        p_i = jnp.where(t == j, pows[j][1], p_i)
    return shifts, (p_r, p_i)


def _s5_scan_tile(x_r, x_i, shifts):
    k = 1
    for c_r, c_i in shifts:
        s_r = pltpu.roll(x_r, k, 0)
        s_i = pltpu.roll(x_i, k, 0)
        x_r, x_i = x_r + c_r * s_r - c_i * s_i, x_i + c_r * s_i + c_i * s_r
        k *= 2
    return x_r, x_i


def _s5_prompt_kernel(u_ref, bw_ref, cw_ref, ar_ref, ai_ref, d_ref, z_ref, hfin_ref, bu_ref, carry_ref):
    tt = pl.program_id(2)
    n_rows = bu_ref.shape[0]

    @pl.when(tt == 0)
    def _():
        carry_ref[...] = jnp.zeros_like(carry_ref)

    bu_ref[...] = _dot(u_ref[0].astype(BF16), bw_ref[0])
    shifts, (p_r, p_i) = _s5_coefs(ar_ref[0], ai_ref[0], 8)

    ch = S5_SCAN_CHUNK
    d_skip = d_ref[0]

    def scan_chunk(c, carry):
        for i in range(ch // 8):
            h_r, h_i = carry
            r0 = pl.multiple_of(c * ch + i * 8, 8)
            x_r, x_i = _s5_scan_tile(bu_ref[pl.ds(r0, 8), 0:S5_HW], bu_ref[pl.ds(r0, 8), S5_HW:2 * S5_HW], shifts)
            x_r, x_i = x_r + p_r * h_r - p_i * h_i, x_i + p_r * h_i + p_i * h_r
            bu_ref[pl.ds(r0, 8), 0:S5_HW] = x_r
            bu_ref[pl.ds(r0, 8), S5_HW:2 * S5_HW] = x_i
            carry = (x_r[7:8], x_i[7:8])
        return carry

    def emit_chunk(c):
        rows = pl.ds(pl.multiple_of(c * ch, ch), ch)
        y = _dot(bu_ref[rows, :].astype(BF16), cw_ref[0]) + d_skip * u_ref[0, rows, :]
        z_ref[0, rows, :] = jax.nn.gelu(y).astype(z_ref.dtype)

    def body(c, carry):
        emit_chunk(c - 1)
        return scan_chunk(c, carry)

    carry = scan_chunk(0, (carry_ref[:, 0:S5_HW], carry_ref[:, S5_HW:2 * S5_HW]))
    h_r, h_i = lax.fori_loop(1, n_rows // ch, body, carry)
    emit_chunk(n_rows // ch - 1)
    carry_ref[:, 0:S5_HW] = h_r
    carry_ref[:, S5_HW:2 * S5_HW] = h_i

    @pl.when(tt == pl.num_programs(2) - 1)
    def _():
        hfin_ref[0, 0] = carry_ref[...]


def s5_prompt(h_all, bw, cw, a_r, a_i, d_skip, *, tb):
    bsz, t, _ = h_all.shape
    blk = lambda b, g, s: (g, 0, 0)
    return pl.pallas_call(
        _s5_prompt_kernel,
        grid=(bsz, S5_NGB, t // tb),
        in_specs=[
            pl.BlockSpec((1, tb, S5_UW), lambda b, g, s: (b, s, g)),
            pl.BlockSpec((1, S5_UW, 2 * S5_HW), blk),
            pl.BlockSpec((1, 2 * S5_HW, S5_UW), blk),
            pl.BlockSpec((1, 1, S5_HW), blk),
            pl.BlockSpec((1, 1, S5_HW), blk),
            pl.BlockSpec((1, 1, S5_UW), blk),
        ],
        out_specs=[
            pl.BlockSpec((1, tb, S5_UW), lambda b, g, s: (b, s, g)),
            pl.BlockSpec((1, 1, 1, 2 * S5_HW), lambda b, g, s: (b, g, 0, 0)),
        ],
        out_shape=[SDS((bsz, t, D_S5), BF16), SDS((bsz, S5_NGB, 1, 2 * S5_HW), F32)],
        scratch_shapes=[pltpu.VMEM((tb, 2 * S5_HW), F32), pltpu.VMEM((1, 2 * S5_HW), F32)],
        compiler_params=_cparams(3),
        name="s5_prompt",
    )(h_all, bw, cw, a_r, a_i, d_skip)


def _s5_sample_kernel(u_ref, h0_ref, bw_ref, cw_ref, ar_ref, ai_ref, d_ref, z_ref, h_ref, *, period):
    u = u_ref[...]
    bu = _dot(u.astype(BF16), bw_ref[0])
    shifts, (p_r, p_i) = _s5_coefs(ar_ref[0], ai_ref[0], period)
    for i in range(u.shape[0] // 8):
        rows = slice(i * 8, (i + 1) * 8)
        x_r, x_i = _s5_scan_tile(bu[rows, 0:S5_HW], bu[rows, S5_HW:2 * S5_HW], shifts)
        h0_r = h0_ref[rows, 0:S5_HW]
        h0_i = h0_ref[rows, S5_HW:2 * S5_HW]
        h_ref[rows, 0:S5_HW] = x_r + p_r * h0_r - p_i * h0_i
        h_ref[rows, S5_HW:2 * S5_HW] = x_i + p_r * h0_i + p_i * h0_r
    y = _dot(h_ref[...].astype(BF16), cw_ref[0]) + d_ref[0] * u
    z_ref[...] = jax.nn.gelu(y).astype(z_ref.dtype)


def s5_sample(h_all, h0x, bw, cw, a_r, a_i, d_skip, *, period):
    m = h_all.shape[0]
    blk = lambda g: (g, 0, 0)
    return pl.pallas_call(
        functools.partial(_s5_sample_kernel, period=period),
        grid=(S5_NGB,),
        in_specs=[
            pl.BlockSpec((m, S5_UW), lambda g: (0, g)),
            pl.BlockSpec((m, 2 * S5_HW), lambda g: (0, g)),
            pl.BlockSpec((1, S5_UW, 2 * S5_HW), blk),
            pl.BlockSpec((1, 2 * S5_HW, S5_UW), blk),
            pl.BlockSpec((1, 1, S5_HW), blk),
            pl.BlockSpec((1, 1, S5_HW), blk),
            pl.BlockSpec((1, 1, S5_UW), blk),
        ],
        out_specs=[pl.BlockSpec((m, S5_UW), lambda g: (0, g)), pl.BlockSpec((m, 2 * S5_HW), lambda g: (0, g))],
        out_shape=[SDS((m, D_S5), BF16), SDS((m, S5_NGB * 2 * S5_HW), F32)],
        compiler_params=_cparams(1),
        name="s5_sample",
    )(h_all, h0x, bw, cw, a_r, a_i, d_skip)


def _bias_kernel(rb_ref, o_ref):
    h = pl.program_id(0)
    q = lax.broadcasted_iota(jnp.int32, (WINDOW, 2 * WINDOW), 0)
    k = lax.broadcasted_iota(jnp.int32, (WINDOW, 2 * WINDOW), 1)
    n = jnp.maximum(q + WINDOW - k, 0)
    max_exact = N_BUCKETS // 2
    nf = jnp.maximum(n, 1).astype(F32)
    large = max_exact + (jnp.log(nf / max_exact) / math.log(MAX_DISTANCE / max_exact)
                         * (N_BUCKETS - max_exact)).astype(jnp.int32)
    large = jnp.minimum(large, N_BUCKETS - 1)
    bucket = jnp.where(n < max_exact, n, large)
    out = jnp.zeros((WINDOW, 2 * WINDOW), F32)
    for b in range(N_BUCKETS):
        out = jnp.where(bucket == b, rb_ref[b, h], out)
    o_ref[0] = out


def bias_table(rel_bias):
    return pl.pallas_call(
        _bias_kernel,
        grid=(SWA_HEADS,),
        in_specs=[pl.BlockSpec(memory_space=pltpu.SMEM)],
        out_specs=pl.BlockSpec((1, WINDOW, 2 * WINDOW), lambda h: (h, 0, 0)),
        out_shape=SDS((SWA_HEADS, WINDOW, 2 * WINDOW), F32),
        compiler_params=_cparams(1),
        name="bias_table",
    )(rel_bias)


def _swa_softmax(s, valid, sink_col):
    s = jnp.where(valid, s, NEG_BIG)
    m = jnp.maximum(jnp.max(s, axis=-1, keepdims=True), sink_col)
    p = jnp.exp(s - m)
    return p / (jnp.sum(p, axis=-1, keepdims=True) + jnp.exp(sink_col - m))


def _sink_col(sinks_ref, first_head, rows_per_head):
    n = SWA_GQ * rows_per_head
    r = lax.broadcasted_iota(jnp.int32, (n, 1), 0)
    col = jnp.full((n, 1), sinks_ref[first_head], F32)
    for g in range(1, SWA_GQ):
        col = jnp.where(r >= g * rows_per_head, sinks_ref[first_head + g], col)
    return col


SWA_KV_PER_STEP = 4


def _swa_prompt_kernel(sinks_ref, q_ref, kp_ref, kc_ref, vp_ref, vc_ref, bias_ref, o_ref):
    nkv = SWA_KV_PER_STEP
    part = pl.program_id(0)
    i = pl.program_id(2)
    blk = WINDOW
    q = q_ref[0].astype(BF16)
    kk = jnp.concatenate([kp_ref[0], kc_ref[0]], axis=0).astype(BF16)
    vv = jnp.concatenate([vp_ref[0], vc_ref[0]], axis=0).astype(BF16)
    qi = lax.broadcasted_iota(jnp.int32, (SWA_GQ * blk, 2 * blk), 0) & (blk - 1)
    ki = lax.broadcasted_iota(jnp.int32, (SWA_GQ * blk, 2 * blk), 1)
    valid = (ki > qi) & (ki <= qi + blk) & ((ki >= blk) | (i > 0))
    outs = []
    for j in range(nkv):
        k_j = kk[:, j * HEAD_DIM:(j + 1) * HEAD_DIM]
        v_j = vv[:, j * HEAD_DIM:(j + 1) * HEAD_DIM]
        q4 = jnp.concatenate([q[:, (j * SWA_GQ + g) * HEAD_DIM:(j * SWA_GQ + g + 1) * HEAD_DIM]
                              for g in range(SWA_GQ)], axis=0)
        s = _dot_nt(q4, k_j) * (HEAD_DIM ** -0.5)
        s = s + bias_ref[j * SWA_GQ:(j + 1) * SWA_GQ].reshape(SWA_GQ * blk, 2 * blk)
        p = _swa_softmax(s, valid, _sink_col(sinks_ref, (part * nkv + j) * SWA_GQ, blk))
        o4 = _dot(p.astype(BF16), v_j)
        outs += [o4[g * blk:(g + 1) * blk] for g in range(SWA_GQ)]
    o_ref[0] = jnp.concatenate(outs, axis=1).astype(o_ref.dtype)


def swa_prompt(h_all, sinks, table):
    bsz, t, _ = h_all.shape
    nblk = t // WINDOW
    nkv = SWA_KV_PER_STEP
    qw = nkv * SWA_GQ * HEAD_DIM
    kw = nkv * HEAD_DIM
    q0, k0, v0 = D_S5 // qw, (D_S5 + D_SWA_Q) // kw, (D_S5 + D_SWA_Q + D_SWA_KV) // kw
    cur = lambda c0: (lambda p, b, i: (b, i, c0 + p))
    prev = lambda c0: (lambda p, b, i: (b, jnp.maximum(i - 1, 0), c0 + p))
    return pl.pallas_call(
        _swa_prompt_kernel,
        grid=(SWA_KV_HEADS // nkv, bsz, nblk),
        in_specs=[
            pl.BlockSpec(memory_space=pltpu.SMEM),
            pl.BlockSpec((1, WINDOW, qw), cur(q0)),
            pl.BlockSpec((1, WINDOW, kw), prev(k0)),
            pl.BlockSpec((1, WINDOW, kw), cur(k0)),
            pl.BlockSpec((1, WINDOW, kw), prev(v0)),
            pl.BlockSpec((1, WINDOW, kw), cur(v0)),
            pl.BlockSpec((nkv * SWA_GQ, WINDOW, 2 * WINDOW), lambda p, b, i: (p, 0, 0)),
        ],
        out_specs=pl.BlockSpec((1, WINDOW, qw), lambda p, b, i: (b, i, p)),
        out_shape=SDS((bsz, t, D_SWA_Q), BF16),
        compiler_params=_cparams(3),
        name="swa_prompt",
    )(sinks, h_all, h_all, h_all, h_all, h_all, table)


def _swa_sample_kernel(sinks_ref, q_ref, kk_ref, vv_ref, bias_ref, o_ref, *, t_new):
    rows = q_ref.shape[0]
    n_keys = kk_ref.shape[1]
    q = q_ref[...].astype(BF16)
    r = lax.broadcasted_iota(jnp.int32, (SWA_GQ * rows, n_keys), 0)
    ti = r & (t_new - 1)
    ki = lax.broadcasted_iota(jnp.int32, (SWA_GQ * rows, n_keys), 1)
    valid = (ki > ti) & (ki <= ti + WINDOW)
    second = (r & (rows - 1)) >= t_new
    second_o = (lax.broadcasted_iota(jnp.int32, (SWA_GQ * rows, HEAD_DIM), 0) & (rows - 1)) >= t_new
    outs = []
    for j in range(SWA_KV_HEADS):
        cols = slice(j * HEAD_DIM, (j + 1) * HEAD_DIM)
        q4 = jnp.concatenate([q[:, (j * SWA_GQ + g) * HEAD_DIM:(j * SWA_GQ + g + 1) * HEAD_DIM]
                              for g in range(SWA_GQ)], axis=0)
        s = jnp.where(second, _dot_nt(q4, kk_ref[1][:, cols].astype(BF16)),
                      _dot_nt(q4, kk_ref[0][:, cols].astype(BF16))) * (HEAD_DIM ** -0.5)
        s = s + bias_ref[j * SWA_GQ:(j + 1) * SWA_GQ].reshape(SWA_GQ * rows, n_keys)
        p = _swa_softmax(s, valid, _sink_col(sinks_ref, j * SWA_GQ, rows)).astype(BF16)
        o4 = jnp.where(second_o, _dot(p, vv_ref[1][:, cols].astype(BF16)), _dot(p, vv_ref[0][:, cols].astype(BF16)))
        outs += [o4[g * rows:(g + 1) * rows] for g in range(SWA_GQ)]
    o_ref[...] = jnp.concatenate(outs, axis=1).astype(o_ref.dtype)


def swa_sample(h_all, kk, vv, sinks, bias_s, *, t_new):
    m = h_all.shape[0]
    n_keys = kk.shape[1]
    rows = 2 * t_new
    return pl.pallas_call(
        functools.partial(_swa_sample_kernel, t_new=t_new),
        grid=(m // rows,),
        in_specs=[
            pl.BlockSpec(memory_space=pltpu.SMEM),
            pl.BlockSpec((rows, D_SWA_Q), lambda i: (i, D_S5 // D_SWA_Q)),
            pl.BlockSpec((2, n_keys, D_SWA_KV), lambda i: (i, 0, 0)),
            pl.BlockSpec((2, n_keys, D_SWA_KV), lambda i: (i, 0, 0)),
            pl.BlockSpec((SWA_HEADS, rows, n_keys), lambda i: (0, 0, 0)),
        ],
        out_specs=pl.BlockSpec((rows, D_SWA_Q), lambda i: (i, 0)),
        out_shape=SDS((m, D_SWA_Q), BF16),
        compiler_params=_cparams(1),
        name="swa_sample",
    )(sinks, h_all, kk, vv, bias_s)


def _hg_gates(q, fz, lb):
    qs = q * jax.nn.sigmoid(q)
    f = lb + (1.0 - lb) * jax.nn.sigmoid(fz)
    return qs, jnp.log(f), 1.0 - f


HG_GROUP = 4


def _hgrn_prompt_kernel(q_ref, f_ref, v_ref, lb_ref, o_ref, st_ref, s_scr, *, n_heads):
    tt = pl.program_id(2)
    n_rows = q_ref.shape[1]
    c, ng = HG_CHUNK, HG_GROUP
    gr = c * ng

    @pl.when(tt == 0)
    def _():
        s_scr[...] = jnp.zeros_like(s_scr)

    ri = lax.broadcasted_iota(jnp.int32, (gr, gr), 0)
    ci = lax.broadcasted_iota(jnp.int32, (gr, gr), 1)
    causal = (ri >= ci) & ((ri // c) == (ci // c))
    t_in = lax.broadcasted_iota(jnp.int32, (gr, HG_DK), 0) & (c - 1)
    own = ((lax.broadcasted_iota(jnp.int32, (gr, ng * HG_DK), 0) // c)
           == (lax.broadcasted_iota(jnp.int32, (gr, ng * HG_DK), 1) // HG_DK)).astype(BF16)

    def body(gi, _):
        r0 = pl.multiple_of(gi * gr, gr)
        for hd in range(n_heads):
            cols = slice(hd * HG_DK, (hd + 1) * HG_DK)
            qs, lf, k = _hg_gates(q_ref[0, pl.ds(r0, gr), cols], f_ref[0, pl.ds(r0, gr), cols], lb_ref[:, cols])
            v = v_ref[0, pl.ds(r0, gr), cols].astype(BF16)
            cum = lf
            sh = 1
            while sh < c:
                cum = cum + jnp.where(t_in >= sh, pltpu.roll(cum, sh, 0), 0.0)
                sh *= 2
            last3 = cum.reshape(ng, c, HG_DK)[:, c - 1:c, :]
            last = jnp.broadcast_to(last3, (ng, c, HG_DK)).reshape(gr, HG_DK)
            qt = (qs * jnp.exp(cum)).astype(BF16)
            kt = (k * jnp.exp(-cum)).astype(BF16)
            kl = (k * jnp.exp(last - cum)).astype(BF16)
            decay = jnp.exp(last3)
            attn = jnp.where(causal, _dot_nt(qt, kt), 0.0).astype(BF16)
            kv = _dot_tn(v, jnp.concatenate([kl] * ng, axis=1) * own)
            st = s_scr[hd]
            starts = []
            for j in range(ng):
                starts.append(st)
                st = st * decay[j] + kv[:, j * HG_DK:(j + 1) * HG_DK]
            s_scr[hd] = st
            s_cat = jnp.concatenate(starts, axis=1).astype(BF16)
            o_ref[0, pl.ds(r0, gr), cols] = (_dot(attn, v)
                                            + _dot_nt(jnp.concatenate([qt] * ng, axis=1) * own, s_cat))
        return 0

    lax.fori_loop(0, n_rows // gr, body, 0, unroll=True)

    @pl.when(tt == pl.num_programs(2) - 1)
    def _():
        st_ref[0] = s_scr[...]


def hgrn_prompt(h_all, lb, *, tb, n_heads=2):
    bsz, t, _ = h_all.shape
    hw = n_heads * HG_DK
    nhb = D_MODEL // hw
    col = lambda c0: (lambda b, h, s: (b, s, c0 * nhb + h))
    return pl.pallas_call(
        functools.partial(_hgrn_prompt_kernel, n_heads=n_heads),
        grid=(bsz, nhb, t // tb),
        in_specs=[
            pl.BlockSpec((1, tb, hw), col(0)),
            pl.BlockSpec((1, tb, hw), col(1)),
            pl.BlockSpec((1, tb, hw), col(2)),
            pl.BlockSpec((1, hw), lambda b, h, s: (0, h)),
        ],
        out_specs=[
            pl.BlockSpec((1, tb, hw), lambda b, h, s: (b, s, h)),
            pl.BlockSpec((1, n_heads, HG_DK, HG_DK), lambda b, h, s: (b, h, 0, 0)),
        ],
        out_shape=[SDS((bsz, t, D_MODEL), F32), SDS((bsz, HG_HEADS, HG_DK, HG_DK), F32)],
        scratch_shapes=[pltpu.VMEM((n_heads, HG_DK, HG_DK), F32)],
        compiler_params=_cparams(3),
        name="hgrn_prompt",
    )(h_all, h_all, h_all, lb.reshape(1, D_MODEL))


def _hgrn_sample_kernel(q_ref, f_ref, v_ref, lb_ref, s0_ref, o_ref, s_ref, *, n_heads, t_new):
    rows = q_ref.shape[0]
    ri = lax.broadcasted_iota(jnp.int32, (rows, rows), 0)
    ci = lax.broadcasted_iota(jnp.int32, (rows, rows), 1)
    causal = (ri >= ci) & ((ri >= t_new) == (ci >= t_new))
    tril = causal.astype(F32)
    second = lax.broadcasted_iota(jnp.int32, (rows, HG_DK), 0) >= t_new
    eye = (lax.broadcasted_iota(jnp.int32, (HG_DK, HG_DK), 0)
           == lax.broadcasted_iota(jnp.int32, (HG_DK, HG_DK), 1))
    for hd in range(n_heads):
        cols = slice(hd * HG_DK, (hd + 1) * HG_DK)
        qs, lf, k = _hg_gates(q_ref[:, cols], f_ref[:, cols], lb_ref[:, cols])
        v = v_ref[:, cols].astype(BF16)
        cum = jnp.dot(tril, lf, preferred_element_type=F32, precision=lax.Precision.HIGHEST)
        last0 = cum[t_new - 1:t_new]
        last1 = cum[rows - 1:rows]
        last = jnp.where(second, last1, last0)
        qt = (qs * jnp.exp(cum)).astype(BF16)
        kt = (k * jnp.exp(-cum)).astype(BF16)
        kl = k * jnp.exp(last - cum)
        attn = jnp.where(causal, _dot_nt(qt, kt), 0.0).astype(BF16)
        s0a = s0_ref[0, hd]
        s0b = s0_ref[1, hd]
        o_ref[:, cols] = _dot(attn, v) + jnp.where(second, _dot(qt, s0b.astype(BF16)), _dot(qt, s0a.astype(BF16)))
        for bb, (s0, lst) in enumerate(((s0a, last0), (s0b, last1))):
            decay = jnp.sum(jnp.where(eye, jnp.exp(lst), 0.0), axis=1, keepdims=True)
            kl_b = jnp.where(second == (bb == 1), kl, 0.0).astype(BF16)
            s_ref[bb, hd] = decay * s0 + _dot_tn(kl_b, v)


def hgrn_sample(h_all, lb, s0, *, t_new, n_heads=8):
    m = h_all.shape[0]
    rows = 2 * t_new
    hw = n_heads * HG_DK
    nhb = D_MODEL // hw
    col = lambda c0: (lambda i, h: (i, c0 * nhb + h))
    return pl.pallas_call(
        functools.partial(_hgrn_sample_kernel, n_heads=n_heads, t_new=t_new),
        grid=(m // rows, nhb),
        in_specs=[
            pl.BlockSpec((rows, hw), col(0)),
            pl.BlockSpec((rows, hw), col(1)),
            pl.BlockSpec((rows, hw), col(2)),
            pl.BlockSpec((1, hw), lambda i, h: (0, h)),
            pl.BlockSpec((2, n_heads, HG_DK, HG_DK), lambda i, h: (i, h, 0, 0)),
        ],
        out_specs=[
            pl.BlockSpec((rows, hw), lambda i, h: (i, h)),
            pl.BlockSpec((2, n_heads, HG_DK, HG_DK), lambda i, h: (i, h, 0, 0)),
        ],
        out_shape=[SDS((m, D_MODEL), F32), SDS(s0.shape, F32)],
        compiler_params=_cparams(2),
        name="hgrn_sample",
    )(h_all, h_all, h_all, lb.reshape(1, D_MODEL), s0)


def _xattn_kernel(y_ref, yb_ref, wq_ref, mk_ref, mv_ref, wo_ref, g_ref, b_ref, o_ref, ob_ref, *, n_seq):
    rows = y_ref.shape[0]
    q = _dot(yb_ref[...], wq_ref[...]).astype(BF16)
    second = lax.broadcasted_iota(jnp.int32, (rows, 1), 0) >= rows // 2
    heads = []
    for h in range(MEM_HEADS):
        cols = slice(h * MEM_HEAD_DIM, (h + 1) * MEM_HEAD_DIM)
        qh = q[:, cols]
        s = _dot_nt(qh, mk_ref[0][:, cols].astype(BF16))
        if n_seq == 2:
            s = jnp.where(second, _dot_nt(qh, mk_ref[1][:, cols].astype(BF16)), s)
        s = s * (MEM_HEAD_DIM ** -0.5)
        e = jnp.exp(s - jnp.max(s, axis=-1, keepdims=True))
        p = (e / jnp.sum(e, axis=-1, keepdims=True)).astype(BF16)
        o = _dot(p, mv_ref[0][:, cols].astype(BF16))
        if n_seq == 2:
            o = jnp.where(second, _dot(p, mv_ref[1][:, cols].astype(BF16)), o)
        heads.append(o.astype(BF16))
    _proj_res_ln_chunked([jnp.concatenate(heads, axis=1)], wo_ref, y_ref, g_ref, b_ref, o_ref, ob_ref)


def xattn_ln(y, yb, wq, mk, mv, wo, g, b, *, bm, rows_per_seq):
    m, d = y.shape
    n_seq = 2 if bm == 2 * rows_per_seq else 1
    assert n_seq == 2 or rows_per_seq % bm == 0
    row = lambda i: (i, 0)
    fixed = lambda i: (0, 0)
    mem = (lambda i: (i, 0, 0)) if n_seq == 2 else (lambda i: (i * bm // rows_per_seq, 0, 0))
    return pl.pallas_call(
        functools.partial(_xattn_kernel, n_seq=n_seq),
        grid=(m // bm,),
        in_specs=[
            pl.BlockSpec((bm, d), row),
            pl.BlockSpec((bm, d), row),
            pl.BlockSpec((d, D_MEM), fixed),
            pl.BlockSpec((n_seq, N_MEM, D_MEM), mem),
            pl.BlockSpec((n_seq, N_MEM, D_MEM), mem),
            pl.BlockSpec((D_MEM, d), fixed),
            pl.BlockSpec((1, d), fixed),
            pl.BlockSpec((1, d), fixed),
        ],
        out_specs=[pl.BlockSpec((bm, d), row), pl.BlockSpec((bm, d), row)],
        out_shape=[SDS((m, d), F32), SDS((m, d), BF16)],
        compiler_params=_cparams(1),
        name="xattn_ln",
    )(y, yb, wq, mk, mv, wo, g.reshape(1, d), b.reshape(1, d))


def _s5_discretize(lam_re, lam_im, log_dt, b_re, b_im):
    lr = jnp.minimum(lam_re.astype(F32), -1e-4)
    li = lam_im.astype(F32)
    dt = jnp.exp(log_dt.astype(F32))[:, None]
    mag = jnp.exp(lr * dt)
    a_re = mag * jnp.cos(li * dt)
    a_im = mag * jnp.sin(li * dt)
    den = lr * lr + li * li
    fr = ((a_re - 1.0) * lr + a_im * li) / den
    fi = (a_im * lr - (a_re - 1.0) * li) / den
    br, bi = b_re.astype(F32), b_im.astype(F32)
    bb_re = fr[..., None] * br - fi[..., None] * bi
    bb_im = fr[..., None] * bi + fi[..., None] * br
    return a_re, a_im, bb_re, bb_im


def _s5_block_weights(a_re, a_im, bb_re, bb_im, c_re, c_im, d_skip):
    eye = jnp.eye(S5_GB, dtype=F32)
    shp_b = (S5_NGB, S5_GB, S5_STATE, S5_GROUP)
    shp_c = (S5_NGB, S5_GB, S5_GROUP, S5_STATE)
    blk_b = lambda w: jnp.einsum('bgph,gk->bghkp', w.reshape(shp_b), eye).reshape(S5_NGB, S5_UW, S5_HW)
    blk_c = lambda w: jnp.einsum('bghp,gk->bkpgh', w.reshape(shp_c), eye).reshape(S5_NGB, S5_HW, S5_UW)
    bw = jnp.concatenate([blk_b(bb_re), blk_b(bb_im)], axis=-1).astype(BF16)
    cw = jnp.concatenate([blk_c(c_re.astype(F32)), blk_c(-c_im.astype(F32))], axis=1).astype(BF16)
    return (bw, cw, a_re.reshape(S5_NGB, 1, S5_HW), a_im.reshape(S5_NGB, 1, S5_HW),
            d_skip.astype(F32).reshape(S5_NGB, 1, S5_UW))


def _s5_state_to_blocks(s_re, s_im):
    b = s_re.shape[0]
    return jnp.concatenate([s_re.reshape(b, S5_NGB, S5_HW), s_im.reshape(b, S5_NGB, S5_HW)],
                           axis=-1).reshape(b, S5_NGB * 2 * S5_HW)


def _s5_state_from_blocks(h):
    b = h.shape[0]
    return (h[:, :, :S5_HW].reshape(b, S5_GROUPS, S5_STATE), h[:, :, S5_HW:].reshape(b, S5_GROUPS, S5_STATE))


def kernel(x_prompt, x_sample, cache_mem_k, cache_mem_v, cache_swa_k, cache_swa_v, state_s5_re, state_s5_im, state_hgrn, mem_prompt, rel_bias, w_even_in, s5_lam_re, s5_lam_im, s5_log_dt, s5_b_re, s5_b_im, s5_c_re, s5_c_im, s5_d, s5_w_glu, swa_sinks, w_even_out, hg_lb_logits, w_odd_in, hg_norm_g, w_odd_out, w_mem_q, w_mem_k, w_mem_v, w_mem_o, w_ffn_gate, w_ffn_up, w_ffn_down, ln_g, ln_b):
    bsz, seq, d = x_prompt.shape
    dec_b, dec_t, _ = x_sample.shape
    mp, ms = bsz * seq, dec_b * dec_t
    w_buf = cache_swa_k.shape[2]
    bf = lambda w: w.astype(BF16)

    lb_soft = jax.nn.softmax(hg_lb_logits.astype(F32), axis=0)
    lower_bounds = jnp.cumsum(lb_soft, axis=0) - lb_soft[0]
    table = bias_table(rel_bias.astype(F32))
    bias_s = jnp.tile(table[:, :dec_t], (1, 2, 1))
    mem_b = bf(mem_prompt).reshape(bsz * N_MEM, d)

    yp = x_prompt.reshape(mp, d)
    ys = x_sample.reshape(ms, d)
    ypb, ysb = bf(yp), bf(ys)
    p_mem_k, p_mem_v = [], []
    p_swa_k, p_swa_v, p_s5_re, p_s5_im, p_hg = [], [], [], [], []
    s_swa_k, s_swa_v, s_s5_re, s_s5_im, s_hg = [], [], [], [], []

    for l in range(DEPTH):
        j = l // 2
        g0, b0 = ln_g[l, 0], ln_b[l, 0]
        if l % 2 == 0:
            a_re, a_im, bb_re, bb_im = _s5_discretize(s5_lam_re[j], s5_lam_im[j], s5_log_dt[j], s5_b_re[j], s5_b_im[j])
            s5w = _s5_block_weights(a_re, a_im, bb_re, bb_im, s5_c_re[j], s5_c_im[j], s5_d[j])
            w_glu, w_out = cast_layer(s5_w_glu, j), cast_layer(w_even_out, j)
            sinks = swa_sinks[j].astype(F32)
            kv0 = D_S5 + D_SWA_Q

            hs, w_in = matmul_cast(ysb, w_even_in, j, bn=512)
            hp = matmul(ypb, w_in, bm=1024, bn=1024).reshape(bsz, seq, -1)
            z, hfin = s5_prompt(hp, *s5w, tb=512)
            s5_out = glu(z.reshape(mp, D_S5), w_glu, bm=1024, bn=1024)
            att = swa_prompt(hp, sinks, table).reshape(mp, D_SWA_Q)
            hr, hi = _s5_state_from_blocks(hfin[:, :, 0])
            p_s5_re.append(hr); p_s5_im.append(hi)
            p_swa_k.append(hp[:, seq - w_buf:, kv0:kv0 + D_SWA_KV].reshape(bsz, w_buf, SWA_KV_HEADS, HEAD_DIM))
            p_swa_v.append(hp[:, seq - w_buf:, kv0 + D_SWA_KV:].reshape(bsz, w_buf, SWA_KV_HEADS, HEAD_DIM))
            yp, ypb = proj_res_ln([s5_out, att], w_out, yp, g0, b0, bm=128)

            h0x = jnp.repeat(_s5_state_to_blocks(state_s5_re[j].astype(F32), state_s5_im[j].astype(F32)), dec_t, axis=0)
            z, h_steps = s5_sample(hs, h0x, *s5w, period=dec_t)
            s5_out = glu(z, w_glu, bm=ms, bn=1024)
            hs3 = hs.reshape(dec_b, dec_t, -1)
            kpad = jnp.zeros((dec_b, 2 * WINDOW - w_buf - dec_t, D_SWA_KV), F32)
            kk = jnp.concatenate([cache_swa_k[j].reshape(dec_b, w_buf, D_SWA_KV).astype(F32),
                                  hs3[:, :, kv0:kv0 + D_SWA_KV], kpad], axis=1)
            vv = jnp.concatenate([cache_swa_v[j].reshape(dec_b, w_buf, D_SWA_KV).astype(F32),
                                  hs3[:, :, kv0 + D_SWA_KV:], kpad], axis=1)
            att = swa_sample(hs, kk, vv, sinks, bias_s, t_new=dec_t)
            hr, hi = _s5_state_from_blocks(h_steps.reshape(dec_b, dec_t, S5_NGB, 2 * S5_HW)[:, dec_t - 1])
            s_s5_re.append(hr); s_s5_im.append(hi)
            s_swa_k.append(kk[:, dec_t:dec_t + w_buf].reshape(dec_b, w_buf, SWA_KV_HEADS, HEAD_DIM))
            s_swa_v.append(vv[:, dec_t:dec_t + w_buf].reshape(dec_b, w_buf, SWA_KV_HEADS, HEAD_DIM))
            ys, ysb = proj_res_ln([s5_out, att], w_out, ys, g0, b0, bm=ms)
        else:
            w_out = cast_layer(w_odd_out, j)
            lb = lower_bounds[l]

            hs, w_in = matmul_cast(ysb, w_odd_in, j, bn=512)
            hp = matmul(ypb, w_in, bm=1024, bn=1024)
            o, st = hgrn_prompt(hp.reshape(bsz, seq, -1), lb, tb=512, n_heads=4)
            p_hg.append(jnp.swapaxes(st, -1, -2))
            yp, ypb = hg_out_ln(o.reshape(mp, d), hp, hg_norm_g[j], w_out, yp, g0, b0, bm=128)

            o, s_new = hgrn_sample(hs, lb, state_hgrn[j].astype(F32), t_new=dec_t)
            s_hg.append(s_new)
            ys, ysb = hg_out_ln(o, hs, hg_norm_g[j], w_out, ys, g0, b0, bm=ms)

        wq, wk, wv, wo = (cast_layer(w, l) for w in (w_mem_q, w_mem_k, w_mem_v, w_mem_o))
        mk = matmul(mem_b, wk, bm=bsz * N_MEM, bn=D_MEM).reshape(bsz, N_MEM, D_MEM)
        mv = matmul(mem_b, wv, bm=bsz * N_MEM, bn=D_MEM).reshape(bsz, N_MEM, D_MEM)
        p_mem_k.append(mk.reshape(bsz, N_MEM, MEM_HEADS, MEM_HEAD_DIM))
        p_mem_v.append(mv.reshape(bsz, N_MEM, MEM_HEADS, MEM_HEAD_DIM))
        yp, ypb = xattn_ln(yp, ypb, wq, mk, mv, wo, ln_g[l, 1], ln_b[l, 1], bm=256, rows_per_seq=seq)
        ys, ysb = xattn_ln(ys, ysb, wq, cache_mem_k[l].reshape(dec_b, N_MEM, D_MEM).astype(F32),
                           cache_mem_v[l].reshape(dec_b, N_MEM, D_MEM).astype(F32), wo,
                           ln_g[l, 1], ln_b[l, 1], bm=2 * dec_t, rows_per_seq=dec_t)

        ys, ysb, wg, wu, wd = ffn_ln_cast(ys, w_ffn_gate, w_ffn_up, w_ffn_down, l, ln_g[l, 2], ln_b[l, 2])
        yp, ypb = ffn_ln(yp, wg, wu, wd, ln_g[l, 2], ln_b[l, 2], bm=512)

    return (yp.reshape(bsz, seq, d), ys.reshape(dec_b, dec_t, d),
            jnp.stack(p_mem_k), jnp.stack(p_mem_v),
            jnp.stack(p_swa_k), jnp.stack(p_swa_v),
            jnp.stack(p_s5_re), jnp.stack(p_s5_im), jnp.stack(p_hg),
            jnp.stack(s_swa_k), jnp.stack(s_swa_v),
            jnp.stack(s_s5_re), jnp.stack(s_s5_im), jnp.stack(s_hg))
```

```python
import functools
import math

import jax
import jax.numpy as jnp
from jax import lax
from jax.experimental import pallas as pl
from jax.experimental.pallas import tpu as pltpu

F32 = jnp.float32
BF16 = jnp.bfloat16
SDS = jax.ShapeDtypeStruct

D_MODEL = 4096
DEPTH = 2
ALPHA = (2 * DEPTH) ** 0.25
LN_EPS = 1e-5
RMS_EPS = 1e-6
NEG_BIG = -1e30

S5_GROUPS, S5_GROUP, S5_STATE = 128, 16, 64
D_S5 = S5_GROUPS * S5_GROUP
S5_GB = 16
S5_NGB = S5_GROUPS // S5_GB
S5_UW = S5_GB * S5_GROUP
S5_HW = S5_GB * S5_STATE
S5_SCAN_CHUNK = 128

HEAD_DIM, SWA_HEADS, SWA_KV_HEADS, SWA_GQ = 64, 32, 8, 4
D_SWA_Q, D_SWA_KV = 2048, 512
WINDOW = 128
N_BUCKETS, MAX_DISTANCE = 32, 128

HG_DK, HG_HEADS, HG_CHUNK = 128, 32, 32
N_MEM, MEM_HEADS, MEM_HEAD_DIM, D_MEM = 256, 4, 128, 512

VMEM_LIMIT = 56 * 1024 * 1024
VMEM_LIMIT_MAX = 60 * 1024 * 1024
PROJ_COL_CHUNKS = 4
LN_ROW_CHUNK = 64


def _cparams(n_axes, vmem=VMEM_LIMIT):
    return pltpu.CompilerParams(dimension_semantics=("arbitrary",) * n_axes, vmem_limit_bytes=vmem)


def _dot(a, b):
    return jnp.dot(a, b, preferred_element_type=F32)


def _dot_nt(a, b):
    return lax.dot_general(a, b, (((1,), (1,)), ((), ())), preferred_element_type=F32)


def _dot_tn(a, b):
    return lax.dot_general(a, b, (((0,), (0,)), ((), ())), preferred_element_type=F32)


def _layer_norm_rows(s, g, b):
    mu = jnp.mean(s, axis=-1, keepdims=True)
    c = s - mu
    var = jnp.mean(c * c, axis=-1, keepdims=True)
    return c * lax.rsqrt(var + LN_EPS) * g + b


def _proj_res_sum(xs, w_ref, res_ref, s_ref):
    d = s_ref.shape[1]
    cw = d // PROJ_COL_CHUNKS
    total = None
    for j in range(PROJ_COL_CHUNKS):
        cols = slice(j * cw, (j + 1) * cw)
        s = ALPHA * res_ref[:, cols]
        off = 0
        for x in xs:
            s = s + _dot(x, w_ref[off:off + x.shape[1], cols])
            off += x.shape[1]
        s_ref[:, cols] = s
        part = jnp.sum(s, axis=-1, keepdims=True)
        total = part if total is None else total + part
    return total


def _ln_rows_from(s_ref, total, g_ref, b_ref, y_ref, yb_ref):
    d = s_ref.shape[1]
    cw = d // PROJ_COL_CHUNKS
    mu = total * (1.0 / d)
    sq = None
    for j in range(PROJ_COL_CHUNKS):
        c = s_ref[:, j * cw:(j + 1) * cw] - mu
        part = jnp.sum(c * c, axis=-1, keepdims=True)
        sq = part if sq is None else sq + part
    rstd = lax.rsqrt(sq * (1.0 / d) + LN_EPS)
    for j in range(PROJ_COL_CHUNKS):
        cols = slice(j * cw, (j + 1) * cw)
        y = (s_ref[:, cols] - mu) * rstd * g_ref[:, cols] + b_ref[:, cols]
        y_ref[:, cols] = y
        yb_ref[:, cols] = y.astype(BF16)


def _proj_res_ln(xs, w_ref, res_ref, g_ref, b_ref, y_ref, yb_ref):
    total = _proj_res_sum(xs, w_ref, res_ref, y_ref)
    _ln_rows_from(y_ref, total, g_ref, b_ref, y_ref, yb_ref)


def _cast_kernel(w_ref, o_ref):
    o_ref[...] = w_ref[0].astype(o_ref.dtype)


CAST_BLOCK_BYTES = 8 * 1024 * 1024


def cast_layer(w, layer):
    _, k, n = w.shape
    rows = next(r for r in (4096, 2048, 1024, 512, 256, 128) if k % r == 0 and r * n * 4 <= CAST_BLOCK_BYTES)
    return pl.pallas_call(
        _cast_kernel,
        grid=(k // rows,),
        in_specs=[pl.BlockSpec((1, rows, n), lambda i: (layer, i, 0))],
        out_specs=pl.BlockSpec((rows, n), lambda i: (i, 0)),
        out_shape=SDS((k, n), BF16),
        compiler_params=_cparams(1),
        name="cast_layer",
    )(w)


def _mm_kernel(x_ref, w_ref, o_ref):
    o_ref[...] = _dot(x_ref[...], w_ref[...]).astype(o_ref.dtype)


def matmul(x, w, *, bm, bn, out_dtype=F32):
    m, k = x.shape
    n = w.shape[1]
    return pl.pallas_call(
        _mm_kernel,
        grid=(m // bm, n // bn),
        in_specs=[pl.BlockSpec((bm, k), lambda i, j: (i, 0)), pl.BlockSpec((k, bn), lambda i, j: (0, j))],
        out_specs=pl.BlockSpec((bm, bn), lambda i, j: (i, j)),
        out_shape=SDS((m, n), out_dtype),
        compiler_params=_cparams(2),
        name="matmul",
    )(x, w)


def _mm_cast_kernel(x_ref, w_ref, o_ref, wb_ref):
    wb_ref[...] = w_ref[0].astype(BF16)
    o_ref[...] = _dot(x_ref[...], wb_ref[...]).astype(o_ref.dtype)


def matmul_cast(x, w, layer, *, bn, out_dtype=F32):
    m, k = x.shape
    n = w.shape[2]
    return pl.pallas_call(
        _mm_cast_kernel,
        grid=(n // bn,),
        in_specs=[pl.BlockSpec((m, k), lambda j: (0, 0)), pl.BlockSpec((1, k, bn), lambda j: (layer, 0, j))],
        out_specs=[pl.BlockSpec((m, bn), lambda j: (0, j)), pl.BlockSpec((k, bn), lambda j: (0, j))],
        out_shape=[SDS((m, n), out_dtype), SDS((k, n), BF16)],
        compiler_params=_cparams(1),
        name="matmul_cast",
    )(x, w)


def _glu_kernel(z_ref, zt_ref, w_ref, o_ref):
    a = _dot(z_ref[...], w_ref[...])
    o_ref[...] = (zt_ref[...].astype(F32) * jax.nn.sigmoid(a)).astype(o_ref.dtype)


def glu(z, w, *, bm, bn):
    m, k = z.shape
    n = w.shape[1]
    return pl.pallas_call(
        _glu_kernel,
        grid=(m // bm, n // bn),
        in_specs=[pl.BlockSpec((bm, k), lambda i, j: (i, 0)), pl.BlockSpec((bm, bn), lambda i, j: (i, j)),
                  pl.BlockSpec((k, bn), lambda i, j: (0, j))],
        out_specs=pl.BlockSpec((bm, bn), lambda i, j: (i, j)),
        out_shape=SDS((m, n), BF16),
        compiler_params=_cparams(2),
        name="glu",
    )(z, z, w)


def _proj_ln_kernel(*refs, n_in):
    xs = refs[:n_in]
    w_ref, res_ref, g_ref, b_ref, y_ref, yb_ref = refs[n_in:]
    _proj_res_ln([x_ref[...] for x_ref in xs], w_ref, res_ref, g_ref, b_ref, y_ref, yb_ref)


def proj_res_ln(xs, w, res, g, b, *, bm):
    m, d = res.shape
    k = w.shape[0]
    row = lambda i: (i, 0)
    fixed = lambda i: (0, 0)
    return pl.pallas_call(
        functools.partial(_proj_ln_kernel, n_in=len(xs)),
        grid=(m // bm,),
        in_specs=[pl.BlockSpec((bm, x.shape[1]), row) for x in xs] + [
            pl.BlockSpec((k, d), fixed, pipeline_mode=pl.Buffered(1)),
            pl.BlockSpec((bm, d), row),
            pl.BlockSpec((1, d), fixed),
            pl.BlockSpec((1, d), fixed),
        ],
        out_specs=[pl.BlockSpec((bm, d), row), pl.BlockSpec((bm, d), row)],
        out_shape=[SDS((m, d), F32), SDS((m, d), BF16)],
        compiler_params=_cparams(1),
        name="proj_res_ln",
    )(*xs, w, res, g.reshape(1, d), b.reshape(1, d))


def _hg_out_kernel(o_ref, gate_ref, ng_ref, w_ref, res_ref, g_ref, b_ref, y_ref, yb_ref):
    o = o_ref[...]
    xn = o * lax.rsqrt(jnp.mean(o * o, axis=-1, keepdims=True) + RMS_EPS) * ng_ref[...]
    xn = xn * jax.nn.sigmoid(gate_ref[...])
    _proj_res_ln([xn.astype(BF16)], w_ref, res_ref, g_ref, b_ref, y_ref, yb_ref)


def hg_out_ln(o, h_all, norm_g, w, res, g, b, *, bm):
    m, d = res.shape
    row = lambda i: (i, 0)
    fixed = lambda i: (0, 0)
    return pl.pallas_call(
        _hg_out_kernel,
        grid=(m // bm,),
        in_specs=[
            pl.BlockSpec((bm, d), row),
            pl.BlockSpec((bm, d), lambda i: (i, 3)),
            pl.BlockSpec((1, d), fixed),
            pl.BlockSpec((d, d), fixed, pipeline_mode=pl.Buffered(1)),
            pl.BlockSpec((bm, d), row),
            pl.BlockSpec((1, d), fixed),
            pl.BlockSpec((1, d), fixed),
        ],
        out_specs=[pl.BlockSpec((bm, d), row), pl.BlockSpec((bm, d), row)],
        out_shape=[SDS((m, d), F32), SDS((m, d), BF16)],
        compiler_params=_cparams(1),
        name="hg_out_ln",
    )(o, h_all, norm_g.reshape(1, d), w, res, g.reshape(1, d), b.reshape(1, d))


def _ffn_step(f, x_ref, wgu_ref, wd_ref, g_ref, b_ref, y_ref, yb_ref):
    @pl.when(f == 0)
    def _():
        x = x_ref[...]
        yb_ref[...] = x.astype(BF16)
        y_ref[...] = ALPHA * x

    bf = wd_ref.shape[0]
    gu = _dot(yb_ref[...], wgu_ref[...])
    gate, up = gu[:, :bf], gu[:, bf:]
    h = (gate * jax.nn.sigmoid(gate) * up).astype(BF16)
    y_ref[...] += _dot(h, wd_ref[...])

    @pl.when(f == pl.num_programs(1) - 1)
    def _():
        g, b = g_ref[...], b_ref[...]
        chunk = math.gcd(y_ref.shape[0], LN_ROW_CHUNK)

        def ln_chunk(i, _):
            rows = pl.ds(pl.multiple_of(i * chunk, chunk), chunk)
            y = _layer_norm_rows(y_ref[rows, :], g, b)
            y_ref[rows, :] = y
            yb_ref[rows, :] = y.astype(BF16)
            return 0

        lax.fori_loop(0, y_ref.shape[0] // chunk, ln_chunk, 0)


def _ffn_kernel(x_ref, wgu_ref, wd_ref, g_ref, b_ref, y_ref, yb_ref):
    _ffn_step(pl.program_id(1), x_ref, wgu_ref.at[0], wd_ref, g_ref, b_ref, y_ref, yb_ref)


def ffn_ln(x, wgu, wd, g, b, *, bm):
    m, d = x.shape
    nf, _, bf2 = wgu.shape
    return pl.pallas_call(
        _ffn_kernel,
        grid=(m // bm, nf),
        in_specs=[
            pl.BlockSpec((bm, d), lambda i, f: (i, 0)),
            pl.BlockSpec((1, d, bf2), lambda i, f: (f, 0, 0)),
            pl.BlockSpec((bf2 // 2, d), lambda i, f: (f, 0)),
            pl.BlockSpec((1, d), lambda i, f: (0, 0)),
            pl.BlockSpec((1, d), lambda i, f: (0, 0)),
        ],
        out_specs=[pl.BlockSpec((bm, d), lambda i, f: (i, 0)), pl.BlockSpec((bm, d), lambda i, f: (i, 0))],
        out_shape=[SDS((m, d), F32), SDS((m, d), BF16)],
        compiler_params=_cparams(2, vmem=VMEM_LIMIT_MAX),
        name="ffn_ln",
    )(x, wgu, wd, g.reshape(1, d), b.reshape(1, d))


def _ffn_cast_kernel(x_ref, wg_ref, wu_ref, wd_ref, g_ref, b_ref, y_ref, yb_ref, wgub_ref, wdb_ref):
    bf = wd_ref.shape[1]
    wgub_ref[0, :, :bf] = wg_ref[0].astype(BF16)
    wgub_ref[0, :, bf:] = wu_ref[0].astype(BF16)
    wdb_ref[...] = wd_ref[0].astype(BF16)
    _ffn_step(pl.program_id(1), x_ref, wgub_ref.at[0], wdb_ref, g_ref, b_ref, y_ref, yb_ref)


def ffn_ln_cast(x, wg, wu, wd, layer, g, b, *, bf=256):
    m, d = x.shape
    dff = wg.shape[2]
    fixed = lambda i, f: (0, 0)
    return pl.pallas_call(
        _ffn_cast_kernel,
        grid=(1, dff // bf),
        in_specs=[
            pl.BlockSpec((m, d), fixed),
            pl.BlockSpec((1, d, bf), lambda i, f: (layer, 0, f)),
            pl.BlockSpec((1, d, bf), lambda i, f: (layer, 0, f)),
            pl.BlockSpec((1, bf, d), lambda i, f: (layer, f, 0)),
            pl.BlockSpec((1, d), fixed),
            pl.BlockSpec((1, d), fixed),
        ],
        out_specs=[
            pl.BlockSpec((m, d), fixed),
            pl.BlockSpec((m, d), fixed),
            pl.BlockSpec((1, d, 2 * bf), lambda i, f: (f, 0, 0)),
            pl.BlockSpec((bf, d), lambda i, f: (f, 0)),
        ],
        out_shape=[SDS((m, d), F32), SDS((m, d), BF16), SDS((dff // bf, d, 2 * bf), BF16), SDS((dff, d), BF16)],
        compiler_params=_cparams(2),
        name="ffn_ln_cast",
    )(x, wg, wu, wd, g.reshape(1, d), b.reshape(1, d))


def _cmul(ar, ai, br, bi):
    return ar * br - ai * bi, ar * bi + ai * br


def _s5_coefs(a_r, a_i, period):
    l = a_r.shape[1]
    pows = [(a_r, a_i)]
    for _ in range(period - 1):
        pows.append(_cmul(pows[-1][0], pows[-1][1], a_r, a_i))
    t = lax.broadcasted_iota(jnp.int32, (8, l), 0) & (period - 1)
    shifts = []
    k = 1
    while k < period:
        keep = t >= k
        shifts.append((jnp.where(keep, pows[k - 1][0], 0.0), jnp.where(keep, pows[k - 1][1], 0.0)))
        k *= 2
    p_r = jnp.broadcast_to(pows[0][0], (8, l))
    p_i = jnp.broadcast_to(pows[0][1], (8, l))
    for j in range(1, period):
        p_r = jnp.where(t == j, pows[j][0], p_r)
        p_i = jnp.where(t == j, pows[j][1], p_i)
    return shifts, (p_r, p_i)


def _s5_scan_tile(x_r, x_i, shifts):
    k = 1
    for c_r, c_i in shifts:
        s_r = pltpu.roll(x_r, k, 0)
        s_i = pltpu.roll(x_i, k, 0)
        x_r, x_i = x_r + c_r * s_r - c_i * s_i, x_i + c_r * s_i + c_i * s_r
        k *= 2
    return x_r, x_i


def _s5_prompt_kernel(u_ref, bw_ref, cw_ref, ar_ref, ai_ref, d_ref, z_ref, hfin_ref, bu_ref, carry_ref):
    tt = pl.program_id(2)
    n_rows = bu_ref.shape[0]

    @pl.when(tt == 0)
    def _():
        carry_ref[...] = jnp.zeros_like(carry_ref)

    bu_ref[...] = _dot(u_ref[0].astype(BF16), bw_ref[0])
    shifts, (p_r, p_i) = _s5_coefs(ar_ref[0], ai_ref[0], 8)

    ch = S5_SCAN_CHUNK
    d_skip = d_ref[0]

    def scan_chunk(c, carry):
        for i in range(ch // 8):
            h_r, h_i = carry
            r0 = pl.multiple_of(c * ch + i * 8, 8)
            x_r, x_i = _s5_scan_tile(bu_ref[pl.ds(r0, 8), 0:S5_HW], bu_ref[pl.ds(r0, 8), S5_HW:2 * S5_HW], shifts)
            x_r, x_i = x_r + p_r * h_r - p_i * h_i, x_i + p_r * h_i + p_i * h_r
            bu_ref[pl.ds(r0, 8), 0:S5_HW] = x_r
            bu_ref[pl.ds(r0, 8), S5_HW:2 * S5_HW] = x_i
            carry = (x_r[7:8], x_i[7:8])
        return carry

    def emit_chunk(c):
        rows = pl.ds(pl.multiple_of(c * ch, ch), ch)
        y = _dot(bu_ref[rows, :].astype(BF16), cw_ref[0]) + d_skip * u_ref[0, rows, :]
        z_ref[0, rows, :] = jax.nn.gelu(y).astype(z_ref.dtype)

    def body(c, carry):
        emit_chunk(c - 1)
        return scan_chunk(c, carry)

    carry = scan_chunk(0, (carry_ref[:, 0:S5_HW], carry_ref[:, S5_HW:2 * S5_HW]))
    h_r, h_i = lax.fori_loop(1, n_rows // ch, body, carry)
    emit_chunk(n_rows // ch - 1)
    carry_ref[:, 0:S5_HW] = h_r
    carry_ref[:, S5_HW:2 * S5_HW] = h_i

    @pl.when(tt == pl.num_programs(2) - 1)
    def _():
        hfin_ref[0, 0] = carry_ref[...]


def s5_prompt(h_all, bw, cw, a_r, a_i, d_skip, *, tb):
    bsz, t, _ = h_all.shape
    blk = lambda b, g, s: (g, 0, 0)
    return pl.pallas_call(
        _s5_prompt_kernel,
        grid=(bsz, S5_NGB, t // tb),
        in_specs=[
            pl.BlockSpec((1, tb, S5_UW), lambda b, g, s: (b, s, g)),
            pl.BlockSpec((1, S5_UW, 2 * S5_HW), blk),
            pl.BlockSpec((1, 2 * S5_HW, S5_UW), blk),
            pl.BlockSpec((1, 1, S5_HW), blk),
            pl.BlockSpec((1, 1, S5_HW), blk),
            pl.BlockSpec((1, 1, S5_UW), blk),
        ],
        out_specs=[
            pl.BlockSpec((1, tb, S5_UW), lambda b, g, s: (b, s, g)),
            pl.BlockSpec((1, 1, 1, 2 * S5_HW), lambda b, g, s: (b, g, 0, 0)),
        ],
        out_shape=[SDS((bsz, t, D_S5), BF16), SDS((bsz, S5_NGB, 1, 2 * S5_HW), F32)],
        scratch_shapes=[pltpu.VMEM((tb, 2 * S5_HW), F32), pltpu.VMEM((1, 2 * S5_HW), F32)],
        compiler_params=_cparams(3),
        name="s5_prompt",
    )(h_all, bw, cw, a_r, a_i, d_skip)


def _s5_sample_kernel(u_ref, h0_ref, bw_ref, cw_ref, ar_ref, ai_ref, d_ref, z_ref, h_ref, *, period):
    u = u_ref[...]
    bu = _dot(u.astype(BF16), bw_ref[0])
    shifts, (p_r, p_i) = _s5_coefs(ar_ref[0], ai_ref[0], period)
    for i in range(u.shape[0] // 8):
        rows = slice(i * 8, (i + 1) * 8)
        x_r, x_i = _s5_scan_tile(bu[rows, 0:S5_HW], bu[rows, S5_HW:2 * S5_HW], shifts)
        h0_r = h0_ref[rows, 0:S5_HW]
        h0_i = h0_ref[rows, S5_HW:2 * S5_HW]
        h_ref[rows, 0:S5_HW] = x_r + p_r * h0_r - p_i * h0_i
        h_ref[rows, S5_HW:2 * S5_HW] = x_i + p_r * h0_i + p_i * h0_r
    y = _dot(h_ref[...].astype(BF16), cw_ref[0]) + d_ref[0] * u
    z_ref[...] = jax.nn.gelu(y).astype(z_ref.dtype)


def s5_sample(h_all, h0x, bw, cw, a_r, a_i, d_skip, *, period):
    m = h_all.shape[0]
    blk = lambda g: (g, 0, 0)
    return pl.pallas_call(
        functools.partial(_s5_sample_kernel, period=period),
        grid=(S5_NGB,),
        in_specs=[
            pl.BlockSpec((m, S5_UW), lambda g: (0, g)),
            pl.BlockSpec((m, 2 * S5_HW), lambda g: (0, g)),
            pl.BlockSpec((1, S5_UW, 2 * S5_HW), blk),
            pl.BlockSpec((1, 2 * S5_HW, S5_UW), blk),
            pl.BlockSpec((1, 1, S5_HW), blk),
            pl.BlockSpec((1, 1, S5_HW), blk),
            pl.BlockSpec((1, 1, S5_UW), blk),
        ],
        out_specs=[pl.BlockSpec((m, S5_UW), lambda g: (0, g)), pl.BlockSpec((m, 2 * S5_HW), lambda g: (0, g))],
        out_shape=[SDS((m, D_S5), BF16), SDS((m, S5_NGB * 2 * S5_HW), F32)],
        compiler_params=_cparams(1),
        name="s5_sample",
    )(h_all, h0x, bw, cw, a_r, a_i, d_skip)


def _bias_kernel(rb_ref, o_ref):
    h = pl.program_id(0)
    q = lax.broadcasted_iota(jnp.int32, (WINDOW, 2 * WINDOW), 0)
    k = lax.broadcasted_iota(jnp.int32, (WINDOW, 2 * WINDOW), 1)
    n = jnp.maximum(q + WINDOW - k, 0)
    max_exact = N_BUCKETS // 2
    nf = jnp.maximum(n, 1).astype(F32)
    large = max_exact + (jnp.log(nf / max_exact) / math.log(MAX_DISTANCE / max_exact)
                         * (N_BUCKETS - max_exact)).astype(jnp.int32)
    large = jnp.minimum(large, N_BUCKETS - 1)
    bucket = jnp.where(n < max_exact, n, large)
    out = jnp.zeros((WINDOW, 2 * WINDOW), F32)
    for b in range(N_BUCKETS):
        out = jnp.where(bucket == b, rb_ref[b, h], out)
    o_ref[0] = out


def bias_table(rel_bias):
    return pl.pallas_call(
        _bias_kernel,
        grid=(SWA_HEADS,),
        in_specs=[pl.BlockSpec(memory_space=pltpu.SMEM)],
        out_specs=pl.BlockSpec((1, WINDOW, 2 * WINDOW), lambda h: (h, 0, 0)),
        out_shape=SDS((SWA_HEADS, WINDOW, 2 * WINDOW), F32),
        compiler_params=_cparams(1),
        name="bias_table",
    )(rel_bias)


def _swa_softmax(s, valid, sink_col):
    s = jnp.where(valid, s, NEG_BIG)
    m = jnp.maximum(jnp.max(s, axis=-1, keepdims=True), sink_col)
    p = jnp.exp(s - m)
    return p / (jnp.sum(p, axis=-1, keepdims=True) + jnp.exp(sink_col - m))


def _sink_col(sinks_ref, first_head, rows_per_head):
    n = SWA_GQ * rows_per_head
    r = lax.broadcasted_iota(jnp.int32, (n, 1), 0)
    col = jnp.full((n, 1), sinks_ref[first_head], F32)
    for g in range(1, SWA_GQ):
        col = jnp.where(r >= g * rows_per_head, sinks_ref[first_head + g], col)
    return col


SWA_KV_PER_STEP = 4


def _swa_prompt_kernel(sinks_ref, q_ref, kp_ref, kc_ref, vp_ref, vc_ref, bias_ref, o_ref):
    nkv = SWA_KV_PER_STEP
    part = pl.program_id(0)
    i = pl.program_id(2)
    blk = WINDOW
    q = q_ref[0].astype(BF16)
    kk = jnp.concatenate([kp_ref[0], kc_ref[0]], axis=0).astype(BF16)
    vv = jnp.concatenate([vp_ref[0], vc_ref[0]], axis=0).astype(BF16)
    qi = lax.broadcasted_iota(jnp.int32, (SWA_GQ * blk, 2 * blk), 0) & (blk - 1)
    ki = lax.broadcasted_iota(jnp.int32, (SWA_GQ * blk, 2 * blk), 1)
    valid = (ki > qi) & (ki <= qi + blk) & ((ki >= blk) | (i > 0))
    outs = []
    for j in range(nkv):
        k_j = kk[:, j * HEAD_DIM:(j + 1) * HEAD_DIM]
        v_j = vv[:, j * HEAD_DIM:(j + 1) * HEAD_DIM]
        q4 = jnp.concatenate([q[:, (j * SWA_GQ + g) * HEAD_DIM:(j * SWA_GQ + g + 1) * HEAD_DIM]
                              for g in range(SWA_GQ)], axis=0)
        s = _dot_nt(q4, k_j) * (HEAD_DIM ** -0.5)
        s = s + bias_ref[j * SWA_GQ:(j + 1) * SWA_GQ].reshape(SWA_GQ * blk, 2 * blk)
        p = _swa_softmax(s, valid, _sink_col(sinks_ref, (part * nkv + j) * SWA_GQ, blk))
        o4 = _dot(p.astype(BF16), v_j)
        outs += [o4[g * blk:(g + 1) * blk] for g in range(SWA_GQ)]
    o_ref[0] = jnp.concatenate(outs, axis=1).astype(o_ref.dtype)


def swa_prompt(h_all, sinks, table):
    bsz, t, _ = h_all.shape
    nblk = t // WINDOW
    nkv = SWA_KV_PER_STEP
    qw = nkv * SWA_GQ * HEAD_DIM
    kw = nkv * HEAD_DIM
    q0, k0, v0 = D_S5 // qw, (D_S5 + D_SWA_Q) // kw, (D_S5 + D_SWA_Q + D_SWA_KV) // kw
    cur = lambda c0: (lambda p, b, i: (b, i, c0 + p))
    prev = lambda c0: (lambda p, b, i: (b, jnp.maximum(i - 1, 0), c0 + p))
    return pl.pallas_call(
        _swa_prompt_kernel,
        grid=(SWA_KV_HEADS // nkv, bsz, nblk),
        in_specs=[
            pl.BlockSpec(memory_space=pltpu.SMEM),
            pl.BlockSpec((1, WINDOW, qw), cur(q0)),
            pl.BlockSpec((1, WINDOW, kw), prev(k0)),
            pl.BlockSpec((1, WINDOW, kw), cur(k0)),
            pl.BlockSpec((1, WINDOW, kw), prev(v0)),
            pl.BlockSpec((1, WINDOW, kw), cur(v0)),
            pl.BlockSpec((nkv * SWA_GQ, WINDOW, 2 * WINDOW), lambda p, b, i: (p, 0, 0)),
        ],
        out_specs=pl.BlockSpec((1, WINDOW, qw), lambda p, b, i: (b, i, p)),
        out_shape=SDS((bsz, t, D_SWA_Q), BF16),
        compiler_params=_cparams(3),
        name="swa_prompt",
    )(sinks, h_all, h_all, h_all, h_all, h_all, table)


def _swa_sample_kernel(sinks_ref, q_ref, kk_ref, vv_ref, bias_ref, o_ref, *, t_new):
    rows = q_ref.shape[0]
    n_keys = kk_ref.shape[1]
    q = q_ref[...].astype(BF16)
    r = lax.broadcasted_iota(jnp.int32, (SWA_GQ * rows, n_keys), 0)
    ti = r & (t_new - 1)
    ki = lax.broadcasted_iota(jnp.int32, (SWA_GQ * rows, n_keys), 1)
    valid = (ki > ti) & (ki <= ti + WINDOW)
    second = (r & (rows - 1)) >= t_new
    second_o = (lax.broadcasted_iota(jnp.int32, (SWA_GQ * rows, HEAD_DIM), 0) & (rows - 1)) >= t_new
    outs = []
    for j in range(SWA_KV_HEADS):
        cols = slice(j * HEAD_DIM, (j + 1) * HEAD_DIM)
        q4 = jnp.concatenate([q[:, (j * SWA_GQ + g) * HEAD_DIM:(j * SWA_GQ + g + 1) * HEAD_DIM]
                              for g in range(SWA_GQ)], axis=0)
        s = jnp.where(second, _dot_nt(q4, kk_ref[1][:, cols].astype(BF16)),
                      _dot_nt(q4, kk_ref[0][:, cols].astype(BF16))) * (HEAD_DIM ** -0.5)
        s = s + bias_ref[j * SWA_GQ:(j + 1) * SWA_GQ].reshape(SWA_GQ * rows, n_keys)
        p = _swa_softmax(s, valid, _sink_col(sinks_ref, j * SWA_GQ, rows)).astype(BF16)
        o4 = jnp.where(second_o, _dot(p, vv_ref[1][:, cols].astype(BF16)), _dot(p, vv_ref[0][:, cols].astype(BF16)))
        outs += [o4[g * rows:(g + 1) * rows] for g in range(SWA_GQ)]
    o_ref[...] = jnp.concatenate(outs, axis=1).astype(o_ref.dtype)


def swa_sample(h_all, kk, vv, sinks, bias_s, *, t_new):
    m = h_all.shape[0]
    n_keys = kk.shape[1]
    rows = 2 * t_new
    return pl.pallas_call(
        functools.partial(_swa_sample_kernel, t_new=t_new),
        grid=(m // rows,),
        in_specs=[
            pl.BlockSpec(memory_space=pltpu.SMEM),
            pl.BlockSpec((rows, D_SWA_Q), lambda i: (i, D_S5 // D_SWA_Q)),
            pl.BlockSpec((2, n_keys, D_SWA_KV), lambda i: (i, 0, 0)),
            pl.BlockSpec((2, n_keys, D_SWA_KV), lambda i: (i, 0, 0)),
            pl.BlockSpec((SWA_HEADS, rows, n_keys), lambda i: (0, 0, 0)),
        ],
        out_specs=pl.BlockSpec((rows, D_SWA_Q), lambda i: (i, 0)),
        out_shape=SDS((m, D_SWA_Q), BF16),
        compiler_params=_cparams(1),
        name="swa_sample",
    )(sinks, h_all, kk, vv, bias_s)


def _hg_gates(q, fz, lb):
    qs = q * jax.nn.sigmoid(q)
    f = lb + (1.0 - lb) * jax.nn.sigmoid(fz)
    return qs, jnp.log(f), 1.0 - f


HG_GROUP = 4


def _hgrn_prompt_kernel(q_ref, f_ref, v_ref, lb_ref, o_ref, st_ref, s_scr, *, n_heads):
    tt = pl.program_id(2)
    n_rows = q_ref.shape[1]
    c, ng = HG_CHUNK, HG_GROUP
    gr = c * ng

    @pl.when(tt == 0)
    def _():
        s_scr[...] = jnp.zeros_like(s_scr)

    ri = lax.broadcasted_iota(jnp.int32, (gr, gr), 0)
    ci = lax.broadcasted_iota(jnp.int32, (gr, gr), 1)
    causal = (ri >= ci) & ((ri // c) == (ci // c))
    t_in = lax.broadcasted_iota(jnp.int32, (gr, HG_DK), 0) & (c - 1)
    own = ((lax.broadcasted_iota(jnp.int32, (gr, ng * HG_DK), 0) // c)
           == (lax.broadcasted_iota(jnp.int32, (gr, ng * HG_DK), 1) // HG_DK)).astype(BF16)

    def body(gi, _):
        r0 = pl.multiple_of(gi * gr, gr)
        for hd in range(n_heads):
            cols = slice(hd * HG_DK, (hd + 1) * HG_DK)
            qs, lf, k = _hg_gates(q_ref[0, pl.ds(r0, gr), cols], f_ref[0, pl.ds(r0, gr), cols], lb_ref[:, cols])
            v = v_ref[0, pl.ds(r0, gr), cols].astype(BF16)
            cum = lf
            sh = 1
            while sh < c:
                cum = cum + jnp.where(t_in >= sh, pltpu.roll(cum, sh, 0), 0.0)
                sh *= 2
            last3 = cum.reshape(ng, c, HG_DK)[:, c - 1:c, :]
            last = jnp.broadcast_to(last3, (ng, c, HG_DK)).reshape(gr, HG_DK)
            qt = (qs * jnp.exp(cum)).astype(BF16)
            kt = (k * jnp.exp(-cum)).astype(BF16)
            kl = (k * jnp.exp(last - cum)).astype(BF16)
            decay = jnp.exp(last3)
            attn = jnp.where(causal, _dot_nt(qt, kt), 0.0).astype(BF16)
            kv = _dot_tn(v, jnp.concatenate([kl] * ng, axis=1) * own)
            st = s_scr[hd]
            starts = []
            for j in range(ng):
                starts.append(st)
                st = st * decay[j] + kv[:, j * HG_DK:(j + 1) * HG_DK]
            s_scr[hd] = st
            s_cat = jnp.concatenate(starts, axis=1).astype(BF16)
            o_ref[0, pl.ds(r0, gr), cols] = (_dot(attn, v)
                                            + _dot_nt(jnp.concatenate([qt] * ng, axis=1) * own, s_cat))
        return 0

    lax.fori_loop(0, n_rows // gr, body, 0, unroll=True)

    @pl.when(tt == pl.num_programs(2) - 1)
    def _():
        st_ref[0] = s_scr[...]


def hgrn_prompt(h_all, lb, *, tb, n_heads=2):
    bsz, t, _ = h_all.shape
    hw = n_heads * HG_DK
    nhb = D_MODEL // hw
    col = lambda c0: (lambda b, h, s: (b, s, c0 * nhb + h))
    return pl.pallas_call(
        functools.partial(_hgrn_prompt_kernel, n_heads=n_heads),
        grid=(bsz, nhb, t // tb),
        in_specs=[
            pl.BlockSpec((1, tb, hw), col(0)),
            pl.BlockSpec((1, tb, hw), col(1)),
            pl.BlockSpec((1, tb, hw), col(2)),
            pl.BlockSpec((1, hw), lambda b, h, s: (0, h)),
        ],
        out_specs=[
            pl.BlockSpec((1, tb, hw), lambda b, h, s: (b, s, h)),
            pl.BlockSpec((1, n_heads, HG_DK, HG_DK), lambda b, h, s: (b, h, 0, 0)),
        ],
        out_shape=[SDS((bsz, t, D_MODEL), F32), SDS((bsz, HG_HEADS, HG_DK, HG_DK), F32)],
        scratch_shapes=[pltpu.VMEM((n_heads, HG_DK, HG_DK), F32)],
        compiler_params=_cparams(3),
        name="hgrn_prompt",
    )(h_all, h_all, h_all, lb.reshape(1, D_MODEL))


def _hgrn_sample_kernel(q_ref, f_ref, v_ref, lb_ref, s0_ref, o_ref, s_ref, *, n_heads, t_new):
    rows = q_ref.shape[0]
    ri = lax.broadcasted_iota(jnp.int32, (rows, rows), 0)
    ci = lax.broadcasted_iota(jnp.int32, (rows, rows), 1)
    causal = (ri >= ci) & ((ri >= t_new) == (ci >= t_new))
    tril = causal.astype(F32)
    second = lax.broadcasted_iota(jnp.int32, (rows, HG_DK), 0) >= t_new
    eye = (lax.broadcasted_iota(jnp.int32, (HG_DK, HG_DK), 0)
           == lax.broadcasted_iota(jnp.int32, (HG_DK, HG_DK), 1))
    for hd in range(n_heads):
        cols = slice(hd * HG_DK, (hd + 1) * HG_DK)
        qs, lf, k = _hg_gates(q_ref[:, cols], f_ref[:, cols], lb_ref[:, cols])
        v = v_ref[:, cols].astype(BF16)
        cum = jnp.dot(tril, lf, preferred_element_type=F32, precision=lax.Precision.HIGHEST)
        last0 = cum[t_new - 1:t_new]
        last1 = cum[rows - 1:rows]
        last = jnp.where(second, last1, last0)
        qt = (qs * jnp.exp(cum)).astype(BF16)
        kt = (k * jnp.exp(-cum)).astype(BF16)
        kl = k * jnp.exp(last - cum)
        attn = jnp.where(causal, _dot_nt(qt, kt), 0.0).astype(BF16)
        s0a = s0_ref[0, hd]
        s0b = s0_ref[1, hd]
        o_ref[:, cols] = _dot(attn, v) + jnp.where(second, _dot(qt, s0b.astype(BF16)), _dot(qt, s0a.astype(BF16)))
        for bb, (s0, lst) in enumerate(((s0a, last0), (s0b, last1))):
            decay = jnp.sum(jnp.where(eye, jnp.exp(lst), 0.0), axis=1, keepdims=True)
            kl_b = jnp.where(second == (bb == 1), kl, 0.0).astype(BF16)
            s_ref[bb, hd] = decay * s0 + _dot_tn(kl_b, v)


def hgrn_sample(h_all, lb, s0, *, t_new, n_heads=8):
    m = h_all.shape[0]
    rows = 2 * t_new
    hw = n_heads * HG_DK
    nhb = D_MODEL // hw
    col = lambda c0: (lambda i, h: (i, c0 * nhb + h))
    return pl.pallas_call(
        functools.partial(_hgrn_sample_kernel, n_heads=n_heads, t_new=t_new),
        grid=(m // rows, nhb),
        in_specs=[
            pl.BlockSpec((rows, hw), col(0)),
            pl.BlockSpec((rows, hw), col(1)),
            pl.BlockSpec((rows, hw), col(2)),
            pl.BlockSpec((1, hw), lambda i, h: (0, h)),
            pl.BlockSpec((2, n_heads, HG_DK, HG_DK), lambda i, h: (i, h, 0, 0)),
        ],
        out_specs=[
            pl.BlockSpec((rows, hw), lambda i, h: (i, h)),
            pl.BlockSpec((2, n_heads, HG_DK, HG_DK), lambda i, h: (i, h, 0, 0)),
        ],
        out_shape=[SDS((m, D_MODEL), F32), SDS(s0.shape, F32)],
        compiler_params=_cparams(2),
        name="hgrn_sample",
    )(h_all, h_all, h_all, lb.reshape(1, D_MODEL), s0)


def _xattn_kernel(y_ref, yb_ref, wq_ref, mk_ref, mv_ref, wo_ref, g_ref, b_ref, o_ref, ob_ref, *, n_seq):
    rows = y_ref.shape[0]
    q = _dot(yb_ref[...], wq_ref[...]).astype(BF16)
    second = lax.broadcasted_iota(jnp.int32, (rows, 1), 0) >= rows // 2

    def mem_head(ref, seq, h):
        if len(ref.shape) == 3:
            return ref[seq, :, h * MEM_HEAD_DIM:(h + 1) * MEM_HEAD_DIM].astype(BF16)
        return ref[0, seq, :, h, :].astype(BF16)

    heads = []
    for h in range(MEM_HEADS):
        qh = q[:, h * MEM_HEAD_DIM:(h + 1) * MEM_HEAD_DIM]
        s = _dot_nt(qh, mem_head(mk_ref, 0, h))
        if n_seq == 2:
            s = jnp.where(second, _dot_nt(qh, mem_head(mk_ref, 1, h)), s)
        s = s * (MEM_HEAD_DIM ** -0.5)
        e = jnp.exp(s - jnp.max(s, axis=-1, keepdims=True))
        p = (e / jnp.sum(e, axis=-1, keepdims=True)).astype(BF16)
        o = _dot(p, mem_head(mv_ref, 0, h))
        if n_seq == 2:
            o = jnp.where(second, _dot(p, mem_head(mv_ref, 1, h)), o)
        heads.append(o.astype(BF16))
    _proj_res_ln([jnp.concatenate(heads, axis=1)], wo_ref, y_ref, g_ref, b_ref, o_ref, ob_ref)


def xattn_ln(y, yb, wq, mk, mv, wo, g, b, *, bm, rows_per_seq, layer=None):
    m, d = y.shape
    n_seq = 2 if bm == 2 * rows_per_seq else 1
    assert n_seq == 2 or rows_per_seq % bm == 0
    row = lambda i: (i, 0)
    fixed = lambda i: (0, 0)
    if layer is None:
        mem = (lambda i: (i, 0, 0)) if n_seq == 2 else (lambda i: (i * bm // rows_per_seq, 0, 0))
        mem_spec = pl.BlockSpec((n_seq, N_MEM, D_MEM), mem)
    else:
        assert n_seq == 2
        mem_spec = pl.BlockSpec((1, n_seq, N_MEM, MEM_HEADS, MEM_HEAD_DIM), lambda i: (layer, i, 0, 0, 0))
    return pl.pallas_call(
        functools.partial(_xattn_kernel, n_seq=n_seq),
        grid=(m // bm,),
        in_specs=[
            pl.BlockSpec((bm, d), row),
            pl.BlockSpec((bm, d), row),
            pl.BlockSpec((d, D_MEM), fixed),
            mem_spec,
            mem_spec,
            pl.BlockSpec((D_MEM, d), fixed),
            pl.BlockSpec((1, d), fixed),
            pl.BlockSpec((1, d), fixed),
        ],
        out_specs=[pl.BlockSpec((bm, d), row), pl.BlockSpec((bm, d), row)],
        out_shape=[SDS((m, d), F32), SDS((m, d), BF16)],
        compiler_params=_cparams(1),
        name="xattn_ln",
    )(y, yb, wq, mk, mv, wo, g.reshape(1, d), b.reshape(1, d))


def _s5_discretize(lam_re, lam_im, log_dt, b_re, b_im):
    lr = jnp.minimum(lam_re.astype(F32), -1e-4)
    li = lam_im.astype(F32)
    dt = jnp.exp(log_dt.astype(F32))[:, None]
    mag = jnp.exp(lr * dt)
    a_re = mag * jnp.cos(li * dt)
    a_im = mag * jnp.sin(li * dt)
    den = lr * lr + li * li
    fr = ((a_re - 1.0) * lr + a_im * li) / den
    fi = (a_im * lr - (a_re - 1.0) * li) / den
    br, bi = b_re.astype(F32), b_im.astype(F32)
    bb_re = fr[..., None] * br - fi[..., None] * bi
    bb_im = fr[..., None] * bi + fi[..., None] * br
    return a_re, a_im, bb_re, bb_im


def _s5_block_weights(a_re, a_im, bb_re, bb_im, c_re, c_im, d_skip):
    eye = jnp.eye(S5_GB, dtype=F32)
    shp_b = (S5_NGB, S5_GB, S5_STATE, S5_GROUP)
    shp_c = (S5_NGB, S5_GB, S5_GROUP, S5_STATE)
    blk_b = lambda w: jnp.einsum('bgph,gk->bghkp', w.reshape(shp_b), eye).reshape(S5_NGB, S5_UW, S5_HW)
    blk_c = lambda w: jnp.einsum('bghp,gk->bkpgh', w.reshape(shp_c), eye).reshape(S5_NGB, S5_HW, S5_UW)
    bw = jnp.concatenate([blk_b(bb_re), blk_b(bb_im)], axis=-1).astype(BF16)
    cw = jnp.concatenate([blk_c(c_re.astype(F32)), blk_c(-c_im.astype(F32))], axis=1).astype(BF16)
    return (bw, cw, a_re.reshape(S5_NGB, 1, S5_HW), a_im.reshape(S5_NGB, 1, S5_HW),
            d_skip.astype(F32).reshape(S5_NGB, 1, S5_UW))


def _s5_state_to_blocks(s_re, s_im):
    b = s_re.shape[0]
    return jnp.concatenate([s_re.reshape(b, S5_NGB, S5_HW), s_im.reshape(b, S5_NGB, S5_HW)],
                           axis=-1).reshape(b, S5_NGB * 2 * S5_HW)


def _s5_state_from_blocks(h):
    b = h.shape[0]
    return (h[:, :, :S5_HW].reshape(b, S5_GROUPS, S5_STATE), h[:, :, S5_HW:].reshape(b, S5_GROUPS, S5_STATE))


def kernel(x_prompt, x_sample, cache_mem_k, cache_mem_v, cache_swa_k, cache_swa_v, state_s5_re, state_s5_im, state_hgrn, mem_prompt, rel_bias, w_even_in, s5_lam_re, s5_lam_im, s5_log_dt, s5_b_re, s5_b_im, s5_c_re, s5_c_im, s5_d, s5_w_glu, swa_sinks, w_even_out, hg_lb_logits, w_odd_in, hg_norm_g, w_odd_out, w_mem_q, w_mem_k, w_mem_v, w_mem_o, w_ffn_gate, w_ffn_up, w_ffn_down, ln_g, ln_b):
    bsz, seq, d = x_prompt.shape
    dec_b, dec_t, _ = x_sample.shape
    mp, ms = bsz * seq, dec_b * dec_t
    w_buf = cache_swa_k.shape[2]
    bf = lambda w: w.astype(BF16)

    lb_soft = jax.nn.softmax(hg_lb_logits.astype(F32), axis=0)
    lower_bounds = jnp.cumsum(lb_soft, axis=0) - lb_soft[0]
    table = bias_table(rel_bias.astype(F32))
    bias_s = jnp.tile(table[:, :dec_t], (1, 2, 1))
    mem_b = bf(mem_prompt).reshape(bsz * N_MEM, d)

    yp = x_prompt.reshape(mp, d)
    ys = x_sample.reshape(ms, d)
    ypb, ysb = bf(yp), bf(ys)
    p_mem_k, p_mem_v = [], []
    p_swa_k, p_swa_v, p_s5_re, p_s5_im, p_hg = [], [], [], [], []
    s_swa_k, s_swa_v, s_s5_re, s_s5_im, s_hg = [], [], [], [], []

    for l in range(DEPTH):
        j = l // 2
        g0, b0 = ln_g[l, 0], ln_b[l, 0]
        if l % 2 == 0:
            a_re, a_im, bb_re, bb_im = _s5_discretize(s5_lam_re[j], s5_lam_im[j], s5_log_dt[j], s5_b_re[j], s5_b_im[j])
            s5w = _s5_block_weights(a_re, a_im, bb_re, bb_im, s5_c_re[j], s5_c_im[j], s5_d[j])
            w_glu, w_out = cast_layer(s5_w_glu, j), cast_layer(w_even_out, j)
            sinks = swa_sinks[j].astype(F32)
            kv0 = D_S5 + D_SWA_Q

            hs, w_in = matmul_cast(ysb, w_even_in, j, bn=512)
            hp = matmul(ypb, w_in, bm=1024, bn=1024).reshape(bsz, seq, -1)
            z, hfin = s5_prompt(hp, *s5w, tb=512)
            s5_out = glu(z.reshape(mp, D_S5), w_glu, bm=1024, bn=1024)
            att = swa_prompt(hp, sinks, table).reshape(mp, D_SWA_Q)
            hr, hi = _s5_state_from_blocks(hfin[:, :, 0])
            p_s5_re.append(hr); p_s5_im.append(hi)
            p_swa_k.append(hp[:, seq - w_buf:, kv0:kv0 + D_SWA_KV].reshape(bsz, w_buf, SWA_KV_HEADS, HEAD_DIM))
            p_swa_v.append(hp[:, seq - w_buf:, kv0 + D_SWA_KV:].reshape(bsz, w_buf, SWA_KV_HEADS, HEAD_DIM))
            yp, ypb = proj_res_ln([s5_out, att], w_out, yp, g0, b0, bm=128)

            h0x = jnp.repeat(_s5_state_to_blocks(state_s5_re[j].astype(F32), state_s5_im[j].astype(F32)), dec_t, axis=0)
            z, h_steps = s5_sample(hs, h0x, *s5w, period=dec_t)
            s5_out = glu(z, w_glu, bm=ms, bn=1024)
            hs3 = hs.reshape(dec_b, dec_t, -1)
            kpad = jnp.zeros((dec_b, 2 * WINDOW - w_buf - dec_t, D_SWA_KV), F32)
            kk = jnp.concatenate([cache_swa_k[j].reshape(dec_b, w_buf, D_SWA_KV).astype(F32),
                                  hs3[:, :, kv0:kv0 + D_SWA_KV], kpad], axis=1)
            vv = jnp.concatenate([cache_swa_v[j].reshape(dec_b, w_buf, D_SWA_KV).astype(F32),
                                  hs3[:, :, kv0 + D_SWA_KV:], kpad], axis=1)
            att = swa_sample(hs, kk, vv, sinks, bias_s, t_new=dec_t)
            hr, hi = _s5_state_from_blocks(h_steps.reshape(dec_b, dec_t, S5_NGB, 2 * S5_HW)[:, dec_t - 1])
            s_s5_re.append(hr); s_s5_im.append(hi)
            s_swa_k.append(kk[:, dec_t:dec_t + w_buf].reshape(dec_b, w_buf, SWA_KV_HEADS, HEAD_DIM))
            s_swa_v.append(vv[:, dec_t:dec_t + w_buf].reshape(dec_b, w_buf, SWA_KV_HEADS, HEAD_DIM))
            ys, ysb = proj_res_ln([s5_out, att], w_out, ys, g0, b0, bm=ms)
        else:
            w_out = cast_layer(w_odd_out, j)
            lb = lower_bounds[l]

            hs, w_in = matmul_cast(ysb, w_odd_in, j, bn=512)
            hp = matmul(ypb, w_in, bm=1024, bn=1024)
            o, st = hgrn_prompt(hp.reshape(bsz, seq, -1), lb, tb=512, n_heads=4)
            p_hg.append(jnp.swapaxes(st, -1, -2))
            yp, ypb = hg_out_ln(o.reshape(mp, d), hp, hg_norm_g[j], w_out, yp, g0, b0, bm=128)

            o, s_new = hgrn_sample(hs, lb, state_hgrn[j].astype(F32), t_new=dec_t)
            s_hg.append(s_new)
            ys, ysb = hg_out_ln(o, hs, hg_norm_g[j], w_out, ys, g0, b0, bm=ms)

        wq, wk, wv, wo = (cast_layer(w, l) for w in (w_mem_q, w_mem_k, w_mem_v, w_mem_o))
        mk = matmul(mem_b, wk, bm=bsz * N_MEM, bn=D_MEM).reshape(bsz, N_MEM, D_MEM)
        mv = matmul(mem_b, wv, bm=bsz * N_MEM, bn=D_MEM).reshape(bsz, N_MEM, D_MEM)
        p_mem_k.append(mk.reshape(bsz, N_MEM, MEM_HEADS, MEM_HEAD_DIM))
        p_mem_v.append(mv.reshape(bsz, N_MEM, MEM_HEADS, MEM_HEAD_DIM))
        yp, ypb = xattn_ln(yp, ypb, wq, mk, mv, wo, ln_g[l, 1], ln_b[l, 1], bm=256, rows_per_seq=seq)
        ys, ysb = xattn_ln(ys, ysb, wq, cache_mem_k, cache_mem_v, wo, ln_g[l, 1], ln_b[l, 1],
                           bm=2 * dec_t, rows_per_seq=dec_t, layer=l)

        ys, ysb, wgu, wd = ffn_ln_cast(ys, w_ffn_gate, w_ffn_up, w_ffn_down, l, ln_g[l, 2], ln_b[l, 2])
        yp, ypb = ffn_ln(yp, wgu, wd, ln_g[l, 2], ln_b[l, 2], bm=512)

    return (yp.reshape(bsz, seq, d), ys.reshape(dec_b, dec_t, d),
            jnp.stack(p_mem_k), jnp.stack(p_mem_v),
            jnp.stack(p_swa_k), jnp.stack(p_swa_v),
            jnp.stack(p_s5_re), jnp.stack(p_s5_im), jnp.stack(p_hg),
            jnp.stack(s_swa_k), jnp.stack(s_swa_v),
            jnp.stack(s_s5_re), jnp.stack(s_s5_im), jnp.stack(s_hg))
```

```python
import functools
import math

import jax
import jax.numpy as jnp
from jax import lax
from jax.experimental import pallas as pl
from jax.experimental.pallas import tpu as pltpu

F32 = jnp.float32
BF16 = jnp.bfloat16
SDS = jax.ShapeDtypeStruct

D_MODEL = 4096
DEPTH = 2
ALPHA = (2 * DEPTH) ** 0.25
LN_EPS = 1e-5
RMS_EPS = 1e-6
NEG_BIG = -1e30

S5_GROUPS, S5_GROUP, S5_STATE = 128, 16, 64
D_S5 = S5_GROUPS * S5_GROUP
S5_GB = 16
S5_NGB = S5_GROUPS // S5_GB
S5_UW = S5_GB * S5_GROUP
S5_HW = S5_GB * S5_STATE
S5_SCAN_CHUNK = 128

HEAD_DIM, SWA_HEADS, SWA_KV_HEADS, SWA_GQ = 64, 32, 8, 4
D_SWA_Q, D_SWA_KV = 2048, 512
WINDOW = 128
N_BUCKETS, MAX_DISTANCE = 32, 128

HG_DK, HG_HEADS, HG_CHUNK = 128, 32, 32
N_MEM, MEM_HEADS, MEM_HEAD_DIM, D_MEM = 256, 4, 128, 512

VMEM_LIMIT = 56 * 1024 * 1024
VMEM_LIMIT_MAX = 60 * 1024 * 1024
PROJ_COL_CHUNKS = 4
LN_ROW_CHUNK = 64


def _cparams(n_axes, vmem=VMEM_LIMIT):
    return pltpu.CompilerParams(dimension_semantics=("arbitrary",) * n_axes, vmem_limit_bytes=vmem)


def _dot(a, b):
    return jnp.dot(a, b, preferred_element_type=F32)


def _dot_nt(a, b):
    return lax.dot_general(a, b, (((1,), (1,)), ((), ())), preferred_element_type=F32)


def _dot_tn(a, b):
    return lax.dot_general(a, b, (((0,), (0,)), ((), ())), preferred_element_type=F32)


def _layer_norm_rows(s, g, b):
    mu = jnp.mean(s, axis=-1, keepdims=True)
    c = s - mu
    var = jnp.mean(c * c, axis=-1, keepdims=True)
    return c * lax.rsqrt(var + LN_EPS) * g + b


def _proj_res_sum(xs, w_ref, res_ref, s_ref):
    d = s_ref.shape[1]
    cw = d // PROJ_COL_CHUNKS
    total = None
    for j in range(PROJ_COL_CHUNKS):
        cols = slice(j * cw, (j + 1) * cw)
        s = ALPHA * res_ref[:, cols]
        off = 0
        for x in xs:
            s = s + _dot(x, w_ref[off:off + x.shape[1], cols])
            off += x.shape[1]
        s_ref[:, cols] = s
        part = jnp.sum(s, axis=-1, keepdims=True)
        total = part if total is None else total + part
    return total


def _ln_rows_from(s_ref, total, g_ref, b_ref, y_ref, yb_ref):
    d = s_ref.shape[1]
    cw = d // PROJ_COL_CHUNKS
    mu = total * (1.0 / d)
    sq = None
    for j in range(PROJ_COL_CHUNKS):
        c = s_ref[:, j * cw:(j + 1) * cw] - mu
        part = jnp.sum(c * c, axis=-1, keepdims=True)
        sq = part if sq is None else sq + part
    rstd = lax.rsqrt(sq * (1.0 / d) + LN_EPS)
    for j in range(PROJ_COL_CHUNKS):
        cols = slice(j * cw, (j + 1) * cw)
        y = (s_ref[:, cols] - mu) * rstd * g_ref[:, cols] + b_ref[:, cols]
        y_ref[:, cols] = y
        yb_ref[:, cols] = y.astype(BF16)


def _proj_res_ln(xs, w_ref, res_ref, g_ref, b_ref, y_ref, yb_ref):
    total = _proj_res_sum(xs, w_ref, res_ref, y_ref)
    _ln_rows_from(y_ref, total, g_ref, b_ref, y_ref, yb_ref)


def _cast_kernel(w_ref, o_ref):
    o_ref[...] = w_ref[0].astype(o_ref.dtype)


CAST_BLOCK_BYTES = 8 * 1024 * 1024


def cast_layer(w, layer):
    _, k, n = w.shape
    rows = next(r for r in (4096, 2048, 1024, 512, 256, 128) if k % r == 0 and r * n * 4 <= CAST_BLOCK_BYTES)
    return pl.pallas_call(
        _cast_kernel,
        grid=(k // rows,),
        in_specs=[pl.BlockSpec((1, rows, n), lambda i: (layer, i, 0))],
        out_specs=pl.BlockSpec((rows, n), lambda i: (i, 0)),
        out_shape=SDS((k, n), BF16),
        compiler_params=_cparams(1),
        name="cast_layer",
    )(w)


def _mm_kernel(x_ref, w_ref, o_ref):
    o_ref[...] = _dot(x_ref[...], w_ref[...]).astype(o_ref.dtype)


def matmul(x, w, *, bm, bn, out_dtype=F32):
    m, k = x.shape
    n = w.shape[1]
    return pl.pallas_call(
        _mm_kernel,
        grid=(m // bm, n // bn),
        in_specs=[pl.BlockSpec((bm, k), lambda i, j: (i, 0)), pl.BlockSpec((k, bn), lambda i, j: (0, j))],
        out_specs=pl.BlockSpec((bm, bn), lambda i, j: (i, j)),
        out_shape=SDS((m, n), out_dtype),
        compiler_params=_cparams(2),
        name="matmul",
    )(x, w)


def _mm_cast_kernel(x_ref, w_ref, o_ref, wb_ref):
    wb_ref[...] = w_ref[0].astype(BF16)
    o_ref[...] = _dot(x_ref[...], wb_ref[...]).astype(o_ref.dtype)


def matmul_cast(x, w, layer, *, bn, out_dtype=F32):
    m, k = x.shape
    n = w.shape[2]
    return pl.pallas_call(
        _mm_cast_kernel,
        grid=(n // bn,),
        in_specs=[pl.BlockSpec((m, k), lambda j: (0, 0)), pl.BlockSpec((1, k, bn), lambda j: (layer, 0, j))],
        out_specs=[pl.BlockSpec((m, bn), lambda j: (0, j)), pl.BlockSpec((k, bn), lambda j: (0, j))],
        out_shape=[SDS((m, n), out_dtype), SDS((k, n), BF16)],
        compiler_params=_cparams(1),
        name="matmul_cast",
    )(x, w)


def _glu_kernel(z_ref, zt_ref, w_ref, o_ref):
    a = _dot(z_ref[...], w_ref[...])
    o_ref[...] = (zt_ref[...].astype(F32) * jax.nn.sigmoid(a)).astype(o_ref.dtype)


def glu(z, w, *, bm, bn):
    m, k = z.shape
    n = w.shape[1]
    return pl.pallas_call(
        _glu_kernel,
        grid=(m // bm, n // bn),
        in_specs=[pl.BlockSpec((bm, k), lambda i, j: (i, 0)), pl.BlockSpec((bm, bn), lambda i, j: (i, j)),
                  pl.BlockSpec((k, bn), lambda i, j: (0, j))],
        out_specs=pl.BlockSpec((bm, bn), lambda i, j: (i, j)),
        out_shape=SDS((m, n), BF16),
        compiler_params=_cparams(2),
        name="glu",
    )(z, z, w)


def _proj_ln_kernel(*refs, n_in):
    xs = refs[:n_in]
    w_ref, res_ref, g_ref, b_ref, y_ref, yb_ref = refs[n_in:]
    _proj_res_ln([x_ref[...] for x_ref in xs], w_ref, res_ref, g_ref, b_ref, y_ref, yb_ref)


def proj_res_ln(xs, w, res, g, b, *, bm):
    m, d = res.shape
    k = w.shape[0]
    row = lambda i: (i, 0)
    fixed = lambda i: (0, 0)
    return pl.pallas_call(
        functools.partial(_proj_ln_kernel, n_in=len(xs)),
        grid=(m // bm,),
        in_specs=[pl.BlockSpec((bm, x.shape[1]), row) for x in xs] + [
            pl.BlockSpec((k, d), fixed, pipeline_mode=pl.Buffered(1)),
            pl.BlockSpec((bm, d), row),
            pl.BlockSpec((1, d), fixed),
            pl.BlockSpec((1, d), fixed),
        ],
        out_specs=[pl.BlockSpec((bm, d), row), pl.BlockSpec((bm, d), row)],
        out_shape=[SDS((m, d), F32), SDS((m, d), BF16)],
        compiler_params=_cparams(1),
        name="proj_res_ln",
    )(*xs, w, res, g.reshape(1, d), b.reshape(1, d))


def _hg_out_kernel(o_ref, gate_ref, ng_ref, w_ref, res_ref, g_ref, b_ref, y_ref, yb_ref):
    o = o_ref[...]
    xn = o * lax.rsqrt(jnp.mean(o * o, axis=-1, keepdims=True) + RMS_EPS) * ng_ref[...]
    xn = xn * jax.nn.sigmoid(gate_ref[...])
    _proj_res_ln([xn.astype(BF16)], w_ref, res_ref, g_ref, b_ref, y_ref, yb_ref)


def hg_out_ln(o, h_all, norm_g, w, res, g, b, *, bm):
    m, d = res.shape
    row = lambda i: (i, 0)
    fixed = lambda i: (0, 0)
    return pl.pallas_call(
        _hg_out_kernel,
        grid=(m // bm,),
        in_specs=[
            pl.BlockSpec((bm, d), row),
            pl.BlockSpec((bm, d), lambda i: (i, 3)),
            pl.BlockSpec((1, d), fixed),
            pl.BlockSpec((d, d), fixed, pipeline_mode=pl.Buffered(1)),
            pl.BlockSpec((bm, d), row),
            pl.BlockSpec((1, d), fixed),
            pl.BlockSpec((1, d), fixed),
        ],
        out_specs=[pl.BlockSpec((bm, d), row), pl.BlockSpec((bm, d), row)],
        out_shape=[SDS((m, d), F32), SDS((m, d), BF16)],
        compiler_params=_cparams(1),
        name="hg_out_ln",
    )(o, h_all, norm_g.reshape(1, d), w, res, g.reshape(1, d), b.reshape(1, d))


def _ffn_step(f, x_ref, wgu_ref, wd_ref, g_ref, b_ref, y_ref, yb_ref):
    @pl.when(f == 0)
    def _():
        x = x_ref[...]
        yb_ref[...] = x.astype(BF16)
        y_ref[...] = ALPHA * x

    bf = wd_ref.shape[0]
    gu = _dot(yb_ref[...], wgu_ref[...])
    gate, up = gu[:, :bf], gu[:, bf:]
    h = (gate * jax.nn.sigmoid(gate) * up).astype(BF16)
    y_ref[...] += _dot(h, wd_ref[...])

    @pl.when(f == pl.num_programs(1) - 1)
    def _():
        g, b = g_ref[...], b_ref[...]
        chunk = math.gcd(y_ref.shape[0], LN_ROW_CHUNK)

        def ln_chunk(i, _):
            rows = pl.ds(pl.multiple_of(i * chunk, chunk), chunk)
            y = _layer_norm_rows(y_ref[rows, :], g, b)
            y_ref[rows, :] = y
            yb_ref[rows, :] = y.astype(BF16)
            return 0

        lax.fori_loop(0, y_ref.shape[0] // chunk, ln_chunk, 0)


def _ffn_kernel(x_ref, wgu_ref, wd_ref, g_ref, b_ref, y_ref, yb_ref):
    _ffn_step(pl.program_id(1), x_ref, wgu_ref.at[0], wd_ref, g_ref, b_ref, y_ref, yb_ref)


def ffn_ln(x, wgu, wd, g, b, *, bm):
    m, d = x.shape
    nf, _, bf2 = wgu.shape
    return pl.pallas_call(
        _ffn_kernel,
        grid=(m // bm, nf),
        in_specs=[
            pl.BlockSpec((bm, d), lambda i, f: (i, 0)),
            pl.BlockSpec((1, d, bf2), lambda i, f: (f, 0, 0)),
            pl.BlockSpec((bf2 // 2, d), lambda i, f: (f, 0)),
            pl.BlockSpec((1, d), lambda i, f: (0, 0)),
            pl.BlockSpec((1, d), lambda i, f: (0, 0)),
        ],
        out_specs=[pl.BlockSpec((bm, d), lambda i, f: (i, 0)), pl.BlockSpec((bm, d), lambda i, f: (i, 0))],
        out_shape=[SDS((m, d), F32), SDS((m, d), BF16)],
        compiler_params=_cparams(2, vmem=VMEM_LIMIT_MAX),
        name="ffn_ln",
    )(x, wgu, wd, g.reshape(1, d), b.reshape(1, d))


def _ffn_cast_kernel(x_ref, wg_ref, wu_ref, wd_ref, g_ref, b_ref, y_ref, yb_ref, wgub_ref, wdb_ref):
    bf = wd_ref.shape[1]
    wgub_ref[0, :, :bf] = wg_ref[0].astype(BF16)
    wgub_ref[0, :, bf:] = wu_ref[0].astype(BF16)
    wdb_ref[...] = wd_ref[0].astype(BF16)
    _ffn_step(pl.program_id(1), x_ref, wgub_ref.at[0], wdb_ref, g_ref, b_ref, y_ref, yb_ref)


def ffn_ln_cast(x, wg, wu, wd, layer, g, b, *, bf=256):
    m, d = x.shape
    dff = wg.shape[2]
    fixed = lambda i, f: (0, 0)
    return pl.pallas_call(
        _ffn_cast_kernel,
        grid=(1, dff // bf),
        in_specs=[
            pl.BlockSpec((m, d), fixed),
            pl.BlockSpec((1, d, bf), lambda i, f: (layer, 0, f)),
            pl.BlockSpec((1, d, bf), lambda i, f: (layer, 0, f)),
            pl.BlockSpec((1, bf, d), lambda i, f: (layer, f, 0)),
            pl.BlockSpec((1, d), fixed),
            pl.BlockSpec((1, d), fixed),
        ],
        out_specs=[
            pl.BlockSpec((m, d), fixed),
            pl.BlockSpec((m, d), fixed),
            pl.BlockSpec((1, d, 2 * bf), lambda i, f: (f, 0, 0)),
            pl.BlockSpec((bf, d), lambda i, f: (f, 0)),
        ],
        out_shape=[SDS((m, d), F32), SDS((m, d), BF16), SDS((dff // bf, d, 2 * bf), BF16), SDS((dff, d), BF16)],
        compiler_params=_cparams(2),
        name="ffn_ln_cast",
    )(x, wg, wu, wd, g.reshape(1, d), b.reshape(1, d))


def _cmul(ar, ai, br, bi):
    return ar * br - ai * bi, ar * bi + ai * br


def _s5_coefs(a_r, a_i, period):
    l = a_r.shape[1]
    pows = [(a_r, a_i)]
    for _ in range(period - 1):
        pows.append(_cmul(pows[-1][0], pows[-1][1], a_r, a_i))
    t = lax.broadcasted_iota(jnp.int32, (8, l), 0) & (period - 1)
    shifts = []
    k = 1
    while k < period:
        keep = t >= k
        shifts.append((jnp.where(keep, pows[k - 1][0], 0.0), jnp.where(keep, pows[k - 1][1], 0.0)))
        k *= 2
    p_r = jnp.broadcast_to(pows[0][0], (8, l))
    p_i = jnp.broadcast_to(pows[0][1], (8, l))
    for j in range(1, period):
        p_r = jnp.where(t == j, pows[j][0], p_r)
        p_i = jnp.where(t == j, pows[j][1], p_i)
    return shifts, (p_r, p_i)


def _s5_scan_tile(x_r, x_i, shifts):
    k = 1
    for c_r, c_i in shifts:
        s_r = pltpu.roll(x_r, k, 0)
        s_i = pltpu.roll(x_i, k, 0)
        x_r, x_i = x_r + c_r * s_r - c_i * s_i, x_i + c_r * s_i + c_i * s_r
        k *= 2
    return x_r, x_i


def _s5_prompt_kernel(u_ref, bw_ref, cw_ref, ar_ref, ai_ref, d_ref, z_ref, hfin_ref, bu_ref, carry_ref):
    tt = pl.program_id(2)
    n_rows = bu_ref.shape[0]

    @pl.when(tt == 0)
    def _():
        carry_ref[...] = jnp.zeros_like(carry_ref)

    bu_ref[...] = _dot(u_ref[0].astype(BF16), bw_ref[0])
    shifts, (p_r, p_i) = _s5_coefs(ar_ref[0], ai_ref[0], 8)

    ch = S5_SCAN_CHUNK
    d_skip = d_ref[0]

    def scan_chunk(c, carry):
        for i in range(ch // 8):
            h_r, h_i = carry
            r0 = pl.multiple_of(c * ch + i * 8, 8)
            x_r, x_i = _s5_scan_tile(bu_ref[pl.ds(r0, 8), 0:S5_HW], bu_ref[pl.ds(r0, 8), S5_HW:2 * S5_HW], shifts)
            x_r, x_i = x_r + p_r * h_r - p_i * h_i, x_i + p_r * h_i + p_i * h_r
            bu_ref[pl.ds(r0, 8), 0:S5_HW] = x_r
            bu_ref[pl.ds(r0, 8), S5_HW:2 * S5_HW] = x_i
            carry = (x_r[7:8], x_i[7:8])
        return carry

    def emit_chunk(c):
        rows = pl.ds(pl.multiple_of(c * ch, ch), ch)
        y = _dot(bu_ref[rows, :].astype(BF16), cw_ref[0]) + d_skip * u_ref[0, rows, :]
        z_ref[0, rows, :] = jax.nn.gelu(y).astype(z_ref.dtype)

    def body(c, carry):
        emit_chunk(c - 1)
        return scan_chunk(c, carry)

    carry = scan_chunk(0, (carry_ref[:, 0:S5_HW], carry_ref[:, S5_HW:2 * S5_HW]))
    h_r, h_i = lax.fori_loop(1, n_rows // ch, body, carry)
    emit_chunk(n_rows // ch - 1)
    carry_ref[:, 0:S5_HW] = h_r
    carry_ref[:, S5_HW:2 * S5_HW] = h_i

    @pl.when(tt == pl.num_programs(2) - 1)
    def _():
        hfin_ref[0, 0] = carry_ref[...]


def s5_prompt(h_all, bw, cw, a_r, a_i, d_skip, *, tb):
    bsz, t, _ = h_all.shape
    blk = lambda b, g, s: (g, 0, 0)
    return pl.pallas_call(
        _s5_prompt_kernel,
        grid=(bsz, S5_NGB, t // tb),
        in_specs=[
            pl.BlockSpec((1, tb, S5_UW), lambda b, g, s: (b, s, g)),
            pl.BlockSpec((1, S5_UW, 2 * S5_HW), blk),
            pl.BlockSpec((1, 2 * S5_HW, S5_UW), blk),
            pl.BlockSpec((1, 1, S5_HW), blk),
            pl.BlockSpec((1, 1, S5_HW), blk),
            pl.BlockSpec((1, 1, S5_UW), blk),
        ],
        out_specs=[
            pl.BlockSpec((1, tb, S5_UW), lambda b, g, s: (b, s, g)),
            pl.BlockSpec((1, 1, 1, 2 * S5_HW), lambda b, g, s: (b, g, 0, 0)),
        ],
        out_shape=[SDS((bsz, t, D_S5), BF16), SDS((bsz, S5_NGB, 1, 2 * S5_HW), F32)],
        scratch_shapes=[pltpu.VMEM((tb, 2 * S5_HW), F32), pltpu.VMEM((1, 2 * S5_HW), F32)],
        compiler_params=_cparams(3),
        name="s5_prompt",
    )(h_all, bw, cw, a_r, a_i, d_skip)


def _s5_sample_kernel(u_ref, h0_ref, bw_ref, cw_ref, ar_ref, ai_ref, d_ref, z_ref, h_ref, *, period):
    u = u_ref[...]
    bu = _dot(u.astype(BF16), bw_ref[0])
    shifts, (p_r, p_i) = _s5_coefs(ar_ref[0], ai_ref[0], period)
    for i in range(u.shape[0] // 8):
        rows = slice(i * 8, (i + 1) * 8)
        x_r, x_i = _s5_scan_tile(bu[rows, 0:S5_HW], bu[rows, S5_HW:2 * S5_HW], shifts)
        h0_r = h0_ref[rows, 0:S5_HW]
        h0_i = h0_ref[rows, S5_HW:2 * S5_HW]
        h_ref[rows, 0:S5_HW] = x_r + p_r * h0_r - p_i * h0_i
        h_ref[rows, S5_HW:2 * S5_HW] = x_i + p_r * h0_i + p_i * h0_r
    y = _dot(h_ref[...].astype(BF16), cw_ref[0]) + d_ref[0] * u
    z_ref[...] = jax.nn.gelu(y).astype(z_ref.dtype)


def s5_sample(h_all, h0x, bw, cw, a_r, a_i, d_skip, *, period):
    m = h_all.shape[0]
    blk = lambda g: (g, 0, 0)
    return pl.pallas_call(
        functools.partial(_s5_sample_kernel, period=period),
        grid=(S5_NGB,),
        in_specs=[
            pl.BlockSpec((m, S5_UW), lambda g: (0, g)),
            pl.BlockSpec((m, 2 * S5_HW), lambda g: (0, g)),
            pl.BlockSpec((1, S5_UW, 2 * S5_HW), blk),
            pl.BlockSpec((1, 2 * S5_HW, S5_UW), blk),
            pl.BlockSpec((1, 1, S5_HW), blk),
            pl.BlockSpec((1, 1, S5_HW), blk),
            pl.BlockSpec((1, 1, S5_UW), blk),
        ],
        out_specs=[pl.BlockSpec((m, S5_UW), lambda g: (0, g)), pl.BlockSpec((m, 2 * S5_HW), lambda g: (0, g))],
        out_shape=[SDS((m, D_S5), BF16), SDS((m, S5_NGB * 2 * S5_HW), F32)],
        compiler_params=_cparams(1),
        name="s5_sample",
    )(h_all, h0x, bw, cw, a_r, a_i, d_skip)


def _bias_kernel(rb_ref, o_ref):
    h = pl.program_id(0)
    q = lax.broadcasted_iota(jnp.int32, (WINDOW, 2 * WINDOW), 0)
    k = lax.broadcasted_iota(jnp.int32, (WINDOW, 2 * WINDOW), 1)
    n = jnp.maximum(q + WINDOW - k, 0)
    max_exact = N_BUCKETS // 2
    nf = jnp.maximum(n, 1).astype(F32)
    large = max_exact + (jnp.log(nf / max_exact) / math.log(MAX_DISTANCE / max_exact)
                         * (N_BUCKETS - max_exact)).astype(jnp.int32)
    large = jnp.minimum(large, N_BUCKETS - 1)
    bucket = jnp.where(n < max_exact, n, large)
    out = jnp.zeros((WINDOW, 2 * WINDOW), F32)
    for b in range(N_BUCKETS):
        out = jnp.where(bucket == b, rb_ref[b, h], out)
    o_ref[0] = out


def bias_table(rel_bias):
    return pl.pallas_call(
        _bias_kernel,
        grid=(SWA_HEADS,),
        in_specs=[pl.BlockSpec(memory_space=pltpu.SMEM)],
        out_specs=pl.BlockSpec((1, WINDOW, 2 * WINDOW), lambda h: (h, 0, 0)),
        out_shape=SDS((SWA_HEADS, WINDOW, 2 * WINDOW), F32),
        compiler_params=_cparams(1),
        name="bias_table",
    )(rel_bias)


def _swa_softmax(s, valid, sink_col):
    s = jnp.where(valid, s, NEG_BIG)
    m = jnp.maximum(jnp.max(s, axis=-1, keepdims=True), sink_col)
    p = jnp.exp(s - m)
    return p / (jnp.sum(p, axis=-1, keepdims=True) + jnp.exp(sink_col - m))


def _sink_col(sinks_ref, first_head, rows_per_head):
    n = SWA_GQ * rows_per_head
    r = lax.broadcasted_iota(jnp.int32, (n, 1), 0)
    col = jnp.full((n, 1), sinks_ref[first_head], F32)
    for g in range(1, SWA_GQ):
        col = jnp.where(r >= g * rows_per_head, sinks_ref[first_head + g], col)
    return col


SWA_KV_PER_STEP = 4


def _swa_prompt_kernel(sinks_ref, q_ref, kp_ref, kc_ref, vp_ref, vc_ref, bias_ref, o_ref):
    nkv = SWA_KV_PER_STEP
    part = pl.program_id(0)
    i = pl.program_id(2)
    blk = WINDOW
    q = q_ref[0].astype(BF16)
    kk = jnp.concatenate([kp_ref[0], kc_ref[0]], axis=0).astype(BF16)
    vv = jnp.concatenate([vp_ref[0], vc_ref[0]], axis=0).astype(BF16)
    qi = lax.broadcasted_iota(jnp.int32, (SWA_GQ * blk, 2 * blk), 0) & (blk - 1)
    ki = lax.broadcasted_iota(jnp.int32, (SWA_GQ * blk, 2 * blk), 1)
    valid = (ki > qi) & (ki <= qi + blk) & ((ki >= blk) | (i > 0))
    heads = [slice(j * HEAD_DIM, (j + 1) * HEAD_DIM) for j in range(nkv)]
    q4 = [jnp.concatenate([q[:, (j * SWA_GQ + g) * HEAD_DIM:(j * SWA_GQ + g + 1) * HEAD_DIM]
                           for g in range(SWA_GQ)], axis=0) for j in range(nkv)]
    s = [_dot_nt(q4[j], kk[:, c]) * (HEAD_DIM ** -0.5) for j, c in enumerate(heads)]
    p = [_swa_softmax(s[j] + bias_ref[j * SWA_GQ:(j + 1) * SWA_GQ].reshape(SWA_GQ * blk, 2 * blk), valid,
                      _sink_col(sinks_ref, (part * nkv + j) * SWA_GQ, blk)).astype(BF16) for j in range(nkv)]
    o4 = [_dot(p[j], vv[:, c]) for j, c in enumerate(heads)]
    o_ref[0] = jnp.concatenate([o4[j][g * blk:(g + 1) * blk] for j in range(nkv) for g in range(SWA_GQ)],
                               axis=1).astype(o_ref.dtype)


def swa_prompt(h_all, sinks, table):
    bsz, t, _ = h_all.shape
    nblk = t // WINDOW
    nkv = SWA_KV_PER_STEP
    qw = nkv * SWA_GQ * HEAD_DIM
    kw = nkv * HEAD_DIM
    q0, k0, v0 = D_S5 // qw, (D_S5 + D_SWA_Q) // kw, (D_S5 + D_SWA_Q + D_SWA_KV) // kw
    cur = lambda c0: (lambda p, b, i: (b, i, c0 + p))
    prev = lambda c0: (lambda p, b, i: (b, jnp.maximum(i - 1, 0), c0 + p))
    return pl.pallas_call(
        _swa_prompt_kernel,
        grid=(SWA_KV_HEADS // nkv, bsz, nblk),
        in_specs=[
            pl.BlockSpec(memory_space=pltpu.SMEM),
            pl.BlockSpec((1, WINDOW, qw), cur(q0)),
            pl.BlockSpec((1, WINDOW, kw), prev(k0)),
            pl.BlockSpec((1, WINDOW, kw), cur(k0)),
            pl.BlockSpec((1, WINDOW, kw), prev(v0)),
            pl.BlockSpec((1, WINDOW, kw), cur(v0)),
            pl.BlockSpec((nkv * SWA_GQ, WINDOW, 2 * WINDOW), lambda p, b, i: (p, 0, 0)),
        ],
        out_specs=pl.BlockSpec((1, WINDOW, qw), lambda p, b, i: (b, i, p)),
        out_shape=SDS((bsz, t, D_SWA_Q), BF16),
        compiler_params=_cparams(3),
        name="swa_prompt",
    )(sinks, h_all, h_all, h_all, h_all, h_all, table)


def _swa_sample_kernel(sinks_ref, q_ref, kk_ref, vv_ref, bias_ref, o_ref, *, t_new):
    rows = q_ref.shape[0]
    n_keys = kk_ref.shape[1]
    q = q_ref[...].astype(BF16)
    r = lax.broadcasted_iota(jnp.int32, (SWA_GQ * rows, n_keys), 0)
    ti = r & (t_new - 1)
    ki = lax.broadcasted_iota(jnp.int32, (SWA_GQ * rows, n_keys), 1)
    valid = (ki > ti) & (ki <= ti + WINDOW)
    second = (r & (rows - 1)) >= t_new
    second_o = (lax.broadcasted_iota(jnp.int32, (SWA_GQ * rows, HEAD_DIM), 0) & (rows - 1)) >= t_new
    nkv = SWA_KV_HEADS
    heads = [slice(j * HEAD_DIM, (j + 1) * HEAD_DIM) for j in range(nkv)]
    kk = [kk_ref[bb].astype(BF16) for bb in range(2)]
    vv = [vv_ref[bb].astype(BF16) for bb in range(2)]
    q4 = [jnp.concatenate([q[:, (j * SWA_GQ + g) * HEAD_DIM:(j * SWA_GQ + g + 1) * HEAD_DIM]
                           for g in range(SWA_GQ)], axis=0) for j in range(nkv)]
    s_seq = [[_dot_nt(q4[j], kk[bb][:, c]) for j, c in enumerate(heads)] for bb in range(2)]
    p = []
    for j in range(nkv):
        s = jnp.where(second, s_seq[1][j], s_seq[0][j]) * (HEAD_DIM ** -0.5)
        s = s + bias_ref[j * SWA_GQ:(j + 1) * SWA_GQ].reshape(SWA_GQ * rows, n_keys)
        p.append(_swa_softmax(s, valid, _sink_col(sinks_ref, j * SWA_GQ, rows)).astype(BF16))
    o_seq = [[_dot(p[j], vv[bb][:, c]) for j, c in enumerate(heads)] for bb in range(2)]
    o4 = [jnp.where(second_o, o_seq[1][j], o_seq[0][j]) for j in range(nkv)]
    o_ref[...] = jnp.concatenate([o4[j][g * rows:(g + 1) * rows] for j in range(nkv) for g in range(SWA_GQ)],
                                 axis=1).astype(o_ref.dtype)


def swa_sample(h_all, kk, vv, sinks, bias_s, *, t_new):
    m = h_all.shape[0]
    n_keys = kk.shape[1]
    rows = 2 * t_new
    return pl.pallas_call(
        functools.partial(_swa_sample_kernel, t_new=t_new),
        grid=(m // rows,),
        in_specs=[
            pl.BlockSpec(memory_space=pltpu.SMEM),
            pl.BlockSpec((rows, D_SWA_Q), lambda i: (i, D_S5 // D_SWA_Q)),
            pl.BlockSpec((2, n_keys, D_SWA_KV), lambda i: (i, 0, 0)),
            pl.BlockSpec((2, n_keys, D_SWA_KV), lambda i: (i, 0, 0)),
            pl.BlockSpec((SWA_HEADS, rows, n_keys), lambda i: (0, 0, 0)),
        ],
        out_specs=pl.BlockSpec((rows, D_SWA_Q), lambda i: (i, 0)),
        out_shape=SDS((m, D_SWA_Q), BF16),
        compiler_params=_cparams(1),
        name="swa_sample",
    )(sinks, h_all, kk, vv, bias_s)


def _hg_gates(q, fz, lb):
    qs = q * jax.nn.sigmoid(q)
    f = lb + (1.0 - lb) * jax.nn.sigmoid(fz)
    return qs, jnp.log(f), 1.0 - f


HG_GROUP = 4


def _hgrn_prompt_kernel(q_ref, f_ref, v_ref, lb_ref, o_ref, st_ref, s_scr, *, n_heads):
    tt = pl.program_id(2)
    n_rows = q_ref.shape[1]
    c, ng = HG_CHUNK, HG_GROUP
    gr = c * ng

    @pl.when(tt == 0)
    def _():
        s_scr[...] = jnp.zeros_like(s_scr)

    ri = lax.broadcasted_iota(jnp.int32, (gr, gr), 0)
    ci = lax.broadcasted_iota(jnp.int32, (gr, gr), 1)
    causal = (ri >= ci) & ((ri // c) == (ci // c))
    t_in = lax.broadcasted_iota(jnp.int32, (gr, HG_DK), 0) & (c - 1)
    own = ((lax.broadcasted_iota(jnp.int32, (gr, ng * HG_DK), 0) // c)
           == (lax.broadcasted_iota(jnp.int32, (gr, ng * HG_DK), 1) // HG_DK)).astype(BF16)

    def body(gi, _):
        r0 = pl.multiple_of(gi * gr, gr)
        rows = pl.ds(r0, gr)
        hs = range(n_heads)
        heads = [slice(hd * HG_DK, (hd + 1) * HG_DK) for hd in hs]
        v, qt, kt, kl, decay = [], [], [], [], []
        for cols in heads:
            qs, lf, k = _hg_gates(q_ref[0, rows, cols], f_ref[0, rows, cols], lb_ref[:, cols])
            v.append(v_ref[0, rows, cols].astype(BF16))
            cum = lf
            sh = 1
            while sh < c:
                cum = cum + jnp.where(t_in >= sh, pltpu.roll(cum, sh, 0), 0.0)
                sh *= 2
            last3 = cum.reshape(ng, c, HG_DK)[:, c - 1:c, :]
            last = jnp.broadcast_to(last3, (ng, c, HG_DK)).reshape(gr, HG_DK)
            qt.append((qs * jnp.exp(cum)).astype(BF16))
            kt.append((k * jnp.exp(-cum)).astype(BF16))
            kl.append((k * jnp.exp(last - cum)).astype(BF16))
            decay.append(jnp.exp(last3))
        kv = [_dot_tn(v[h], jnp.concatenate([kl[h]] * ng, axis=1) * own) for h in hs]
        attn = [jnp.where(causal, _dot_nt(qt[h], kt[h]), 0.0).astype(BF16) for h in hs]
        s_cat = []
        for h in hs:
            st = s_scr[h]
            starts = []
            for j in range(ng):
                starts.append(st)
                st = st * decay[h][j] + kv[h][:, j * HG_DK:(j + 1) * HG_DK]
            s_scr[h] = st
            s_cat.append(jnp.concatenate(starts, axis=1).astype(BF16))
        intra = [_dot(attn[h], v[h]) for h in hs]
        inter = [_dot_nt(jnp.concatenate([qt[h]] * ng, axis=1) * own, s_cat[h]) for h in hs]
        for h, cols in enumerate(heads):
            o_ref[0, rows, cols] = intra[h] + inter[h]
        return 0

    lax.fori_loop(0, n_rows // gr, body, 0, unroll=True)

    @pl.when(tt == pl.num_programs(2) - 1)
    def _():
        st_ref[0] = s_scr[...]


def hgrn_prompt(h_all, lb, *, tb, n_heads=2):
    bsz, t, _ = h_all.shape
    hw = n_heads * HG_DK
    nhb = D_MODEL // hw
    col = lambda c0: (lambda b, h, s: (b, s, c0 * nhb + h))
    return pl.pallas_call(
        functools.partial(_hgrn_prompt_kernel, n_heads=n_heads),
        grid=(bsz, nhb, t // tb),
        in_specs=[
            pl.BlockSpec((1, tb, hw), col(0)),
            pl.BlockSpec((1, tb, hw), col(1)),
            pl.BlockSpec((1, tb, hw), col(2)),
            pl.BlockSpec((1, hw), lambda b, h, s: (0, h)),
        ],
        out_specs=[
            pl.BlockSpec((1, tb, hw), lambda b, h, s: (b, s, h)),
            pl.BlockSpec((1, n_heads, HG_DK, HG_DK), lambda b, h, s: (b, h, 0, 0)),
        ],
        out_shape=[SDS((bsz, t, D_MODEL), F32), SDS((bsz, HG_HEADS, HG_DK, HG_DK), F32)],
        scratch_shapes=[pltpu.VMEM((n_heads, HG_DK, HG_DK), F32)],
        compiler_params=_cparams(3),
        name="hgrn_prompt",
    )(h_all, h_all, h_all, lb.reshape(1, D_MODEL))


def _hgrn_sample_kernel(q_ref, f_ref, v_ref, lb_ref, s0_ref, o_ref, s_ref, *, n_heads, t_new):
    rows = q_ref.shape[0]
    ri = lax.broadcasted_iota(jnp.int32, (rows, rows), 0)
    ci = lax.broadcasted_iota(jnp.int32, (rows, rows), 1)
    causal = (ri >= ci) & ((ri >= t_new) == (ci >= t_new))
    tril = causal.astype(F32)
    hw = n_heads * HG_DK
    second_w = lax.broadcasted_iota(jnp.int32, (rows, hw), 0) >= t_new
    second = second_w[:, :HG_DK]
    eye = (lax.broadcasted_iota(jnp.int32, (HG_DK, HG_DK), 0)
           == lax.broadcasted_iota(jnp.int32, (HG_DK, HG_DK), 1))
    heads = [slice(hd * HG_DK, (hd + 1) * HG_DK) for hd in range(n_heads)]
    qs, lf, k = _hg_gates(q_ref[...], f_ref[...], lb_ref[...])
    v = v_ref[...].astype(BF16)
    cum = jnp.dot(tril, lf, preferred_element_type=F32, precision=lax.Precision.HIGHEST)
    last0 = cum[t_new - 1:t_new]
    last1 = cum[rows - 1:rows]
    qt = (qs * jnp.exp(cum)).astype(BF16)
    kt = (k * jnp.exp(-cum)).astype(BF16)
    kl = k * jnp.exp(jnp.where(second_w, last1, last0) - cum)
    kl_seq = (jnp.where(second_w, 0.0, kl).astype(BF16), jnp.where(second_w, kl, 0.0).astype(BF16))
    decay_seq = (jnp.exp(last0), jnp.exp(last1))
    attn = [jnp.where(causal, _dot_nt(qt[:, c], kt[:, c]), 0.0).astype(BF16) for c in heads]
    inter = [[_dot(qt[:, c], s0_ref[bb, hd].astype(BF16)) for hd, c in enumerate(heads)] for bb in range(2)]
    intra = [_dot(a, v[:, c]) for a, c in zip(attn, heads)]
    for hd, c in enumerate(heads):
        o_ref[:, c] = intra[hd] + jnp.where(second, inter[1][hd], inter[0][hd])
    for bb in range(2):
        kv = [_dot_tn(kl_seq[bb][:, c], v[:, c]) for c in heads]
        for hd, c in enumerate(heads):
            decay = jnp.sum(jnp.where(eye, decay_seq[bb][:, c], 0.0), axis=1, keepdims=True)
            s_ref[bb, hd] = decay * s0_ref[bb, hd] + kv[hd]


def hgrn_sample(h_all, lb, s0, *, t_new, n_heads=8):
    m = h_all.shape[0]
    rows = 2 * t_new
    hw = n_heads * HG_DK
    nhb = D_MODEL // hw
    col = lambda c0: (lambda i, h: (i, c0 * nhb + h))
    return pl.pallas_call(
        functools.partial(_hgrn_sample_kernel, n_heads=n_heads, t_new=t_new),
        grid=(m // rows, nhb),
        in_specs=[
            pl.BlockSpec((rows, hw), col(0)),
            pl.BlockSpec((rows, hw), col(1)),
            pl.BlockSpec((rows, hw), col(2)),
            pl.BlockSpec((1, hw), lambda i, h: (0, h)),
            pl.BlockSpec((2, n_heads, HG_DK, HG_DK), lambda i, h: (i, h, 0, 0)),
        ],
        out_specs=[
            pl.BlockSpec((rows, hw), lambda i, h: (i, h)),
            pl.BlockSpec((2, n_heads, HG_DK, HG_DK), lambda i, h: (i, h, 0, 0)),
        ],
        out_shape=[SDS((m, D_MODEL), F32), SDS(s0.shape, F32)],
        compiler_params=_cparams(2),
        name="hgrn_sample",
    )(h_all, h_all, h_all, lb.reshape(1, D_MODEL), s0)


def _xattn_kernel(y_ref, yb_ref, wq_ref, mk_ref, mv_ref, wo_ref, g_ref, b_ref, o_ref, ob_ref, *, n_seq):
    rows = y_ref.shape[0]
    q = _dot(yb_ref[...], wq_ref[...]).astype(BF16)
    second = lax.broadcasted_iota(jnp.int32, (rows, 1), 0) >= rows // 2

    def mem_head(ref, seq, h):
        if len(ref.shape) == 3:
            return ref[seq, :, h * MEM_HEAD_DIM:(h + 1) * MEM_HEAD_DIM].astype(BF16)
        return ref[0, seq, :, h, :].astype(BF16)

    hs = range(MEM_HEADS)
    qh = [q[:, h * MEM_HEAD_DIM:(h + 1) * MEM_HEAD_DIM] for h in hs]
    s_seq = [[_dot_nt(qh[h], mem_head(mk_ref, seq, h)) for h in hs] for seq in range(n_seq)]
    p = []
    for h in hs:
        s = s_seq[0][h] if n_seq == 1 else jnp.where(second, s_seq[1][h], s_seq[0][h])
        s = s * (MEM_HEAD_DIM ** -0.5)
        e = jnp.exp(s - jnp.max(s, axis=-1, keepdims=True))
        p.append((e / jnp.sum(e, axis=-1, keepdims=True)).astype(BF16))
    o_seq = [[_dot(p[h], mem_head(mv_ref, seq, h)) for h in hs] for seq in range(n_seq)]
    heads = [(o_seq[0][h] if n_seq == 1 else jnp.where(second, o_seq[1][h], o_seq[0][h])).astype(BF16) for h in hs]
    _proj_res_ln([jnp.concatenate(heads, axis=1)], wo_ref, y_ref, g_ref, b_ref, o_ref, ob_ref)


def xattn_ln(y, yb, wq, mk, mv, wo, g, b, *, bm, rows_per_seq, layer=None):
    m, d = y.shape
    n_seq = 2 if bm == 2 * rows_per_seq else 1
    assert n_seq == 2 or rows_per_seq % bm == 0
    row = lambda i: (i, 0)
    fixed = lambda i: (0, 0)
    if layer is None:
        mem = (lambda i: (i, 0, 0)) if n_seq == 2 else (lambda i: (i * bm // rows_per_seq, 0, 0))
        mem_spec = pl.BlockSpec((n_seq, N_MEM, D_MEM), mem)
    else:
        assert n_seq == 2
        mem_spec = pl.BlockSpec((1, n_seq, N_MEM, MEM_HEADS, MEM_HEAD_DIM), lambda i: (layer, i, 0, 0, 0))
    return pl.pallas_call(
        functools.partial(_xattn_kernel, n_seq=n_seq),
        grid=(m // bm,),
        in_specs=[
            pl.BlockSpec((bm, d), row),
            pl.BlockSpec((bm, d), row),
            pl.BlockSpec((d, D_MEM), fixed),
            mem_spec,
            mem_spec,
            pl.BlockSpec((D_MEM, d), fixed),
            pl.BlockSpec((1, d), fixed),
            pl.BlockSpec((1, d), fixed),
        ],
        out_specs=[pl.BlockSpec((bm, d), row), pl.BlockSpec((bm, d), row)],
        out_shape=[SDS((m, d), F32), SDS((m, d), BF16)],
        compiler_params=_cparams(1),
        name="xattn_ln",
    )(y, yb, wq, mk, mv, wo, g.reshape(1, d), b.reshape(1, d))


def _s5_discretize(lam_re, lam_im, log_dt, b_re, b_im):
    lr = jnp.minimum(lam_re.astype(F32), -1e-4)
    li = lam_im.astype(F32)
    dt = jnp.exp(log_dt.astype(F32))[:, None]
    mag = jnp.exp(lr * dt)
    a_re = mag * jnp.cos(li * dt)
    a_im = mag * jnp.sin(li * dt)
    den = lr * lr + li * li
    fr = ((a_re - 1.0) * lr + a_im * li) / den
    fi = (a_im * lr - (a_re - 1.0) * li) / den
    br, bi = b_re.astype(F32), b_im.astype(F32)
    bb_re = fr[..., None] * br - fi[..., None] * bi
    bb_im = fr[..., None] * bi + fi[..., None] * br
    return a_re, a_im, bb_re, bb_im


def _s5_block_weights(a_re, a_im, bb_re, bb_im, c_re, c_im, d_skip):
    eye = jnp.eye(S5_GB, dtype=F32)
    shp_b = (S5_NGB, S5_GB, S5_STATE, S5_GROUP)
    shp_c = (S5_NGB, S5_GB, S5_GROUP, S5_STATE)
    blk_b = lambda w: jnp.einsum('bgph,gk->bghkp', w.reshape(shp_b), eye).reshape(S5_NGB, S5_UW, S5_HW)
    blk_c = lambda w: jnp.einsum('bghp,gk->bkpgh', w.reshape(shp_c), eye).reshape(S5_NGB, S5_HW, S5_UW)
    bw = jnp.concatenate([blk_b(bb_re), blk_b(bb_im)], axis=-1).astype(BF16)
    cw = jnp.concatenate([blk_c(c_re.astype(F32)), blk_c(-c_im.astype(F32))], axis=1).astype(BF16)
    return (bw, cw, a_re.reshape(S5_NGB, 1, S5_HW), a_im.reshape(S5_NGB, 1, S5_HW),
            d_skip.astype(F32).reshape(S5_NGB, 1, S5_UW))


def _s5_state_to_blocks(s_re, s_im):
    b = s_re.shape[0]
    return jnp.concatenate([s_re.reshape(b, S5_NGB, S5_HW), s_im.reshape(b, S5_NGB, S5_HW)],
                           axis=-1).reshape(b, S5_NGB * 2 * S5_HW)


def _s5_state_from_blocks(h):
    b = h.shape[0]
    return (h[:, :, :S5_HW].reshape(b, S5_GROUPS, S5_STATE), h[:, :, S5_HW:].reshape(b, S5_GROUPS, S5_STATE))


def kernel(x_prompt, x_sample, cache_mem_k, cache_mem_v, cache_swa_k, cache_swa_v, state_s5_re, state_s5_im, state_hgrn, mem_prompt, rel_bias, w_even_in, s5_lam_re, s5_lam_im, s5_log_dt, s5_b_re, s5_b_im, s5_c_re, s5_c_im, s5_d, s5_w_glu, swa_sinks, w_even_out, hg_lb_logits, w_odd_in, hg_norm_g, w_odd_out, w_mem_q, w_mem_k, w_mem_v, w_mem_o, w_ffn_gate, w_ffn_up, w_ffn_down, ln_g, ln_b):
    bsz, seq, d = x_prompt.shape
    dec_b, dec_t, _ = x_sample.shape
    mp, ms = bsz * seq, dec_b * dec_t
    w_buf = cache_swa_k.shape[2]
    bf = lambda w: w.astype(BF16)

    lb_soft = jax.nn.softmax(hg_lb_logits.astype(F32), axis=0)
    lower_bounds = jnp.cumsum(lb_soft, axis=0) - lb_soft[0]
    table = bias_table(rel_bias.astype(F32))
    bias_s = jnp.tile(table[:, :dec_t], (1, 2, 1))
    mem_b = bf(mem_prompt).reshape(bsz * N_MEM, d)

    yp = x_prompt.reshape(mp, d)
    ys = x_sample.reshape(ms, d)
    ypb, ysb = bf(yp), bf(ys)
    p_mem_k, p_mem_v = [], []
    p_swa_k, p_swa_v, p_s5_re, p_s5_im, p_hg = [], [], [], [], []
    s_swa_k, s_swa_v, s_s5_re, s_s5_im, s_hg = [], [], [], [], []

    for l in range(DEPTH):
        j = l // 2
        g0, b0 = ln_g[l, 0], ln_b[l, 0]
        if l % 2 == 0:
            a_re, a_im, bb_re, bb_im = _s5_discretize(s5_lam_re[j], s5_lam_im[j], s5_log_dt[j], s5_b_re[j], s5_b_im[j])
            s5w = _s5_block_weights(a_re, a_im, bb_re, bb_im, s5_c_re[j], s5_c_im[j], s5_d[j])
            w_glu, w_out = cast_layer(s5_w_glu, j), cast_layer(w_even_out, j)
            sinks = swa_sinks[j].astype(F32)
            kv0 = D_S5 + D_SWA_Q

            hs, w_in = matmul_cast(ysb, w_even_in, j, bn=512)
            hp = matmul(ypb, w_in, bm=1024, bn=1024).reshape(bsz, seq, -1)
            z, hfin = s5_prompt(hp, *s5w, tb=512)
            s5_out = glu(z.reshape(mp, D_S5), w_glu, bm=1024, bn=1024)
            att = swa_prompt(hp, sinks, table).reshape(mp, D_SWA_Q)
            hr, hi = _s5_state_from_blocks(hfin[:, :, 0])
            p_s5_re.append(hr); p_s5_im.append(hi)
            p_swa_k.append(hp[:, seq - w_buf:, kv0:kv0 + D_SWA_KV].reshape(bsz, w_buf, SWA_KV_HEADS, HEAD_DIM))
            p_swa_v.append(hp[:, seq - w_buf:, kv0 + D_SWA_KV:].reshape(bsz, w_buf, SWA_KV_HEADS, HEAD_DIM))
            yp, ypb = proj_res_ln([s5_out, att], w_out, yp, g0, b0, bm=128)

            h0x = jnp.repeat(_s5_state_to_blocks(state_s5_re[j].astype(F32), state_s5_im[j].astype(F32)), dec_t, axis=0)
            z, h_steps = s5_sample(hs, h0x, *s5w, period=dec_t)
            s5_out = glu(z, w_glu, bm=ms, bn=1024)
            hs3 = hs.reshape(dec_b, dec_t, -1)
            kpad = jnp.zeros((dec_b, 2 * WINDOW - w_buf - dec_t, D_SWA_KV), F32)
            kk = jnp.concatenate([cache_swa_k[j].reshape(dec_b, w_buf, D_SWA_KV).astype(F32),
                                  hs3[:, :, kv0:kv0 + D_SWA_KV], kpad], axis=1)
            vv = jnp.concatenate([cache_swa_v[j].reshape(dec_b, w_buf, D_SWA_KV).astype(F32),
                                  hs3[:, :, kv0 + D_SWA_KV:], kpad], axis=1)
            att = swa_sample(hs, kk, vv, sinks, bias_s, t_new=dec_t)
            hr, hi = _s5_state_from_blocks(h_steps.reshape(dec_b, dec_t, S5_NGB, 2 * S5_HW)[:, dec_t - 1])
            s_s5_re.append(hr); s_s5_im.append(hi)
            s_swa_k.append(kk[:, dec_t:dec_t + w_buf].reshape(dec_b, w_buf, SWA_KV_HEADS, HEAD_DIM))
            s_swa_v.append(vv[:, dec_t:dec_t + w_buf].reshape(dec_b, w_buf, SWA_KV_HEADS, HEAD_DIM))
            ys, ysb = proj_res_ln([s5_out, att], w_out, ys, g0, b0, bm=ms)
        else:
            w_out = cast_layer(w_odd_out, j)
            lb = lower_bounds[l]

            hs, w_in = matmul_cast(ysb, w_odd_in, j, bn=512)
            hp = matmul(ypb, w_in, bm=1024, bn=1024)
            o, st = hgrn_prompt(hp.reshape(bsz, seq, -1), lb, tb=512, n_heads=4)
            p_hg.append(jnp.swapaxes(st, -1, -2))
            yp, ypb = hg_out_ln(o.reshape(mp, d), hp, hg_norm_g[j], w_out, yp, g0, b0, bm=128)

            o, s_new = hgrn_sample(hs, lb, state_hgrn[j].astype(F32), t_new=dec_t)
            s_hg.append(s_new)
            ys, ysb = hg_out_ln(o, hs, hg_norm_g[j], w_out, ys, g0, b0, bm=ms)

        wq, wk, wv, wo = (cast_layer(w, l) for w in (w_mem_q, w_mem_k, w_mem_v, w_mem_o))
        mk = matmul(mem_b, wk, bm=bsz * N_MEM, bn=D_MEM).reshape(bsz, N_MEM, D_MEM)
        mv = matmul(mem_b, wv, bm=bsz * N_MEM, bn=D_MEM).reshape(bsz, N_MEM, D_MEM)
        p_mem_k.append(mk.reshape(bsz, N_MEM, MEM_HEADS, MEM_HEAD_DIM))
        p_mem_v.append(mv.reshape(bsz, N_MEM, MEM_HEADS, MEM_HEAD_DIM))
        yp, ypb = xattn_ln(yp, ypb, wq, mk, mv, wo, ln_g[l, 1], ln_b[l, 1], bm=256, rows_per_seq=seq)
        ys, ysb = xattn_ln(ys, ysb, wq, cache_mem_k, cache_mem_v, wo, ln_g[l, 1], ln_b[l, 1],
                           bm=2 * dec_t, rows_per_seq=dec_t, layer=l)

        ys, ysb, wgu, wd = ffn_ln_cast(ys, w_ffn_gate, w_ffn_up, w_ffn_down, l, ln_g[l, 2], ln_b[l, 2])
        yp, ypb = ffn_ln(yp, wgu, wd, ln_g[l, 2], ln_b[l, 2], bm=512)

    return (yp.reshape(bsz, seq, d), ys.reshape(dec_b, dec_t, d),
            jnp.stack(p_mem_k), jnp.stack(p_mem_v),
            jnp.stack(p_swa_k), jnp.stack(p_swa_v),
            jnp.stack(p_s5_re), jnp.stack(p_s5_im), jnp.stack(p_hg),
            jnp.stack(s_swa_k), jnp.stack(s_swa_v),
            jnp.stack(s_s5_re), jnp.stack(s_s5_im), jnp.stack(s_hg))
```

```python
import functools
import math

import jax
import jax.numpy as jnp
from jax import lax
from jax.experimental import pallas as pl
from jax.experimental.pallas import tpu as pltpu

F32 = jnp.float32
BF16 = jnp.bfloat16
SDS = jax.ShapeDtypeStruct

D_MODEL = 4096
DEPTH = 2
ALPHA = (2 * DEPTH) ** 0.25
LN_EPS = 1e-5
RMS_EPS = 1e-6
NEG_BIG = -1e30

S5_GROUPS, S5_GROUP, S5_STATE = 128, 16, 64
D_S5 = S5_GROUPS * S5_GROUP
S5_GB = 16
S5_NGB = S5_GROUPS // S5_GB
S5_UW = S5_GB * S5_GROUP
S5_HW = S5_GB * S5_STATE
S5_SCAN_CHUNK = 128

HEAD_DIM, SWA_HEADS, SWA_KV_HEADS, SWA_GQ = 64, 32, 8, 4
D_SWA_Q, D_SWA_KV = 2048, 512
WINDOW = 128
N_BUCKETS, MAX_DISTANCE = 32, 128

HG_DK, HG_HEADS, HG_CHUNK = 128, 32, 32
N_MEM, MEM_HEADS, MEM_HEAD_DIM, D_MEM = 256, 4, 128, 512

VMEM_LIMIT = 56 * 1024 * 1024
VMEM_LIMIT_MAX = 60 * 1024 * 1024
PROJ_COL_CHUNKS = 4
LN_ROW_CHUNK = 64


def _cparams(n_axes, vmem=VMEM_LIMIT):
    return pltpu.CompilerParams(dimension_semantics=("arbitrary",) * n_axes, vmem_limit_bytes=vmem)


def _dot(a, b):
    return jnp.dot(a, b, preferred_element_type=F32)


def _dot_nt(a, b):
    return lax.dot_general(a, b, (((1,), (1,)), ((), ())), preferred_element_type=F32)


def _dot_tn(a, b):
    return lax.dot_general(a, b, (((0,), (0,)), ((), ())), preferred_element_type=F32)


def _layer_norm_rows(s, g, b):
    mu = jnp.mean(s, axis=-1, keepdims=True)
    c = s - mu
    var = jnp.mean(c * c, axis=-1, keepdims=True)
    return c * lax.rsqrt(var + LN_EPS) * g + b


def _proj_res_sum(xs, w_ref, res_ref, s_ref):
    d = s_ref.shape[1]
    cw = d // PROJ_COL_CHUNKS
    total = None
    for j in range(PROJ_COL_CHUNKS):
        cols = slice(j * cw, (j + 1) * cw)
        s = ALPHA * res_ref[:, cols]
        off = 0
        for x in xs:
            s = s + _dot(x, w_ref[off:off + x.shape[1], cols])
            off += x.shape[1]
        s_ref[:, cols] = s
        part = jnp.sum(s, axis=-1, keepdims=True)
        total = part if total is None else total + part
    return total


def _ln_rows_from(s_ref, total, g_ref, b_ref, y_ref, yb_ref):
    d = s_ref.shape[1]
    cw = d // PROJ_COL_CHUNKS
    mu = total * (1.0 / d)
    sq = None
    for j in range(PROJ_COL_CHUNKS):
        c = s_ref[:, j * cw:(j + 1) * cw] - mu
        part = jnp.sum(c * c, axis=-1, keepdims=True)
        sq = part if sq is None else sq + part
    rstd = lax.rsqrt(sq * (1.0 / d) + LN_EPS)
    for j in range(PROJ_COL_CHUNKS):
        cols = slice(j * cw, (j + 1) * cw)
        y = (s_ref[:, cols] - mu) * rstd * g_ref[:, cols] + b_ref[:, cols]
        y_ref[:, cols] = y
        yb_ref[:, cols] = y.astype(BF16)


def _proj_res_ln(xs, w_ref, res_ref, g_ref, b_ref, y_ref, yb_ref):
    total = _proj_res_sum(xs, w_ref, res_ref, y_ref)
    _ln_rows_from(y_ref, total, g_ref, b_ref, y_ref, yb_ref)


def _cast_kernel(w_ref, o_ref):
    o_ref[...] = w_ref[0].astype(o_ref.dtype)


CAST_BLOCK_BYTES = 8 * 1024 * 1024


def cast_layer(w, layer):
    _, k, n = w.shape
    rows = next(r for r in (4096, 2048, 1024, 512, 256, 128) if k % r == 0 and r * n * 4 <= CAST_BLOCK_BYTES)
    return pl.pallas_call(
        _cast_kernel,
        grid=(k // rows,),
        in_specs=[pl.BlockSpec((1, rows, n), lambda i: (layer, i, 0))],
        out_specs=pl.BlockSpec((rows, n), lambda i: (i, 0)),
        out_shape=SDS((k, n), BF16),
        compiler_params=_cparams(1),
        name="cast_layer",
    )(w)


def _mm_kernel(x_ref, w_ref, o_ref):
    o_ref[...] = _dot(x_ref[...], w_ref[...]).astype(o_ref.dtype)


def matmul(x, w, *, bm, bn, out_dtype=F32):
    m, k = x.shape
    n = w.shape[1]
    return pl.pallas_call(
        _mm_kernel,
        grid=(m // bm, n // bn),
        in_specs=[pl.BlockSpec((bm, k), lambda i, j: (i, 0)), pl.BlockSpec((k, bn), lambda i, j: (0, j))],
        out_specs=pl.BlockSpec((bm, bn), lambda i, j: (i, j)),
        out_shape=SDS((m, n), out_dtype),
        compiler_params=_cparams(2),
        name="matmul",
    )(x, w)


def _mm_x32_kernel(x_ref, w_ref, o_ref, xb_ref):
    @pl.when(pl.program_id(1) == 0)
    def _():
        xb_ref[...] = x_ref[...].astype(BF16)

    o_ref[...] = _dot(xb_ref[...], w_ref[...]).astype(o_ref.dtype)


def matmul_x32(x, w, *, bm, bn, out_dtype=F32):
    m, k = x.shape
    n = w.shape[1]
    return pl.pallas_call(
        _mm_x32_kernel,
        grid=(m // bm, n // bn),
        in_specs=[pl.BlockSpec((bm, k), lambda i, j: (i, 0)), pl.BlockSpec((k, bn), lambda i, j: (0, j))],
        out_specs=pl.BlockSpec((bm, bn), lambda i, j: (i, j)),
        out_shape=SDS((m, n), out_dtype),
        scratch_shapes=[pltpu.VMEM((bm, k), BF16)],
        compiler_params=_cparams(2),
        name="matmul_x32",
    )(x, w)


def _mm_w32_kernel(x_ref, w_ref, o_ref):
    o_ref[...] = _dot(x_ref[...], w_ref[0].astype(BF16)).astype(o_ref.dtype)


def matmul_w32(x, w, layer, *, bn, out_dtype=F32):
    m, k = x.shape
    n = w.shape[2]
    return pl.pallas_call(
        _mm_w32_kernel,
        grid=(n // bn,),
        in_specs=[pl.BlockSpec((m, k), lambda j: (0, 0)), pl.BlockSpec((1, k, bn), lambda j: (layer, 0, j))],
        out_specs=pl.BlockSpec((m, bn), lambda j: (0, j)),
        out_shape=SDS((m, n), out_dtype),
        compiler_params=_cparams(1),
        name="matmul_w32",
    )(x, w)


def _mm_cast_kernel(x_ref, w_ref, o_ref, wb_ref):
    wb_ref[...] = w_ref[0].astype(BF16)
    o_ref[...] = _dot(x_ref[...], wb_ref[...]).astype(o_ref.dtype)


def matmul_cast(x, w, layer, *, bn, out_dtype=F32):
    m, k = x.shape
    n = w.shape[2]
    return pl.pallas_call(
        _mm_cast_kernel,
        grid=(n // bn,),
        in_specs=[pl.BlockSpec((m, k), lambda j: (0, 0)), pl.BlockSpec((1, k, bn), lambda j: (layer, 0, j))],
        out_specs=[pl.BlockSpec((m, bn), lambda j: (0, j)), pl.BlockSpec((k, bn), lambda j: (0, j))],
        out_shape=[SDS((m, n), out_dtype), SDS((k, n), BF16)],
        compiler_params=_cparams(1),
        name="matmul_cast",
    )(x, w)


def _glu_kernel(z_ref, zt_ref, w_ref, o_ref):
    a = _dot(z_ref[...], w_ref[...])
    o_ref[...] = (zt_ref[...].astype(F32) * jax.nn.sigmoid(a)).astype(o_ref.dtype)


def glu(z, w, *, bm, bn):
    m, k = z.shape
    n = w.shape[1]
    return pl.pallas_call(
        _glu_kernel,
        grid=(m // bm, n // bn),
        in_specs=[pl.BlockSpec((bm, k), lambda i, j: (i, 0)), pl.BlockSpec((bm, bn), lambda i, j: (i, j)),
                  pl.BlockSpec((k, bn), lambda i, j: (0, j))],
        out_specs=pl.BlockSpec((bm, bn), lambda i, j: (i, j)),
        out_shape=SDS((m, n), BF16),
        compiler_params=_cparams(2),
        name="glu",
    )(z, z, w)


def _proj_ln_kernel(*refs, n_in):
    xs = refs[:n_in]
    w_ref, res_ref, g_ref, b_ref, y_ref, yb_ref = refs[n_in:]
    _proj_res_ln([x_ref[...] for x_ref in xs], w_ref, res_ref, g_ref, b_ref, y_ref, yb_ref)


def proj_res_ln(xs, w, res, g, b, *, bm):
    m, d = res.shape
    k = w.shape[0]
    row = lambda i: (i, 0)
    fixed = lambda i: (0, 0)
    return pl.pallas_call(
        functools.partial(_proj_ln_kernel, n_in=len(xs)),
        grid=(m // bm,),
        in_specs=[pl.BlockSpec((bm, x.shape[1]), row) for x in xs] + [
            pl.BlockSpec((k, d), fixed, pipeline_mode=pl.Buffered(1)),
            pl.BlockSpec((bm, d), row),
            pl.BlockSpec((1, d), fixed),
            pl.BlockSpec((1, d), fixed),
        ],
        out_specs=[pl.BlockSpec((bm, d), row), pl.BlockSpec((bm, d), row)],
        out_shape=[SDS((m, d), F32), SDS((m, d), BF16)],
        compiler_params=_cparams(1),
        name="proj_res_ln",
    )(*xs, w, res, g.reshape(1, d), b.reshape(1, d))


def _hg_out_kernel(o_ref, gate_ref, ng_ref, w_ref, res_ref, g_ref, b_ref, y_ref, yb_ref):
    o = o_ref[...]
    xn = o * lax.rsqrt(jnp.mean(o * o, axis=-1, keepdims=True) + RMS_EPS) * ng_ref[...]
    xn = xn * jax.nn.sigmoid(gate_ref[...])
    _proj_res_ln([xn.astype(BF16)], w_ref, res_ref, g_ref, b_ref, y_ref, yb_ref)


def hg_out_ln(o, h_all, norm_g, w, res, g, b, *, bm):
    m, d = res.shape
    row = lambda i: (i, 0)
    fixed = lambda i: (0, 0)
    return pl.pallas_call(
        _hg_out_kernel,
        grid=(m // bm,),
        in_specs=[
            pl.BlockSpec((bm, d), row),
            pl.BlockSpec((bm, d), lambda i: (i, 3)),
            pl.BlockSpec((1, d), fixed),
            pl.BlockSpec((d, d), fixed, pipeline_mode=pl.Buffered(1)),
            pl.BlockSpec((bm, d), row),
            pl.BlockSpec((1, d), fixed),
            pl.BlockSpec((1, d), fixed),
        ],
        out_specs=[pl.BlockSpec((bm, d), row), pl.BlockSpec((bm, d), row)],
        out_shape=[SDS((m, d), F32), SDS((m, d), BF16)],
        compiler_params=_cparams(1),
        name="hg_out_ln",
    )(o, h_all, norm_g.reshape(1, d), w, res, g.reshape(1, d), b.reshape(1, d))


def _ffn_step(f, x_ref, wgu_ref, wd_ref, g_ref, b_ref, y_ref, yb_ref):
    @pl.when(f == 0)
    def _():
        x = x_ref[...]
        yb_ref[...] = x.astype(BF16)
        y_ref[...] = ALPHA * x

    bf = wd_ref.shape[0]
    gu = _dot(yb_ref[...], wgu_ref[...])
    gate, up = gu[:, :bf], gu[:, bf:]
    h = (gate * jax.nn.sigmoid(gate) * up).astype(BF16)
    y_ref[...] += _dot(h, wd_ref[...])

    @pl.when(f == pl.num_programs(1) - 1)
    def _():
        g, b = g_ref[...], b_ref[...]
        chunk = math.gcd(y_ref.shape[0], LN_ROW_CHUNK)

        def ln_chunk(i, _):
            rows = pl.ds(pl.multiple_of(i * chunk, chunk), chunk)
            y = _layer_norm_rows(y_ref[rows, :], g, b)
            y_ref[rows, :] = y
            yb_ref[rows, :] = y.astype(BF16)
            return 0

        lax.fori_loop(0, y_ref.shape[0] // chunk, ln_chunk, 0)


def _ffn_kernel(x_ref, wgu_ref, wd_ref, g_ref, b_ref, y_ref, yb_ref):
    _ffn_step(pl.program_id(1), x_ref, wgu_ref.at[0], wd_ref, g_ref, b_ref, y_ref, yb_ref)


def ffn_ln(x, wgu, wd, g, b, *, bm):
    m, d = x.shape
    nf, _, bf2 = wgu.shape
    return pl.pallas_call(
        _ffn_kernel,
        grid=(m // bm, nf),
        in_specs=[
            pl.BlockSpec((bm, d), lambda i, f: (i, 0)),
            pl.BlockSpec((1, d, bf2), lambda i, f: (f, 0, 0)),
            pl.BlockSpec((bf2 // 2, d), lambda i, f: (f, 0)),
            pl.BlockSpec((1, d), lambda i, f: (0, 0)),
            pl.BlockSpec((1, d), lambda i, f: (0, 0)),
        ],
        out_specs=[pl.BlockSpec((bm, d), lambda i, f: (i, 0)), pl.BlockSpec((bm, d), lambda i, f: (i, 0))],
        out_shape=[SDS((m, d), F32), SDS((m, d), BF16)],
        compiler_params=_cparams(2, vmem=VMEM_LIMIT_MAX),
        name="ffn_ln",
    )(x, wgu, wd, g.reshape(1, d), b.reshape(1, d))


def _ffn_cast_kernel(x_ref, wg_ref, wu_ref, wd_ref, g_ref, b_ref, y_ref, yb_ref, wgub_ref, wdb_ref):
    bf = wd_ref.shape[1]
    wgub_ref[0, :, :bf] = wg_ref[0].astype(BF16)
    wgub_ref[0, :, bf:] = wu_ref[0].astype(BF16)
    wdb_ref[...] = wd_ref[0].astype(BF16)
    _ffn_step(pl.program_id(1), x_ref, wgub_ref.at[0], wdb_ref, g_ref, b_ref, y_ref, yb_ref)


def ffn_ln_cast(x, wg, wu, wd, layer, g, b, *, bf=256):
    m, d = x.shape
    dff = wg.shape[2]
    fixed = lambda i, f: (0, 0)
    return pl.pallas_call(
        _ffn_cast_kernel,
        grid=(1, dff // bf),
        in_specs=[
            pl.BlockSpec((m, d), fixed),
            pl.BlockSpec((1, d, bf), lambda i, f: (layer, 0, f)),
            pl.BlockSpec((1, d, bf), lambda i, f: (layer, 0, f)),
            pl.BlockSpec((1, bf, d), lambda i, f: (layer, f, 0)),
            pl.BlockSpec((1, d), fixed),
            pl.BlockSpec((1, d), fixed),
        ],
        out_specs=[
            pl.BlockSpec((m, d), fixed),
            pl.BlockSpec((m, d), fixed),
            pl.BlockSpec((1, d, 2 * bf), lambda i, f: (f, 0, 0)),
            pl.BlockSpec((bf, d), lambda i, f: (f, 0)),
        ],
        out_shape=[SDS((m, d), F32), SDS((m, d), BF16), SDS((dff // bf, d, 2 * bf), BF16), SDS((dff, d), BF16)],
        compiler_params=_cparams(2),
        name="ffn_ln_cast",
    )(x, wg, wu, wd, g.reshape(1, d), b.reshape(1, d))


def _cmul(ar, ai, br, bi):
    return ar * br - ai * bi, ar * bi + ai * br


def _s5_coefs(a_r, a_i, period):
    l = a_r.shape[1]
    pows = [(a_r, a_i)]
    for _ in range(period - 1):
        pows.append(_cmul(pows[-1][0], pows[-1][1], a_r, a_i))
    t = lax.broadcasted_iota(jnp.int32, (8, l), 0) & (period - 1)
    shifts = []
    k = 1
    while k < period:
        keep = t >= k
        shifts.append((jnp.where(keep, pows[k - 1][0], 0.0), jnp.where(keep, pows[k - 1][1], 0.0)))
        k *= 2
    p_r = jnp.broadcast_to(pows[0][0], (8, l))
    p_i = jnp.broadcast_to(pows[0][1], (8, l))
    for j in range(1, period):
        p_r = jnp.where(t == j, pows[j][0], p_r)
        p_i = jnp.where(t == j, pows[j][1], p_i)
    return shifts, (p_r, p_i)


def _s5_scan_tile(x_r, x_i, shifts):
    k = 1
    for c_r, c_i in shifts:
        s_r = pltpu.roll(x_r, k, 0)
        s_i = pltpu.roll(x_i, k, 0)
        x_r, x_i = x_r + c_r * s_r - c_i * s_i, x_i + c_r * s_i + c_i * s_r
        k *= 2
    return x_r, x_i


def _s5_prompt_kernel(u_ref, bw_ref, cw_ref, ar_ref, ai_ref, d_ref, z_ref, hfin_ref, bu_ref, carry_ref):
    tt = pl.program_id(2)
    n_rows = bu_ref.shape[0]

    @pl.when(tt == 0)
    def _():
        carry_ref[...] = jnp.zeros_like(carry_ref)

    bu_ref[...] = _dot(u_ref[0].astype(BF16), bw_ref[0])
    shifts, (p_r, p_i) = _s5_coefs(ar_ref[0], ai_ref[0], 8)

    ch = S5_SCAN_CHUNK
    d_skip = d_ref[0]

    def scan_chunk(c, carry):
        for i in range(ch // 8):
            h_r, h_i = carry
            r0 = pl.multiple_of(c * ch + i * 8, 8)
            x_r, x_i = _s5_scan_tile(bu_ref[pl.ds(r0, 8), 0:S5_HW], bu_ref[pl.ds(r0, 8), S5_HW:2 * S5_HW], shifts)
            x_r, x_i = x_r + p_r * h_r - p_i * h_i, x_i + p_r * h_i + p_i * h_r
            bu_ref[pl.ds(r0, 8), 0:S5_HW] = x_r
            bu_ref[pl.ds(r0, 8), S5_HW:2 * S5_HW] = x_i
            carry = (x_r[7:8], x_i[7:8])
        return carry

    def emit_chunk(c):
        rows = pl.ds(pl.multiple_of(c * ch, ch), ch)
        y = _dot(bu_ref[rows, :].astype(BF16), cw_ref[0]) + d_skip * u_ref[0, rows, :]
        z_ref[0, rows, :] = jax.nn.gelu(y).astype(z_ref.dtype)

    def body(c, carry):
        emit_chunk(c - 1)
        return scan_chunk(c, carry)

    carry = scan_chunk(0, (carry_ref[:, 0:S5_HW], carry_ref[:, S5_HW:2 * S5_HW]))
    h_r, h_i = lax.fori_loop(1, n_rows // ch, body, carry)
    emit_chunk(n_rows // ch - 1)
    carry_ref[:, 0:S5_HW] = h_r
    carry_ref[:, S5_HW:2 * S5_HW] = h_i

    @pl.when(tt == pl.num_programs(2) - 1)
    def _():
        hfin_ref[0, 0] = carry_ref[...]


def s5_prompt(h_all, bw, cw, a_r, a_i, d_skip, *, tb):
    bsz, t, _ = h_all.shape
    blk = lambda b, g, s: (g, 0, 0)
    return pl.pallas_call(
        _s5_prompt_kernel,
        grid=(bsz, S5_NGB, t // tb),
        in_specs=[
            pl.BlockSpec((1, tb, S5_UW), lambda b, g, s: (b, s, g)),
            pl.BlockSpec((1, S5_UW, 2 * S5_HW), blk),
            pl.BlockSpec((1, 2 * S5_HW, S5_UW), blk),
            pl.BlockSpec((1, 1, S5_HW), blk),
            pl.BlockSpec((1, 1, S5_HW), blk),
            pl.BlockSpec((1, 1, S5_UW), blk),
        ],
        out_specs=[
            pl.BlockSpec((1, tb, S5_UW), lambda b, g, s: (b, s, g)),
            pl.BlockSpec((1, 1, 1, 2 * S5_HW), lambda b, g, s: (b, g, 0, 0)),
        ],
        out_shape=[SDS((bsz, t, D_S5), BF16), SDS((bsz, S5_NGB, 1, 2 * S5_HW), F32)],
        scratch_shapes=[pltpu.VMEM((tb, 2 * S5_HW), F32), pltpu.VMEM((1, 2 * S5_HW), F32)],
        compiler_params=_cparams(3),
        name="s5_prompt",
    )(h_all, bw, cw, a_r, a_i, d_skip)


def _s5_sample_kernel(u_ref, h0_ref, bw_ref, cw_ref, ar_ref, ai_ref, d_ref, z_ref, h_ref, *, period):
    u = u_ref[...]
    bu = _dot(u.astype(BF16), bw_ref[0])
    shifts, (p_r, p_i) = _s5_coefs(ar_ref[0], ai_ref[0], period)
    for i in range(u.shape[0] // 8):
        rows = slice(i * 8, (i + 1) * 8)
        x_r, x_i = _s5_scan_tile(bu[rows, 0:S5_HW], bu[rows, S5_HW:2 * S5_HW], shifts)
        h0_r = h0_ref[rows, 0:S5_HW]
        h0_i = h0_ref[rows, S5_HW:2 * S5_HW]
        h_ref[rows, 0:S5_HW] = x_r + p_r * h0_r - p_i * h0_i
        h_ref[rows, S5_HW:2 * S5_HW] = x_i + p_r * h0_i + p_i * h0_r
    y = _dot(h_ref[...].astype(BF16), cw_ref[0]) + d_ref[0] * u
    z_ref[...] = jax.nn.gelu(y).astype(z_ref.dtype)


def s5_sample(h_all, h0x, bw, cw, a_r, a_i, d_skip, *, period):
    m = h_all.shape[0]
    blk = lambda g: (g, 0, 0)
    return pl.pallas_call(
        functools.partial(_s5_sample_kernel, period=period),
        grid=(S5_NGB,),
        in_specs=[
            pl.BlockSpec((m, S5_UW), lambda g: (0, g)),
            pl.BlockSpec((m, 2 * S5_HW), lambda g: (0, g)),
            pl.BlockSpec((1, S5_UW, 2 * S5_HW), blk),
            pl.BlockSpec((1, 2 * S5_HW, S5_UW), blk),
            pl.BlockSpec((1, 1, S5_HW), blk),
            pl.BlockSpec((1, 1, S5_HW), blk),
            pl.BlockSpec((1, 1, S5_UW), blk),
        ],
        out_specs=[pl.BlockSpec((m, S5_UW), lambda g: (0, g)), pl.BlockSpec((m, 2 * S5_HW), lambda g: (0, g))],
        out_shape=[SDS((m, D_S5), BF16), SDS((m, S5_NGB * 2 * S5_HW), F32)],
        compiler_params=_cparams(1),
        name="s5_sample",
    )(h_all, h0x, bw, cw, a_r, a_i, d_skip)


def _bias_kernel(rb_ref, o_ref):
    h = pl.program_id(0)
    q = lax.broadcasted_iota(jnp.int32, (WINDOW, 2 * WINDOW), 0)
    k = lax.broadcasted_iota(jnp.int32, (WINDOW, 2 * WINDOW), 1)
    n = jnp.maximum(q + WINDOW - k, 0)
    max_exact = N_BUCKETS // 2
    nf = jnp.maximum(n, 1).astype(F32)
    large = max_exact + (jnp.log(nf / max_exact) / math.log(MAX_DISTANCE / max_exact)
                         * (N_BUCKETS - max_exact)).astype(jnp.int32)
    large = jnp.minimum(large, N_BUCKETS - 1)
    bucket = jnp.where(n < max_exact, n, large)
    out = jnp.zeros((WINDOW, 2 * WINDOW), F32)
    for b in range(N_BUCKETS):
        out = jnp.where(bucket == b, rb_ref[b, h], out)
    o_ref[0] = out


def bias_table(rel_bias):
    return pl.pallas_call(
        _bias_kernel,
        grid=(SWA_HEADS,),
        in_specs=[pl.BlockSpec(memory_space=pltpu.SMEM)],
        out_specs=pl.BlockSpec((1, WINDOW, 2 * WINDOW), lambda h: (h, 0, 0)),
        out_shape=SDS((SWA_HEADS, WINDOW, 2 * WINDOW), F32),
        compiler_params=_cparams(1),
        name="bias_table",
    )(rel_bias)


def _swa_softmax(s, valid, sink_col):
    s = jnp.where(valid, s, NEG_BIG)
    m = jnp.maximum(jnp.max(s, axis=-1, keepdims=True), sink_col)
    p = jnp.exp(s - m)
    return p / (jnp.sum(p, axis=-1, keepdims=True) + jnp.exp(sink_col - m))


def _sink_col(sinks_ref, first_head, rows_per_head):
    n = SWA_GQ * rows_per_head
    r = lax.broadcasted_iota(jnp.int32, (n, 1), 0)
    col = jnp.full((n, 1), sinks_ref[first_head], F32)
    for g in range(1, SWA_GQ):
        col = jnp.where(r >= g * rows_per_head, sinks_ref[first_head + g], col)
    return col


SWA_KV_PER_STEP = 4


def _swa_prompt_kernel(sinks_ref, q_ref, kp_ref, kc_ref, vp_ref, vc_ref, bias_ref, o_ref):
    nkv = SWA_KV_PER_STEP
    part = pl.program_id(0)
    i = pl.program_id(2)
    blk = WINDOW
    q = q_ref[0].astype(BF16)
    kk = jnp.concatenate([kp_ref[0], kc_ref[0]], axis=0).astype(BF16)
    vv = jnp.concatenate([vp_ref[0], vc_ref[0]], axis=0).astype(BF16)
    qi = lax.broadcasted_iota(jnp.int32, (SWA_GQ * blk, 2 * blk), 0) & (blk - 1)
    ki = lax.broadcasted_iota(jnp.int32, (SWA_GQ * blk, 2 * blk), 1)
    valid = (ki > qi) & (ki <= qi + blk) & ((ki >= blk) | (i > 0))
    heads = [slice(j * HEAD_DIM, (j + 1) * HEAD_DIM) for j in range(nkv)]
    q4 = [jnp.concatenate([q[:, (j * SWA_GQ + g) * HEAD_DIM:(j * SWA_GQ + g + 1) * HEAD_DIM]
                           for g in range(SWA_GQ)], axis=0) for j in range(nkv)]
    s = [_dot_nt(q4[j], kk[:, c]) * (HEAD_DIM ** -0.5) for j, c in enumerate(heads)]
    p = [_swa_softmax(s[j] + bias_ref[j * SWA_GQ:(j + 1) * SWA_GQ].reshape(SWA_GQ * blk, 2 * blk), valid,
                      _sink_col(sinks_ref, (part * nkv + j) * SWA_GQ, blk)).astype(BF16) for j in range(nkv)]
    o4 = [_dot(p[j], vv[:, c]) for j, c in enumerate(heads)]
    o_ref[0] = jnp.concatenate([o4[j][g * blk:(g + 1) * blk] for j in range(nkv) for g in range(SWA_GQ)],
                               axis=1).astype(o_ref.dtype)


def swa_prompt(h_all, sinks, table):
    bsz, t, _ = h_all.shape
    nblk = t // WINDOW
    nkv = SWA_KV_PER_STEP
    qw = nkv * SWA_GQ * HEAD_DIM
    kw = nkv * HEAD_DIM
    q0, k0, v0 = D_S5 // qw, (D_S5 + D_SWA_Q) // kw, (D_S5 + D_SWA_Q + D_SWA_KV) // kw
    cur = lambda c0: (lambda p, b, i: (b, i, c0 + p))
    prev = lambda c0: (lambda p, b, i: (b, jnp.maximum(i - 1, 0), c0 + p))
    return pl.pallas_call(
        _swa_prompt_kernel,
        grid=(SWA_KV_HEADS // nkv, bsz, nblk),
        in_specs=[
            pl.BlockSpec(memory_space=pltpu.SMEM),
            pl.BlockSpec((1, WINDOW, qw), cur(q0)),
            pl.BlockSpec((1, WINDOW, kw), prev(k0)),
            pl.BlockSpec((1, WINDOW, kw), cur(k0)),
            pl.BlockSpec((1, WINDOW, kw), prev(v0)),
            pl.BlockSpec((1, WINDOW, kw), cur(v0)),
            pl.BlockSpec((nkv * SWA_GQ, WINDOW, 2 * WINDOW), lambda p, b, i: (p, 0, 0)),
        ],
        out_specs=pl.BlockSpec((1, WINDOW, qw), lambda p, b, i: (b, i, p)),
        out_shape=SDS((bsz, t, D_SWA_Q), BF16),
        compiler_params=_cparams(3),
        name="swa_prompt",
    )(sinks, h_all, h_all, h_all, h_all, h_all, table)


def _swa_sample_kernel(sinks_ref, q_ref, kk_ref, vv_ref, bias_ref, o_ref, *, t_new):
    rows = q_ref.shape[0]
    n_keys = kk_ref.shape[1]
    q = q_ref[...].astype(BF16)
    r = lax.broadcasted_iota(jnp.int32, (SWA_GQ * rows, n_keys), 0)
    ti = r & (t_new - 1)
    ki = lax.broadcasted_iota(jnp.int32, (SWA_GQ * rows, n_keys), 1)
    valid = (ki > ti) & (ki <= ti + WINDOW)
    second = (r & (rows - 1)) >= t_new
    second_o = (lax.broadcasted_iota(jnp.int32, (SWA_GQ * rows, HEAD_DIM), 0) & (rows - 1)) >= t_new
    nkv = SWA_KV_HEADS
    heads = [slice(j * HEAD_DIM, (j + 1) * HEAD_DIM) for j in range(nkv)]
    kk = [kk_ref[bb].astype(BF16) for bb in range(2)]
    vv = [vv_ref[bb].astype(BF16) for bb in range(2)]
    q4 = [jnp.concatenate([q[:, (j * SWA_GQ + g) * HEAD_DIM:(j * SWA_GQ + g + 1) * HEAD_DIM]
                           for g in range(SWA_GQ)], axis=0) for j in range(nkv)]
    s_seq = [[_dot_nt(q4[j], kk[bb][:, c]) for j, c in enumerate(heads)] for bb in range(2)]
    p = []
    for j in range(nkv):
        s = jnp.where(second, s_seq[1][j], s_seq[0][j]) * (HEAD_DIM ** -0.5)
        s = s + bias_ref[j * SWA_GQ:(j + 1) * SWA_GQ].reshape(SWA_GQ * rows, n_keys)
        p.append(_swa_softmax(s, valid, _sink_col(sinks_ref, j * SWA_GQ, rows)).astype(BF16))
    o_seq = [[_dot(p[j], vv[bb][:, c]) for j, c in enumerate(heads)] for bb in range(2)]
    o4 = [jnp.where(second_o, o_seq[1][j], o_seq[0][j]) for j in range(nkv)]
    o_ref[...] = jnp.concatenate([o4[j][g * rows:(g + 1) * rows] for j in range(nkv) for g in range(SWA_GQ)],
                                 axis=1).astype(o_ref.dtype)


def swa_sample(h_all, kk, vv, sinks, bias_s, *, t_new):
    m = h_all.shape[0]
    n_keys = kk.shape[1]
    rows = 2 * t_new
    return pl.pallas_call(
        functools.partial(_swa_sample_kernel, t_new=t_new),
        grid=(m // rows,),
        in_specs=[
            pl.BlockSpec(memory_space=pltpu.SMEM),
            pl.BlockSpec((rows, D_SWA_Q), lambda i: (i, D_S5 // D_SWA_Q)),
            pl.BlockSpec((2, n_keys, D_SWA_KV), lambda i: (i, 0, 0)),
            pl.BlockSpec((2, n_keys, D_SWA_KV), lambda i: (i, 0, 0)),
            pl.BlockSpec((SWA_HEADS, rows, n_keys), lambda i: (0, 0, 0)),
        ],
        out_specs=pl.BlockSpec((rows, D_SWA_Q), lambda i: (i, 0)),
        out_shape=SDS((m, D_SWA_Q), BF16),
        compiler_params=_cparams(1),
        name="swa_sample",
    )(sinks, h_all, kk, vv, bias_s)


def _hg_gates(q, fz, lb):
    qs = q * jax.nn.sigmoid(q)
    f = lb + (1.0 - lb) * jax.nn.sigmoid(fz)
    return qs, jnp.log(f), 1.0 - f


HG_GROUP = 4


def _hgrn_prompt_kernel(q_ref, f_ref, v_ref, lb_ref, o_ref, st_ref, s_scr, *, n_heads):
    tt = pl.program_id(2)
    n_rows = q_ref.shape[1]
    c, ng = HG_CHUNK, HG_GROUP
    gr = c * ng

    @pl.when(tt == 0)
    def _():
        s_scr[...] = jnp.zeros_like(s_scr)

    ri = lax.broadcasted_iota(jnp.int32, (gr, gr), 0)
    ci = lax.broadcasted_iota(jnp.int32, (gr, gr), 1)
    causal = (ri >= ci) & ((ri // c) == (ci // c))
    t_in = lax.broadcasted_iota(jnp.int32, (gr, HG_DK), 0) & (c - 1)
    own = ((lax.broadcasted_iota(jnp.int32, (gr, ng * HG_DK), 0) // c)
           == (lax.broadcasted_iota(jnp.int32, (gr, ng * HG_DK), 1) // HG_DK)).astype(BF16)

    def body(gi, _):
        r0 = pl.multiple_of(gi * gr, gr)
        rows = pl.ds(r0, gr)
        hs = range(n_heads)
        heads = [slice(hd * HG_DK, (hd + 1) * HG_DK) for hd in hs]
        v, qt, kt, kl, decay = [], [], [], [], []
        for cols in heads:
            qs, lf, k = _hg_gates(q_ref[0, rows, cols], f_ref[0, rows, cols], lb_ref[:, cols])
            v.append(v_ref[0, rows, cols].astype(BF16))
            cum = lf
            sh = 1
            while sh < c:
                cum = cum + jnp.where(t_in >= sh, pltpu.roll(cum, sh, 0), 0.0)
                sh *= 2
            last3 = cum.reshape(ng, c, HG_DK)[:, c - 1:c, :]
            last = jnp.broadcast_to(last3, (ng, c, HG_DK)).reshape(gr, HG_DK)
            qt.append((qs * jnp.exp(cum)).astype(BF16))
            kt.append((k * jnp.exp(-cum)).astype(BF16))
            kl.append((k * jnp.exp(last - cum)).astype(BF16))
            decay.append(jnp.exp(last3))
        kv = [_dot_tn(v[h], jnp.concatenate([kl[h]] * ng, axis=1) * own) for h in hs]
        attn = [jnp.where(causal, _dot_nt(qt[h], kt[h]), 0.0).astype(BF16) for h in hs]
        s_cat = []
        for h in hs:
            st = s_scr[h]
            starts = []
            for j in range(ng):
                starts.append(st)
                st = st * decay[h][j] + kv[h][:, j * HG_DK:(j + 1) * HG_DK]
            s_scr[h] = st
            s_cat.append(jnp.concatenate(starts, axis=1).astype(BF16))
        intra = [_dot(attn[h], v[h]) for h in hs]
        inter = [_dot_nt(jnp.concatenate([qt[h]] * ng, axis=1) * own, s_cat[h]) for h in hs]
        for h, cols in enumerate(heads):
            o_ref[0, rows, cols] = intra[h] + inter[h]
        return 0

    lax.fori_loop(0, n_rows // gr, body, 0, unroll=True)

    @pl.when(tt == pl.num_programs(2) - 1)
    def _():
        st_ref[0] = s_scr[...]


def hgrn_prompt(h_all, lb, *, tb, n_heads=2):
    bsz, t, _ = h_all.shape
    hw = n_heads * HG_DK
    nhb = D_MODEL // hw
    col = lambda c0: (lambda b, h, s: (b, s, c0 * nhb + h))
    return pl.pallas_call(
        functools.partial(_hgrn_prompt_kernel, n_heads=n_heads),
        grid=(bsz, nhb, t // tb),
        in_specs=[
            pl.BlockSpec((1, tb, hw), col(0)),
            pl.BlockSpec((1, tb, hw), col(1)),
            pl.BlockSpec((1, tb, hw), col(2)),
            pl.BlockSpec((1, hw), lambda b, h, s: (0, h)),
        ],
        out_specs=[
            pl.BlockSpec((1, tb, hw), lambda b, h, s: (b, s, h)),
            pl.BlockSpec((1, n_heads, HG_DK, HG_DK), lambda b, h, s: (b, h, 0, 0)),
        ],
        out_shape=[SDS((bsz, t, D_MODEL), F32), SDS((bsz, HG_HEADS, HG_DK, HG_DK), F32)],
        scratch_shapes=[pltpu.VMEM((n_heads, HG_DK, HG_DK), F32)],
        compiler_params=_cparams(3),
        name="hgrn_prompt",
    )(h_all, h_all, h_all, lb.reshape(1, D_MODEL))


def _hgrn_sample_kernel(q_ref, f_ref, v_ref, lb_ref, s0_ref, o_ref, s_ref, *, n_heads, t_new):
    rows = q_ref.shape[0]
    ri = lax.broadcasted_iota(jnp.int32, (rows, rows), 0)
    ci = lax.broadcasted_iota(jnp.int32, (rows, rows), 1)
    causal = (ri >= ci) & ((ri >= t_new) == (ci >= t_new))
    tril = causal.astype(F32)
    hw = n_heads * HG_DK
    second_w = lax.broadcasted_iota(jnp.int32, (rows, hw), 0) >= t_new
    second = second_w[:, :HG_DK]
    eye = (lax.broadcasted_iota(jnp.int32, (HG_DK, HG_DK), 0)
           == lax.broadcasted_iota(jnp.int32, (HG_DK, HG_DK), 1))
    heads = [slice(hd * HG_DK, (hd + 1) * HG_DK) for hd in range(n_heads)]
    qs, lf, k = _hg_gates(q_ref[...], f_ref[...], lb_ref[...])
    v = v_ref[...].astype(BF16)
    cum = jnp.dot(tril, lf, preferred_element_type=F32, precision=lax.Precision.HIGHEST)
    last0 = cum[t_new - 1:t_new]
    last1 = cum[rows - 1:rows]
    qt = (qs * jnp.exp(cum)).astype(BF16)
    kt = (k * jnp.exp(-cum)).astype(BF16)
    kl = k * jnp.exp(jnp.where(second_w, last1, last0) - cum)
    kl_seq = (jnp.where(second_w, 0.0, kl).astype(BF16), jnp.where(second_w, kl, 0.0).astype(BF16))
    decay_seq = (jnp.exp(last0), jnp.exp(last1))
    attn = [jnp.where(causal, _dot_nt(qt[:, c], kt[:, c]), 0.0).astype(BF16) for c in heads]
    inter = [[_dot(qt[:, c], s0_ref[bb, hd].astype(BF16)) for hd, c in enumerate(heads)] for bb in range(2)]
    intra = [_dot(a, v[:, c]) for a, c in zip(attn, heads)]
    for hd, c in enumerate(heads):
        o_ref[:, c] = intra[hd] + jnp.where(second, inter[1][hd], inter[0][hd])
    for bb in range(2):
        kv = [_dot_tn(kl_seq[bb][:, c], v[:, c]) for c in heads]
        for hd, c in enumerate(heads):
            decay = jnp.sum(jnp.where(eye, decay_seq[bb][:, c], 0.0), axis=1, keepdims=True)
            s_ref[bb, hd] = decay * s0_ref[bb, hd] + kv[hd]


def hgrn_sample(h_all, lb, s0, *, t_new, n_heads=8):
    m = h_all.shape[0]
    rows = 2 * t_new
    hw = n_heads * HG_DK
    nhb = D_MODEL // hw
    col = lambda c0: (lambda i, h: (i, c0 * nhb + h))
    return pl.pallas_call(
        functools.partial(_hgrn_sample_kernel, n_heads=n_heads, t_new=t_new),
        grid=(m // rows, nhb),
        in_specs=[
            pl.BlockSpec((rows, hw), col(0)),
            pl.BlockSpec((rows, hw), col(1)),
            pl.BlockSpec((rows, hw), col(2)),
            pl.BlockSpec((1, hw), lambda i, h: (0, h)),
            pl.BlockSpec((2, n_heads, HG_DK, HG_DK), lambda i, h: (i, h, 0, 0)),
        ],
        out_specs=[
            pl.BlockSpec((rows, hw), lambda i, h: (i, h)),
            pl.BlockSpec((2, n_heads, HG_DK, HG_DK), lambda i, h: (i, h, 0, 0)),
        ],
        out_shape=[SDS((m, D_MODEL), F32), SDS(s0.shape, F32)],
        compiler_params=_cparams(2),
        name="hgrn_sample",
    )(h_all, h_all, h_all, lb.reshape(1, D_MODEL), s0)


def _mem_attention(q, mk_ref, mv_ref, n_seq):
    rows = q.shape[0]
    second = lax.broadcasted_iota(jnp.int32, (rows, 1), 0) >= rows // 2

    def mem_head(ref, seq, h):
        if len(ref.shape) == 3:
            return ref[seq, :, h * MEM_HEAD_DIM:(h + 1) * MEM_HEAD_DIM].astype(BF16)
        return ref[0, seq, :, h, :].astype(BF16)

    hs = range(MEM_HEADS)
    qh = [q[:, h * MEM_HEAD_DIM:(h + 1) * MEM_HEAD_DIM] for h in hs]
    s_seq = [[_dot_nt(qh[h], mem_head(mk_ref, seq, h)) for h in hs] for seq in range(n_seq)]
    p = []
    for h in hs:
        s = s_seq[0][h] if n_seq == 1 else jnp.where(second, s_seq[1][h], s_seq[0][h])
        s = s * (MEM_HEAD_DIM ** -0.5)
        e = jnp.exp(s - jnp.max(s, axis=-1, keepdims=True))
        p.append((e / jnp.sum(e, axis=-1, keepdims=True)).astype(BF16))
    o_seq = [[_dot(p[h], mem_head(mv_ref, seq, h)) for h in hs] for seq in range(n_seq)]
    heads = [(o_seq[0][h] if n_seq == 1 else jnp.where(second, o_seq[1][h], o_seq[0][h])).astype(BF16) for h in hs]
    return jnp.concatenate(heads, axis=1)


def _xattn_kernel(y_ref, yb_ref, wq_ref, mk_ref, mv_ref, wo_ref, g_ref, b_ref, o_ref, ob_ref, *, n_seq):
    q = _dot(yb_ref[...], wq_ref[...]).astype(BF16)
    att = _mem_attention(q, mk_ref, mv_ref, n_seq)
    _proj_res_ln([att], wo_ref, y_ref, g_ref, b_ref, o_ref, ob_ref)


def _mem_attn_kernel(q_ref, mk_ref, mv_ref, o_ref):
    o_ref[...] = _mem_attention(q_ref[...], mk_ref, mv_ref, 2)


def mem_attn_cache(q, cache_k, cache_v, layer, *, rows_per_seq):
    m, dm = q.shape
    rows = 2 * rows_per_seq
    mem_spec = pl.BlockSpec((1, 2, N_MEM, MEM_HEADS, MEM_HEAD_DIM), lambda i: (layer, i, 0, 0, 0))
    return pl.pallas_call(
        _mem_attn_kernel,
        grid=(m // rows,),
        in_specs=[pl.BlockSpec((rows, dm), lambda i: (i, 0)), mem_spec, mem_spec],
        out_specs=pl.BlockSpec((rows, dm), lambda i: (i, 0)),
        out_shape=SDS((m, dm), BF16),
        compiler_params=_cparams(1),
        name="mem_attn_cache",
    )(q, cache_k, cache_v)


def xattn_ln(y, yb, wq, mk, mv, wo, g, b, *, bm, rows_per_seq):
    m, d = y.shape
    assert rows_per_seq % bm == 0
    row = lambda i: (i, 0)
    fixed = lambda i: (0, 0)
    mem_spec = pl.BlockSpec((1, N_MEM, D_MEM), lambda i: (i * bm // rows_per_seq, 0, 0))
    return pl.pallas_call(
        functools.partial(_xattn_kernel, n_seq=1),
        grid=(m // bm,),
        in_specs=[
            pl.BlockSpec((bm, d), row),
            pl.BlockSpec((bm, d), row),
            pl.BlockSpec((d, D_MEM), fixed),
            mem_spec,
            mem_spec,
            pl.BlockSpec((D_MEM, d), fixed),
            pl.BlockSpec((1, d), fixed),
            pl.BlockSpec((1, d), fixed),
        ],
        out_specs=[pl.BlockSpec((bm, d), row), pl.BlockSpec((bm, d), row)],
        out_shape=[SDS((m, d), F32), SDS((m, d), BF16)],
        compiler_params=_cparams(1),
        name="xattn_ln",
    )(y, yb, wq, mk, mv, wo, g.reshape(1, d), b.reshape(1, d))


def _s5_discretize(lam_re, lam_im, log_dt, b_re, b_im):
    lr = jnp.minimum(lam_re.astype(F32), -1e-4)
    li = lam_im.astype(F32)
    dt = jnp.exp(log_dt.astype(F32))[:, None]
    mag = jnp.exp(lr * dt)
    a_re = mag * jnp.cos(li * dt)
    a_im = mag * jnp.sin(li * dt)
    den = lr * lr + li * li
    fr = ((a_re - 1.0) * lr + a_im * li) / den
    fi = (a_im * lr - (a_re - 1.0) * li) / den
    br, bi = b_re.astype(F32), b_im.astype(F32)
    bb_re = fr[..., None] * br - fi[..., None] * bi
    bb_im = fr[..., None] * bi + fi[..., None] * br
    return a_re, a_im, bb_re, bb_im


def _s5_block_weights(a_re, a_im, bb_re, bb_im, c_re, c_im, d_skip):
    eye = jnp.eye(S5_GB, dtype=F32)
    shp_b = (S5_NGB, S5_GB, S5_STATE, S5_GROUP)
    shp_c = (S5_NGB, S5_GB, S5_GROUP, S5_STATE)
    blk_b = lambda w: jnp.einsum('bgph,gk->bghkp', w.reshape(shp_b), eye).reshape(S5_NGB, S5_UW, S5_HW)
    blk_c = lambda w: jnp.einsum('bghp,gk->bkpgh', w.reshape(shp_c), eye).reshape(S5_NGB, S5_HW, S5_UW)
    bw = jnp.concatenate([blk_b(bb_re), blk_b(bb_im)], axis=-1).astype(BF16)
    cw = jnp.concatenate([blk_c(c_re.astype(F32)), blk_c(-c_im.astype(F32))], axis=1).astype(BF16)
    return (bw, cw, a_re.reshape(S5_NGB, 1, S5_HW), a_im.reshape(S5_NGB, 1, S5_HW),
            d_skip.astype(F32).reshape(S5_NGB, 1, S5_UW))


def _s5_state_to_blocks(s_re, s_im):
    b = s_re.shape[0]
    return jnp.concatenate([s_re.reshape(b, S5_NGB, S5_HW), s_im.reshape(b, S5_NGB, S5_HW)],
                           axis=-1).reshape(b, S5_NGB * 2 * S5_HW)


def _s5_state_from_blocks(h):
    b = h.shape[0]
    return (h[:, :, :S5_HW].reshape(b, S5_GROUPS, S5_STATE), h[:, :, S5_HW:].reshape(b, S5_GROUPS, S5_STATE))


def kernel(x_prompt, x_sample, cache_mem_k, cache_mem_v, cache_swa_k, cache_swa_v, state_s5_re, state_s5_im, state_hgrn, mem_prompt, rel_bias, w_even_in, s5_lam_re, s5_lam_im, s5_log_dt, s5_b_re, s5_b_im, s5_c_re, s5_c_im, s5_d, s5_w_glu, swa_sinks, w_even_out, hg_lb_logits, w_odd_in, hg_norm_g, w_odd_out, w_mem_q, w_mem_k, w_mem_v, w_mem_o, w_ffn_gate, w_ffn_up, w_ffn_down, ln_g, ln_b):
    bsz, seq, d = x_prompt.shape
    dec_b, dec_t, _ = x_sample.shape
    mp, ms = bsz * seq, dec_b * dec_t
    w_buf = cache_swa_k.shape[2]
    bf = lambda w: w.astype(BF16)

    lb_soft = jax.nn.softmax(hg_lb_logits.astype(F32), axis=0)
    lower_bounds = jnp.cumsum(lb_soft, axis=0) - lb_soft[0]
    table = bias_table(rel_bias.astype(F32))
    bias_s = jnp.tile(table[:, :dec_t], (1, 2, 1))
    mem_b = bf(mem_prompt).reshape(bsz * N_MEM, d)

    yp = x_prompt.reshape(mp, d)
    ys = x_sample.reshape(ms, d)
    ypb, ysb = None, bf(ys)
    p_mem_k, p_mem_v = [], []
    p_swa_k, p_swa_v, p_s5_re, p_s5_im, p_hg = [], [], [], [], []
    s_swa_k, s_swa_v, s_s5_re, s_s5_im, s_hg = [], [], [], [], []

    for l in range(DEPTH):
        j = l // 2
        g0, b0 = ln_g[l, 0], ln_b[l, 0]
        if l % 2 == 0:
            a_re, a_im, bb_re, bb_im = _s5_discretize(s5_lam_re[j], s5_lam_im[j], s5_log_dt[j], s5_b_re[j], s5_b_im[j])
            s5w = _s5_block_weights(a_re, a_im, bb_re, bb_im, s5_c_re[j], s5_c_im[j], s5_d[j])
            w_glu, w_out = cast_layer(s5_w_glu, j), cast_layer(w_even_out, j)
            sinks = swa_sinks[j].astype(F32)
            kv0 = D_S5 + D_SWA_Q

            hs, w_in = matmul_cast(ysb, w_even_in, j, bn=512)
            hp = (matmul_x32(yp, w_in, bm=512, bn=1024) if ypb is None
                  else matmul(ypb, w_in, bm=1024, bn=1024)).reshape(bsz, seq, -1)
            z, hfin = s5_prompt(hp, *s5w, tb=512)
            s5_out = glu(z.reshape(mp, D_S5), w_glu, bm=1024, bn=1024)
            att = swa_prompt(hp, sinks, table).reshape(mp, D_SWA_Q)
            hr, hi = _s5_state_from_blocks(hfin[:, :, 0])
            p_s5_re.append(hr); p_s5_im.append(hi)
            p_swa_k.append(hp[:, seq - w_buf:, kv0:kv0 + D_SWA_KV].reshape(bsz, w_buf, SWA_KV_HEADS, HEAD_DIM))
            p_swa_v.append(hp[:, seq - w_buf:, kv0 + D_SWA_KV:].reshape(bsz, w_buf, SWA_KV_HEADS, HEAD_DIM))
            yp, ypb = proj_res_ln([s5_out, att], w_out, yp, g0, b0, bm=128)

            h0x = jnp.repeat(_s5_state_to_blocks(state_s5_re[j].astype(F32), state_s5_im[j].astype(F32)), dec_t, axis=0)
            z, h_steps = s5_sample(hs, h0x, *s5w, period=dec_t)
            s5_out = glu(z, w_glu, bm=ms, bn=1024)
            hs3 = hs.reshape(dec_b, dec_t, -1)
            kpad = jnp.zeros((dec_b, 2 * WINDOW - w_buf - dec_t, D_SWA_KV), F32)
            kk = jnp.concatenate([cache_swa_k[j].reshape(dec_b, w_buf, D_SWA_KV).astype(F32),
                                  hs3[:, :, kv0:kv0 + D_SWA_KV], kpad], axis=1)
            vv = jnp.concatenate([cache_swa_v[j].reshape(dec_b, w_buf, D_SWA_KV).astype(F32),
                                  hs3[:, :, kv0 + D_SWA_KV:], kpad], axis=1)
            att = swa_sample(hs, kk, vv, sinks, bias_s, t_new=dec_t)
            hr, hi = _s5_state_from_blocks(h_steps.reshape(dec_b, dec_t, S5_NGB, 2 * S5_HW)[:, dec_t - 1])
            s_s5_re.append(hr); s_s5_im.append(hi)
            s_swa_k.append(kk[:, dec_t:dec_t + w_buf].reshape(dec_b, w_buf, SWA_KV_HEADS, HEAD_DIM))
            s_swa_v.append(vv[:, dec_t:dec_t + w_buf].reshape(dec_b, w_buf, SWA_KV_HEADS, HEAD_DIM))
            ys, ysb = proj_res_ln([s5_out, att], w_out, ys, g0, b0, bm=ms)
        else:
            w_out = cast_layer(w_odd_out, j)
            lb = lower_bounds[l]

            hs, w_in = matmul_cast(ysb, w_odd_in, j, bn=512)
            hp = matmul(ypb, w_in, bm=1024, bn=1024)
            o, st = hgrn_prompt(hp.reshape(bsz, seq, -1), lb, tb=512, n_heads=4)
            p_hg.append(jnp.swapaxes(st, -1, -2))
            yp, ypb = hg_out_ln(o.reshape(mp, d), hp, hg_norm_g[j], w_out, yp, g0, b0, bm=128)

            o, s_new = hgrn_sample(hs, lb, state_hgrn[j].astype(F32), t_new=dec_t)
            s_hg.append(s_new)
            ys, ysb = hg_out_ln(o, hs, hg_norm_g[j], w_out, ys, g0, b0, bm=ms)

        wo = cast_layer(w_mem_o, l)
        mk = matmul_w32(mem_b, w_mem_k, l, bn=D_MEM).reshape(bsz, N_MEM, D_MEM)
        mv = matmul_w32(mem_b, w_mem_v, l, bn=D_MEM).reshape(bsz, N_MEM, D_MEM)
        p_mem_k.append(mk.reshape(bsz, N_MEM, MEM_HEADS, MEM_HEAD_DIM))
        p_mem_v.append(mv.reshape(bsz, N_MEM, MEM_HEADS, MEM_HEAD_DIM))
        qs, wq = matmul_cast(ysb, w_mem_q, l, bn=D_MEM, out_dtype=BF16)
        yp, ypb = xattn_ln(yp, ypb, wq, mk, mv, wo, ln_g[l, 1], ln_b[l, 1], bm=256, rows_per_seq=seq)
        att = mem_attn_cache(qs, cache_mem_k, cache_mem_v, l, rows_per_seq=dec_t)
        ys, ysb = proj_res_ln([att], wo, ys, ln_g[l, 1], ln_b[l, 1], bm=ms)

        ys, ysb, wgu, wd = ffn_ln_cast(ys, w_ffn_gate, w_ffn_up, w_ffn_down, l, ln_g[l, 2], ln_b[l, 2])
        yp, ypb = ffn_ln(yp, wgu, wd, ln_g[l, 2], ln_b[l, 2], bm=512)

    return (yp.reshape(bsz, seq, d), ys.reshape(dec_b, dec_t, d),
            jnp.stack(p_mem_k), jnp.stack(p_mem_v),
            jnp.stack(p_swa_k), jnp.stack(p_swa_v),
            jnp.stack(p_s5_re), jnp.stack(p_s5_im), jnp.stack(p_hg),
            jnp.stack(s_swa_k), jnp.stack(s_swa_v),
            jnp.stack(s_s5_re), jnp.stack(s_s5_im), jnp.stack(s_hg))
```

```python
import functools
import math

import jax
import jax.numpy as jnp
from jax import lax
from jax.experimental import pallas as pl
from jax.experimental.pallas import tpu as pltpu

F32 = jnp.float32
BF16 = jnp.bfloat16
SDS = jax.ShapeDtypeStruct

D_MODEL = 4096
DEPTH = 2
ALPHA = (2 * DEPTH) ** 0.25
LN_EPS = 1e-5
RMS_EPS = 1e-6
NEG_BIG = -1e30

S5_GROUPS, S5_GROUP, S5_STATE = 128, 16, 64
D_S5 = S5_GROUPS * S5_GROUP
S5_GB = 16
S5_NGB = S5_GROUPS // S5_GB
S5_UW = S5_GB * S5_GROUP
S5_HW = S5_GB * S5_STATE

HEAD_DIM, SWA_HEADS, SWA_KV_HEADS, SWA_GQ = 64, 32, 8, 4
D_SWA_Q, D_SWA_KV = 2048, 512
WINDOW = 128
N_BUCKETS, MAX_DISTANCE = 32, 128

HG_DK, HG_HEADS, HG_CHUNK = 128, 32, 32
N_MEM, MEM_HEADS, MEM_HEAD_DIM, D_MEM = 256, 4, 128, 512

VMEM_LIMIT = 56 * 1024 * 1024
VMEM_LIMIT_MAX = 60 * 1024 * 1024
PROJ_COL_CHUNKS = 4
LN_ROW_CHUNK = 64


def _cparams(n_axes, vmem=VMEM_LIMIT):
    return pltpu.CompilerParams(dimension_semantics=("arbitrary",) * n_axes, vmem_limit_bytes=vmem)


def _dot(a, b):
    return jnp.dot(a, b, preferred_element_type=F32)


def _dot_nt(a, b):
    return lax.dot_general(a, b, (((1,), (1,)), ((), ())), preferred_element_type=F32)


def _dot_tn(a, b):
    return lax.dot_general(a, b, (((0,), (0,)), ((), ())), preferred_element_type=F32)


def _layer_norm_rows(s, g, b):
    mu = jnp.mean(s, axis=-1, keepdims=True)
    c = s - mu
    var = jnp.mean(c * c, axis=-1, keepdims=True)
    return c * lax.rsqrt(var + LN_EPS) * g + b


def _proj_res_sum(xs, w_ref, res_ref, s_ref):
    d = s_ref.shape[1]
    cw = d // PROJ_COL_CHUNKS
    total = None
    for j in range(PROJ_COL_CHUNKS):
        cols = slice(j * cw, (j + 1) * cw)
        s = ALPHA * res_ref[:, cols]
        off = 0
        for x in xs:
            s = s + _dot(x, w_ref[off:off + x.shape[1], cols])
            off += x.shape[1]
        s_ref[:, cols] = s
        part = jnp.sum(s, axis=-1, keepdims=True)
        total = part if total is None else total + part
    return total


def _ln_rows_from(s_ref, total, g_ref, b_ref, y_ref, yb_ref):
    d = s_ref.shape[1]
    cw = d // PROJ_COL_CHUNKS
    mu = total * (1.0 / d)
    sq = None
    for j in range(PROJ_COL_CHUNKS):
        c = s_ref[:, j * cw:(j + 1) * cw] - mu
        part = jnp.sum(c * c, axis=-1, keepdims=True)
        sq = part if sq is None else sq + part
    rstd = lax.rsqrt(sq * (1.0 / d) + LN_EPS)
    for j in range(PROJ_COL_CHUNKS):
        cols = slice(j * cw, (j + 1) * cw)
        y = (s_ref[:, cols] - mu) * rstd * g_ref[:, cols] + b_ref[:, cols]
        y_ref[:, cols] = y
        yb_ref[:, cols] = y.astype(BF16)


def _proj_res_ln(xs, w_ref, res_ref, g_ref, b_ref, y_ref, yb_ref):
    total = _proj_res_sum(xs, w_ref, res_ref, y_ref)
    _ln_rows_from(y_ref, total, g_ref, b_ref, y_ref, yb_ref)


def _cast_kernel(w_ref, o_ref):
    o_ref[...] = w_ref[0].astype(o_ref.dtype)


CAST_BLOCK_BYTES = 8 * 1024 * 1024


def cast_layer(w, layer):
    _, k, n = w.shape
    rows = next(r for r in (4096, 2048, 1024, 512, 256, 128) if k % r == 0 and r * n * 4 <= CAST_BLOCK_BYTES)
    return pl.pallas_call(
        _cast_kernel,
        grid=(k // rows,),
        in_specs=[pl.BlockSpec((1, rows, n), lambda i: (layer, i, 0))],
        out_specs=pl.BlockSpec((rows, n), lambda i: (i, 0)),
        out_shape=SDS((k, n), BF16),
        compiler_params=_cparams(1),
        name="cast_layer",
    )(w)


def _mm_kernel(x_ref, w_ref, o_ref):
    o_ref[...] = _dot(x_ref[...], w_ref[...]).astype(o_ref.dtype)


def matmul(x, w, *, bm, bn, out_dtype=F32):
    m, k = x.shape
    n = w.shape[1]
    return pl.pallas_call(
        _mm_kernel,
        grid=(m // bm, n // bn),
        in_specs=[pl.BlockSpec((bm, k), lambda i, j: (i, 0)), pl.BlockSpec((k, bn), lambda i, j: (0, j))],
        out_specs=pl.BlockSpec((bm, bn), lambda i, j: (i, j)),
        out_shape=SDS((m, n), out_dtype),
        compiler_params=_cparams(2),
        name="matmul",
    )(x, w)


def _mm_x32_kernel(x_ref, w_ref, o_ref, xb_ref):
    @pl.when(pl.program_id(1) == 0)
    def _():
        xb_ref[...] = x_ref[...].astype(BF16)

    o_ref[...] = _dot(xb_ref[...], w_ref[...]).astype(o_ref.dtype)


def matmul_x32(x, w, *, bm, bn, out_dtype=F32):
    m, k = x.shape
    n = w.shape[1]
    return pl.pallas_call(
        _mm_x32_kernel,
        grid=(m // bm, n // bn),
        in_specs=[pl.BlockSpec((bm, k), lambda i, j: (i, 0)), pl.BlockSpec((k, bn), lambda i, j: (0, j))],
        out_specs=pl.BlockSpec((bm, bn), lambda i, j: (i, j)),
        out_shape=SDS((m, n), out_dtype),
        scratch_shapes=[pltpu.VMEM((bm, k), BF16)],
        compiler_params=_cparams(2),
        name="matmul_x32",
    )(x, w)


def _mm_w32_kernel(x_ref, w_ref, o_ref):
    o_ref[...] = _dot(x_ref[...], w_ref[0].astype(BF16)).astype(o_ref.dtype)


def matmul_w32(x, w, layer, *, bn, out_dtype=F32):
    m, k = x.shape
    n = w.shape[2]
    return pl.pallas_call(
        _mm_w32_kernel,
        grid=(n // bn,),
        in_specs=[pl.BlockSpec((m, k), lambda j: (0, 0)), pl.BlockSpec((1, k, bn), lambda j: (layer, 0, j))],
        out_specs=pl.BlockSpec((m, bn), lambda j: (0, j)),
        out_shape=SDS((m, n), out_dtype),
        compiler_params=_cparams(1),
        name="matmul_w32",
    )(x, w)


def _mm_cast_kernel(x_ref, w_ref, o_ref, wb_ref):
    wb_ref[...] = w_ref[0].astype(BF16)
    o_ref[...] = _dot(x_ref[...], wb_ref[...]).astype(o_ref.dtype)


def matmul_cast(x, w, layer, *, bn, out_dtype=F32):
    m, k = x.shape
    n = w.shape[2]
    return pl.pallas_call(
        _mm_cast_kernel,
        grid=(n // bn,),
        in_specs=[pl.BlockSpec((m, k), lambda j: (0, 0)), pl.BlockSpec((1, k, bn), lambda j: (layer, 0, j))],
        out_specs=[pl.BlockSpec((m, bn), lambda j: (0, j)), pl.BlockSpec((k, bn), lambda j: (0, j))],
        out_shape=[SDS((m, n), out_dtype), SDS((k, n), BF16)],
        compiler_params=_cparams(1),
        name="matmul_cast",
    )(x, w)


def _glu_kernel(z_ref, zt_ref, w_ref, o_ref):
    a = _dot(z_ref[...], w_ref[...])
    o_ref[...] = (zt_ref[...].astype(F32) * jax.nn.sigmoid(a)).astype(o_ref.dtype)


def glu(z, w, *, bm, bn):
    m, k = z.shape
    n = w.shape[1]
    return pl.pallas_call(
        _glu_kernel,
        grid=(m // bm, n // bn),
        in_specs=[pl.BlockSpec((bm, k), lambda i, j: (i, 0)), pl.BlockSpec((bm, bn), lambda i, j: (i, j)),
                  pl.BlockSpec((k, bn), lambda i, j: (0, j))],
        out_specs=pl.BlockSpec((bm, bn), lambda i, j: (i, j)),
        out_shape=SDS((m, n), BF16),
        compiler_params=_cparams(2),
        name="glu",
    )(z, z, w)


def _proj_ln_kernel(*refs, n_in):
    xs = refs[:n_in]
    w_ref, res_ref, g_ref, b_ref, y_ref, yb_ref = refs[n_in:]
    _proj_res_ln([x_ref[...] for x_ref in xs], w_ref, res_ref, g_ref, b_ref, y_ref, yb_ref)


def proj_res_ln(xs, w, res, g, b, *, bm):
    m, d = res.shape
    k = w.shape[0]
    row = lambda i: (i, 0)
    fixed = lambda i: (0, 0)
    return pl.pallas_call(
        functools.partial(_proj_ln_kernel, n_in=len(xs)),
        grid=(m // bm,),
        in_specs=[pl.BlockSpec((bm, x.shape[1]), row) for x in xs] + [
            pl.BlockSpec((k, d), fixed, pipeline_mode=pl.Buffered(1)),
            pl.BlockSpec((bm, d), row),
            pl.BlockSpec((1, d), fixed),
            pl.BlockSpec((1, d), fixed),
        ],
        out_specs=[pl.BlockSpec((bm, d), row), pl.BlockSpec((bm, d), row)],
        out_shape=[SDS((m, d), F32), SDS((m, d), BF16)],
        compiler_params=_cparams(1),
        name="proj_res_ln",
    )(*xs, w, res, g.reshape(1, d), b.reshape(1, d))


def _hg_out_kernel(o_ref, gate_ref, ng_ref, w_ref, res_ref, g_ref, b_ref, y_ref, yb_ref):
    o = o_ref[...]
    xn = o * lax.rsqrt(jnp.mean(o * o, axis=-1, keepdims=True) + RMS_EPS) * ng_ref[...]
    xn = xn * jax.nn.sigmoid(gate_ref[...])
    _proj_res_ln([xn.astype(BF16)], w_ref, res_ref, g_ref, b_ref, y_ref, yb_ref)


def hg_out_ln(o, h_all, norm_g, w, res, g, b, *, bm):
    m, d = res.shape
    row = lambda i: (i, 0)
    fixed = lambda i: (0, 0)
    return pl.pallas_call(
        _hg_out_kernel,
        grid=(m // bm,),
        in_specs=[
            pl.BlockSpec((bm, d), row),
            pl.BlockSpec((bm, d), lambda i: (i, 3)),
            pl.BlockSpec((1, d), fixed),
            pl.BlockSpec((d, d), fixed, pipeline_mode=pl.Buffered(1)),
            pl.BlockSpec((bm, d), row),
            pl.BlockSpec((1, d), fixed),
            pl.BlockSpec((1, d), fixed),
        ],
        out_specs=[pl.BlockSpec((bm, d), row), pl.BlockSpec((bm, d), row)],
        out_shape=[SDS((m, d), F32), SDS((m, d), BF16)],
        compiler_params=_cparams(1),
        name="hg_out_ln",
    )(o, h_all, norm_g.reshape(1, d), w, res, g.reshape(1, d), b.reshape(1, d))


def _ffn_step(f, x_ref, wgu_ref, wd_ref, g_ref, b_ref, y_ref, yb_ref):
    @pl.when(f == 0)
    def _():
        x = x_ref[...]
        yb_ref[...] = x.astype(BF16)
        y_ref[...] = ALPHA * x

    bf = wd_ref.shape[0]
    gu = _dot(yb_ref[...], wgu_ref[...])
    gate, up = gu[:, :bf], gu[:, bf:]
    h = (gate * jax.nn.sigmoid(gate) * up).astype(BF16)
    y_ref[...] += _dot(h, wd_ref[...])

    @pl.when(f == pl.num_programs(1) - 1)
    def _():
        g, b = g_ref[...], b_ref[...]
        chunk = math.gcd(y_ref.shape[0], LN_ROW_CHUNK)

        def ln_chunk(i, _):
            rows = pl.ds(pl.multiple_of(i * chunk, chunk), chunk)
            y = _layer_norm_rows(y_ref[rows, :], g, b)
            y_ref[rows, :] = y
            yb_ref[rows, :] = y.astype(BF16)
            return 0

        lax.fori_loop(0, y_ref.shape[0] // chunk, ln_chunk, 0)


def _ffn_kernel(x_ref, wgu_ref, wd_ref, g_ref, b_ref, y_ref, yb_ref):
    _ffn_step(pl.program_id(1), x_ref, wgu_ref.at[0], wd_ref, g_ref, b_ref, y_ref, yb_ref)


def ffn_ln(x, wgu, wd, g, b, *, bm):
    m, d = x.shape
    nf, _, bf2 = wgu.shape
    return pl.pallas_call(
        _ffn_kernel,
        grid=(m // bm, nf),
        in_specs=[
            pl.BlockSpec((bm, d), lambda i, f: (i, 0)),
            pl.BlockSpec((1, d, bf2), lambda i, f: (f, 0, 0)),
            pl.BlockSpec((bf2 // 2, d), lambda i, f: (f, 0)),
            pl.BlockSpec((1, d), lambda i, f: (0, 0)),
            pl.BlockSpec((1, d), lambda i, f: (0, 0)),
        ],
        out_specs=[pl.BlockSpec((bm, d), lambda i, f: (i, 0)), pl.BlockSpec((bm, d), lambda i, f: (i, 0))],
        out_shape=[SDS((m, d), F32), SDS((m, d), BF16)],
        compiler_params=_cparams(2, vmem=VMEM_LIMIT_MAX),
        name="ffn_ln",
    )(x, wgu, wd, g.reshape(1, d), b.reshape(1, d))


def _ffn_cast_kernel(x_ref, wg_ref, wu_ref, wd_ref, g_ref, b_ref, y_ref, yb_ref, wgub_ref, wdb_ref):
    bf = wd_ref.shape[1]
    wgub_ref[0, :, :bf] = wg_ref[0].astype(BF16)
    wgub_ref[0, :, bf:] = wu_ref[0].astype(BF16)
    wdb_ref[...] = wd_ref[0].astype(BF16)
    _ffn_step(pl.program_id(1), x_ref, wgub_ref.at[0], wdb_ref, g_ref, b_ref, y_ref, yb_ref)


def ffn_ln_cast(x, wg, wu, wd, layer, g, b, *, bf=256):
    m, d = x.shape
    dff = wg.shape[2]
    fixed = lambda i, f: (0, 0)
    return pl.pallas_call(
        _ffn_cast_kernel,
        grid=(1, dff // bf),
        in_specs=[
            pl.BlockSpec((m, d), fixed),
            pl.BlockSpec((1, d, bf), lambda i, f: (layer, 0, f)),
            pl.BlockSpec((1, d, bf), lambda i, f: (layer, 0, f)),
            pl.BlockSpec((1, bf, d), lambda i, f: (layer, f, 0)),
            pl.BlockSpec((1, d), fixed),
            pl.BlockSpec((1, d), fixed),
        ],
        out_specs=[
            pl.BlockSpec((m, d), fixed),
            pl.BlockSpec((m, d), fixed),
            pl.BlockSpec((1, d, 2 * bf), lambda i, f: (f, 0, 0)),
            pl.BlockSpec((bf, d), lambda i, f: (f, 0)),
        ],
        out_shape=[SDS((m, d), F32), SDS((m, d), BF16), SDS((dff // bf, d, 2 * bf), BF16), SDS((dff, d), BF16)],
        compiler_params=_cparams(2),
        name="ffn_ln_cast",
    )(x, wg, wu, wd, g.reshape(1, d), b.reshape(1, d))


def _cmul(ar, ai, br, bi):
    return ar * br - ai * bi, ar * bi + ai * br


def _s5_coefs(a_r, a_i, period):
    l = a_r.shape[1]
    pows = [(a_r, a_i)]
    for _ in range(period - 1):
        pows.append(_cmul(pows[-1][0], pows[-1][1], a_r, a_i))
    t = lax.broadcasted_iota(jnp.int32, (8, l), 0) & (period - 1)
    shifts = []
    k = 1
    while k < period:
        keep = t >= k
        shifts.append((jnp.where(keep, pows[k - 1][0], 0.0), jnp.where(keep, pows[k - 1][1], 0.0)))
        k *= 2
    p_r = jnp.broadcast_to(pows[0][0], (8, l))
    p_i = jnp.broadcast_to(pows[0][1], (8, l))
    for j in range(1, period):
        p_r = jnp.where(t == j, pows[j][0], p_r)
        p_i = jnp.where(t == j, pows[j][1], p_i)
    return shifts, (p_r, p_i)


def _s5_scan_tile(x_r, x_i, shifts):
    k = 1
    for c_r, c_i in shifts:
        s_r = pltpu.roll(x_r, k, 0)
        s_i = pltpu.roll(x_i, k, 0)
        x_r, x_i = x_r + c_r * s_r - c_i * s_i, x_i + c_r * s_i + c_i * s_r
        k *= 2
    return x_r, x_i


S5_CHAINS = 4
S5_LANE_BLOCKS = 2 * S5_HW // 128


def _s5_prompt_kernel(u_ref, bw_ref, cw_ref, ar_ref, ai_ref, d_ref, z_ref, hfin_ref, h_scr, carry_ref):
    tt = pl.program_id(2)
    n_tiles = h_scr.shape[1]
    tb = n_tiles * 8
    half = S5_LANE_BLOCKS // 2
    chains = range(S5_CHAINS)

    @pl.when(tt == 0)
    def _():
        carry_ref[...] = jnp.zeros_like(carry_ref)

    for ch in chains:
        bu = _dot(u_ref[0, :, ch * S5_UW:(ch + 1) * S5_UW].astype(BF16), bw_ref[ch])
        for s in range(S5_LANE_BLOCKS):
            h_scr[ch, :, s * 8:(s + 1) * 8, :] = bu[:, s * 128:(s + 1) * 128].reshape(n_tiles, 8, 128)
    a = [(ar_ref[ch], ai_ref[ch]) for ch in chains]

    def tile(i, carry):
        hs = list(carry)
        for r in range(8):
            for ch in chains:
                (a_r, a_i), (h_r, h_i) = a[ch], hs[ch]
                re_rows = pl.ds(r, half, stride=8)
                im_rows = pl.ds(8 * half + r, half, stride=8)
                n_r = a_r * h_r - a_i * h_i + h_scr[ch, i, re_rows, :]
                n_i = a_r * h_i + a_i * h_r + h_scr[ch, i, im_rows, :]
                h_scr[ch, i, re_rows, :] = n_r
                h_scr[ch, i, im_rows, :] = n_i
                hs[ch] = (n_r, n_i)
        return tuple(hs)

    final = lax.fori_loop(0, n_tiles, tile, tuple((carry_ref[ch, 0], carry_ref[ch, 1]) for ch in chains))
    for ch in chains:
        carry_ref[ch, 0], carry_ref[ch, 1] = final[ch]
    for ch in chains:
        cols = slice(ch * S5_UW, (ch + 1) * S5_UW)
        h = jnp.concatenate([h_scr[ch, :, s * 8:(s + 1) * 8, :].reshape(tb, 128) for s in range(S5_LANE_BLOCKS)],
                            axis=1)
        y = _dot(h.astype(BF16), cw_ref[ch]) + d_ref[ch] * u_ref[0, :, cols]
        z_ref[0, :, cols] = jax.nn.gelu(y).astype(z_ref.dtype)

    @pl.when(tt == pl.num_programs(2) - 1)
    def _():
        hfin_ref[0] = carry_ref[...]


def s5_prompt(h_all, bw, cw, a_r, a_i, d_skip, *, tb):
    bsz, t, _ = h_all.shape
    nc = S5_CHAINS
    blk = lambda b, g, s: (g, 0, 0)
    rows = pl.BlockSpec((1, tb, nc * S5_UW), lambda b, g, s: (b, s, g))
    return pl.pallas_call(
        _s5_prompt_kernel,
        grid=(bsz, S5_NGB // nc, t // tb),
        in_specs=[
            rows,
            pl.BlockSpec((nc, S5_UW, 2 * S5_HW), blk),
            pl.BlockSpec((nc, 2 * S5_HW, S5_UW), blk),
            pl.BlockSpec((nc, 8, 128), blk),
            pl.BlockSpec((nc, 8, 128), blk),
            pl.BlockSpec((nc, 1, S5_UW), blk),
        ],
        out_specs=[rows, pl.BlockSpec((1, nc, 2, 8, 128), lambda b, g, s: (b, g, 0, 0, 0))],
        out_shape=[SDS((bsz, t, D_S5), BF16), SDS((bsz, S5_NGB, 2, 8, 128), F32)],
        scratch_shapes=[pltpu.VMEM((nc, tb // 8, S5_LANE_BLOCKS * 8, 128), F32), pltpu.VMEM((nc, 2, 8, 128), F32)],
        compiler_params=_cparams(3),
        name="s5_prompt",
    )(h_all, bw, cw, a_r.reshape(S5_NGB, 8, 128), a_i.reshape(S5_NGB, 8, 128), d_skip)


def _s5_sample_kernel(u_ref, h0_ref, bw_ref, cw_ref, ar_ref, ai_ref, d_ref, z_ref, h_ref, *, period):
    u = u_ref[...]
    bu = _dot(u.astype(BF16), bw_ref[0])
    shifts, (p_r, p_i) = _s5_coefs(ar_ref[0], ai_ref[0], period)
    for i in range(u.shape[0] // 8):
        rows = slice(i * 8, (i + 1) * 8)
        x_r, x_i = _s5_scan_tile(bu[rows, 0:S5_HW], bu[rows, S5_HW:2 * S5_HW], shifts)
        h0_r = h0_ref[rows, 0:S5_HW]
        h0_i = h0_ref[rows, S5_HW:2 * S5_HW]
        h_ref[rows, 0:S5_HW] = x_r + p_r * h0_r - p_i * h0_i
        h_ref[rows, S5_HW:2 * S5_HW] = x_i + p_r * h0_i + p_i * h0_r
    y = _dot(h_ref[...].astype(BF16), cw_ref[0]) + d_ref[0] * u
    z_ref[...] = jax.nn.gelu(y).astype(z_ref.dtype)


def s5_sample(h_all, h0x, bw, cw, a_r, a_i, d_skip, *, period):
    m = h_all.shape[0]
    blk = lambda g: (g, 0, 0)
    return pl.pallas_call(
        functools.partial(_s5_sample_kernel, period=period),
        grid=(S5_NGB,),
        in_specs=[
            pl.BlockSpec((m, S5_UW), lambda g: (0, g)),
            pl.BlockSpec((m, 2 * S5_HW), lambda g: (0, g)),
            pl.BlockSpec((1, S5_UW, 2 * S5_HW), blk),
            pl.BlockSpec((1, 2 * S5_HW, S5_UW), blk),
            pl.BlockSpec((1, 1, S5_HW), blk),
            pl.BlockSpec((1, 1, S5_HW), blk),
            pl.BlockSpec((1, 1, S5_UW), blk),
        ],
        out_specs=[pl.BlockSpec((m, S5_UW), lambda g: (0, g)), pl.BlockSpec((m, 2 * S5_HW), lambda g: (0, g))],
        out_shape=[SDS((m, D_S5), BF16), SDS((m, S5_NGB * 2 * S5_HW), F32)],
        compiler_params=_cparams(1),
        name="s5_sample",
    )(h_all, h0x, bw, cw, a_r, a_i, d_skip)


def _bias_kernel(rb_ref, o_ref):
    h = pl.program_id(0)
    q = lax.broadcasted_iota(jnp.int32, (WINDOW, 2 * WINDOW), 0)
    k = lax.broadcasted_iota(jnp.int32, (WINDOW, 2 * WINDOW), 1)
    n = jnp.maximum(q + WINDOW - k, 0)
    max_exact = N_BUCKETS // 2
    nf = jnp.maximum(n, 1).astype(F32)
    large = max_exact + (jnp.log(nf / max_exact) / math.log(MAX_DISTANCE / max_exact)
                         * (N_BUCKETS - max_exact)).astype(jnp.int32)
    large = jnp.minimum(large, N_BUCKETS - 1)
    bucket = jnp.where(n < max_exact, n, large)
    out = jnp.zeros((WINDOW, 2 * WINDOW), F32)
    for b in range(N_BUCKETS):
        out = jnp.where(bucket == b, rb_ref[b, h], out)
    o_ref[0] = out


def bias_table(rel_bias):
    return pl.pallas_call(
        _bias_kernel,
        grid=(SWA_HEADS,),
        in_specs=[pl.BlockSpec(memory_space=pltpu.SMEM)],
        out_specs=pl.BlockSpec((1, WINDOW, 2 * WINDOW), lambda h: (h, 0, 0)),
        out_shape=SDS((SWA_HEADS, WINDOW, 2 * WINDOW), F32),
        compiler_params=_cparams(1),
        name="bias_table",
    )(rel_bias)


def _swa_softmax(s, valid, sink_col):
    s = jnp.where(valid, s, NEG_BIG)
    m = jnp.maximum(jnp.max(s, axis=-1, keepdims=True), sink_col)
    p = jnp.exp(s - m)
    return p / (jnp.sum(p, axis=-1, keepdims=True) + jnp.exp(sink_col - m))


def _sink_col(sinks_ref, first_head, rows_per_head):
    n = SWA_GQ * rows_per_head
    r = lax.broadcasted_iota(jnp.int32, (n, 1), 0)
    col = jnp.full((n, 1), sinks_ref[first_head], F32)
    for g in range(1, SWA_GQ):
        col = jnp.where(r >= g * rows_per_head, sinks_ref[first_head + g], col)
    return col


SWA_KV_PER_STEP = 4


def _swa_prompt_kernel(sinks_ref, q_ref, kp_ref, kc_ref, vp_ref, vc_ref, bias_ref, o_ref):
    nkv = SWA_KV_PER_STEP
    part = pl.program_id(0)
    i = pl.program_id(2)
    blk = WINDOW
    q = q_ref[0].astype(BF16)
    kk = jnp.concatenate([kp_ref[0], kc_ref[0]], axis=0).astype(BF16)
    vv = jnp.concatenate([vp_ref[0], vc_ref[0]], axis=0).astype(BF16)
    qi = lax.broadcasted_iota(jnp.int32, (SWA_GQ * blk, 2 * blk), 0) & (blk - 1)
    ki = lax.broadcasted_iota(jnp.int32, (SWA_GQ * blk, 2 * blk), 1)
    valid = (ki > qi) & (ki <= qi + blk) & ((ki >= blk) | (i > 0))
    heads = [slice(j * HEAD_DIM, (j + 1) * HEAD_DIM) for j in range(nkv)]
    q4 = [jnp.concatenate([q[:, (j * SWA_GQ + g) * HEAD_DIM:(j * SWA_GQ + g + 1) * HEAD_DIM]
                           for g in range(SWA_GQ)], axis=0) for j in range(nkv)]
    s = [_dot_nt(q4[j], kk[:, c]) * (HEAD_DIM ** -0.5) for j, c in enumerate(heads)]
    p = [_swa_softmax(s[j] + bias_ref[j * SWA_GQ:(j + 1) * SWA_GQ].reshape(SWA_GQ * blk, 2 * blk), valid,
                      _sink_col(sinks_ref, (part * nkv + j) * SWA_GQ, blk)).astype(BF16) for j in range(nkv)]
    o4 = [_dot(p[j], vv[:, c]) for j, c in enumerate(heads)]
    o_ref[0] = jnp.concatenate([o4[j][g * blk:(g + 1) * blk] for j in range(nkv) for g in range(SWA_GQ)],
                               axis=1).astype(o_ref.dtype)


def swa_prompt(h_all, sinks, table):
    bsz, t, _ = h_all.shape
    nblk = t // WINDOW
    nkv = SWA_KV_PER_STEP
    qw = nkv * SWA_GQ * HEAD_DIM
    kw = nkv * HEAD_DIM
    q0, k0, v0 = D_S5 // qw, (D_S5 + D_SWA_Q) // kw, (D_S5 + D_SWA_Q + D_SWA_KV) // kw
    cur = lambda c0: (lambda p, b, i: (b, i, c0 + p))
    prev = lambda c0: (lambda p, b, i: (b, jnp.maximum(i - 1, 0), c0 + p))
    return pl.pallas_call(
        _swa_prompt_kernel,
        grid=(SWA_KV_HEADS // nkv, bsz, nblk),
        in_specs=[
            pl.BlockSpec(memory_space=pltpu.SMEM),
            pl.BlockSpec((1, WINDOW, qw), cur(q0)),
            pl.BlockSpec((1, WINDOW, kw), prev(k0)),
            pl.BlockSpec((1, WINDOW, kw), cur(k0)),
            pl.BlockSpec((1, WINDOW, kw), prev(v0)),
            pl.BlockSpec((1, WINDOW, kw), cur(v0)),
            pl.BlockSpec((nkv * SWA_GQ, WINDOW, 2 * WINDOW), lambda p, b, i: (p, 0, 0)),
        ],
        out_specs=pl.BlockSpec((1, WINDOW, qw), lambda p, b, i: (b, i, p)),
        out_shape=SDS((bsz, t, D_SWA_Q), BF16),
        compiler_params=_cparams(3),
        name="swa_prompt",
    )(sinks, h_all, h_all, h_all, h_all, h_all, table)


def _swa_sample_kernel(sinks_ref, q_ref, kk_ref, vv_ref, bias_ref, o_ref, *, t_new):
    rows = q_ref.shape[0]
    n_keys = kk_ref.shape[1]
    q = q_ref[...].astype(BF16)
    r = lax.broadcasted_iota(jnp.int32, (SWA_GQ * rows, n_keys), 0)
    ti = r & (t_new - 1)
    ki = lax.broadcasted_iota(jnp.int32, (SWA_GQ * rows, n_keys), 1)
    valid = (ki > ti) & (ki <= ti + WINDOW)
    second = (r & (rows - 1)) >= t_new
    second_o = (lax.broadcasted_iota(jnp.int32, (SWA_GQ * rows, HEAD_DIM), 0) & (rows - 1)) >= t_new
    nkv = SWA_KV_HEADS
    heads = [slice(j * HEAD_DIM, (j + 1) * HEAD_DIM) for j in range(nkv)]
    kk = [kk_ref[bb].astype(BF16) for bb in range(2)]
    vv = [vv_ref[bb].astype(BF16) for bb in range(2)]
    q4 = [jnp.concatenate([q[:, (j * SWA_GQ + g) * HEAD_DIM:(j * SWA_GQ + g + 1) * HEAD_DIM]
                           for g in range(SWA_GQ)], axis=0) for j in range(nkv)]
    s_seq = [[_dot_nt(q4[j], kk[bb][:, c]) for j, c in enumerate(heads)] for bb in range(2)]
    p = []
    for j in range(nkv):
        s = jnp.where(second, s_seq[1][j], s_seq[0][j]) * (HEAD_DIM ** -0.5)
        s = s + bias_ref[j * SWA_GQ:(j + 1) * SWA_GQ].reshape(SWA_GQ * rows, n_keys)
        p.append(_swa_softmax(s, valid, _sink_col(sinks_ref, j * SWA_GQ, rows)).astype(BF16))
    o_seq = [[_dot(p[j], vv[bb][:, c]) for j, c in enumerate(heads)] for bb in range(2)]
    o4 = [jnp.where(second_o, o_seq[1][j], o_seq[0][j]) for j in range(nkv)]
    o_ref[...] = jnp.concatenate([o4[j][g * rows:(g + 1) * rows] for j in range(nkv) for g in range(SWA_GQ)],
                                 axis=1).astype(o_ref.dtype)


def swa_sample(h_all, kk, vv, sinks, bias_s, *, t_new):
    m = h_all.shape[0]
    n_keys = kk.shape[1]
    rows = 2 * t_new
    return pl.pallas_call(
        functools.partial(_swa_sample_kernel, t_new=t_new),
        grid=(m // rows,),
        in_specs=[
            pl.BlockSpec(memory_space=pltpu.SMEM),
            pl.BlockSpec((rows, D_SWA_Q), lambda i: (i, D_S5 // D_SWA_Q)),
            pl.BlockSpec((2, n_keys, D_SWA_KV), lambda i: (i, 0, 0)),
            pl.BlockSpec((2, n_keys, D_SWA_KV), lambda i: (i, 0, 0)),
            pl.BlockSpec((SWA_HEADS, rows, n_keys), lambda i: (0, 0, 0)),
        ],
        out_specs=pl.BlockSpec((rows, D_SWA_Q), lambda i: (i, 0)),
        out_shape=SDS((m, D_SWA_Q), BF16),
        compiler_params=_cparams(1),
        name="swa_sample",
    )(sinks, h_all, kk, vv, bias_s)


def _hg_gates(q, fz, lb):
    qs = q * jax.nn.sigmoid(q)
    f = lb + (1.0 - lb) * jax.nn.sigmoid(fz)
    return qs, jnp.log(f), 1.0 - f


HG_GROUP = 4


def _hgrn_prompt_kernel(q_ref, f_ref, v_ref, lb_ref, o_ref, st_ref, s_scr, *, n_heads):
    tt = pl.program_id(2)
    n_rows = q_ref.shape[1]
    c, ng = HG_CHUNK, HG_GROUP
    gr = c * ng

    @pl.when(tt == 0)
    def _():
        s_scr[...] = jnp.zeros_like(s_scr)

    ri = lax.broadcasted_iota(jnp.int32, (gr, gr), 0)
    ci = lax.broadcasted_iota(jnp.int32, (gr, gr), 1)
    causal = (ri >= ci) & ((ri // c) == (ci // c))
    t_in = lax.broadcasted_iota(jnp.int32, (gr, HG_DK), 0) & (c - 1)
    own = ((lax.broadcasted_iota(jnp.int32, (gr, ng * HG_DK), 0) // c)
           == (lax.broadcasted_iota(jnp.int32, (gr, ng * HG_DK), 1) // HG_DK)).astype(BF16)

    def body(gi, _):
        r0 = pl.multiple_of(gi * gr, gr)
        rows = pl.ds(r0, gr)
        hs = range(n_heads)
        heads = [slice(hd * HG_DK, (hd + 1) * HG_DK) for hd in hs]
        v, qt, kt, kl, decay = [], [], [], [], []
        for cols in heads:
            qs, lf, k = _hg_gates(q_ref[0, rows, cols], f_ref[0, rows, cols], lb_ref[:, cols])
            v.append(v_ref[0, rows, cols].astype(BF16))
            cum = lf
            sh = 1
            while sh < c:
                cum = cum + jnp.where(t_in >= sh, pltpu.roll(cum, sh, 0), 0.0)
                sh *= 2
            last3 = cum.reshape(ng, c, HG_DK)[:, c - 1:c, :]
            last = jnp.broadcast_to(last3, (ng, c, HG_DK)).reshape(gr, HG_DK)
            qt.append((qs * jnp.exp(cum)).astype(BF16))
            kt.append((k * jnp.exp(-cum)).astype(BF16))
            kl.append((k * jnp.exp(last - cum)).astype(BF16))
            decay.append(jnp.exp(last3))
        kv = [_dot_tn(v[h], jnp.concatenate([kl[h]] * ng, axis=1) * own) for h in hs]
        attn = [jnp.where(causal, _dot_nt(qt[h], kt[h]), 0.0).astype(BF16) for h in hs]
        s_cat = []
        for h in hs:
            st = s_scr[h]
            starts = []
            for j in range(ng):
                starts.append(st)
                st = st * decay[h][j] + kv[h][:, j * HG_DK:(j + 1) * HG_DK]
            s_scr[h] = st
            s_cat.append(jnp.concatenate(starts, axis=1).astype(BF16))
        intra = [_dot(attn[h], v[h]) for h in hs]
        inter = [_dot_nt(jnp.concatenate([qt[h]] * ng, axis=1) * own, s_cat[h]) for h in hs]
        for h, cols in enumerate(heads):
            o_ref[0, rows, cols] = intra[h] + inter[h]
        return 0

    lax.fori_loop(0, n_rows // gr, body, 0, unroll=True)

    @pl.when(tt == pl.num_programs(2) - 1)
    def _():
        st_ref[0] = s_scr[...]


def hgrn_prompt(h_all, lb, *, tb, n_heads=2):
    bsz, t, _ = h_all.shape
    hw = n_heads * HG_DK
    nhb = D_MODEL // hw
    col = lambda c0: (lambda b, h, s: (b, s, c0 * nhb + h))
    return pl.pallas_call(
        functools.partial(_hgrn_prompt_kernel, n_heads=n_heads),
        grid=(bsz, nhb, t // tb),
        in_specs=[
            pl.BlockSpec((1, tb, hw), col(0)),
            pl.BlockSpec((1, tb, hw), col(1)),
            pl.BlockSpec((1, tb, hw), col(2)),
            pl.BlockSpec((1, hw), lambda b, h, s: (0, h)),
        ],
        out_specs=[
            pl.BlockSpec((1, tb, hw), lambda b, h, s: (b, s, h)),
            pl.BlockSpec((1, n_heads, HG_DK, HG_DK), lambda b, h, s: (b, h, 0, 0)),
        ],
        out_shape=[SDS((bsz, t, D_MODEL), F32), SDS((bsz, HG_HEADS, HG_DK, HG_DK), F32)],
        scratch_shapes=[pltpu.VMEM((n_heads, HG_DK, HG_DK), F32)],
        compiler_params=_cparams(3),
        name="hgrn_prompt",
    )(h_all, h_all, h_all, lb.reshape(1, D_MODEL))


def _hgrn_sample_kernel(q_ref, f_ref, v_ref, lb_ref, s0_ref, o_ref, s_ref, *, n_heads, t_new):
    rows = q_ref.shape[0]
    ri = lax.broadcasted_iota(jnp.int32, (rows, rows), 0)
    ci = lax.broadcasted_iota(jnp.int32, (rows, rows), 1)
    causal = (ri >= ci) & ((ri >= t_new) == (ci >= t_new))
    tril = causal.astype(F32)
    hw = n_heads * HG_DK
    second_w = lax.broadcasted_iota(jnp.int32, (rows, hw), 0) >= t_new
    second = second_w[:, :HG_DK]
    eye = (lax.broadcasted_iota(jnp.int32, (HG_DK, HG_DK), 0)
           == lax.broadcasted_iota(jnp.int32, (HG_DK, HG_DK), 1))
    heads = [slice(hd * HG_DK, (hd + 1) * HG_DK) for hd in range(n_heads)]
    qs, lf, k = _hg_gates(q_ref[...], f_ref[...], lb_ref[...])
    v = v_ref[...].astype(BF16)
    cum = jnp.dot(tril, lf, preferred_element_type=F32, precision=lax.Precision.HIGHEST)
    last0 = cum[t_new - 1:t_new]
    last1 = cum[rows - 1:rows]
    qt = (qs * jnp.exp(cum)).astype(BF16)
    kt = (k * jnp.exp(-cum)).astype(BF16)
    kl = k * jnp.exp(jnp.where(second_w, last1, last0) - cum)
    kl_seq = (jnp.where(second_w, 0.0, kl).astype(BF16), jnp.where(second_w, kl, 0.0).astype(BF16))
    decay_seq = (jnp.exp(last0), jnp.exp(last1))
    attn = [jnp.where(causal, _dot_nt(qt[:, c], kt[:, c]), 0.0).astype(BF16) for c in heads]
    inter = [[_dot(qt[:, c], s0_ref[bb, hd].astype(BF16)) for hd, c in enumerate(heads)] for bb in range(2)]
    intra = [_dot(a, v[:, c]) for a, c in zip(attn, heads)]
    for hd, c in enumerate(heads):
        o_ref[:, c] = intra[hd] + jnp.where(second, inter[1][hd], inter[0][hd])
    for bb in range(2):
        kv = [_dot_tn(kl_seq[bb][:, c], v[:, c]) for c in heads]
        for hd, c in enumerate(heads):
            decay = jnp.sum(jnp.where(eye, decay_seq[bb][:, c], 0.0), axis=1, keepdims=True)
            s_ref[bb, hd] = decay * s0_ref[bb, hd] + kv[hd]


def hgrn_sample(h_all, lb, s0, *, t_new, n_heads=8):
    m = h_all.shape[0]
    rows = 2 * t_new
    hw = n_heads * HG_DK
    nhb = D_MODEL // hw
    col = lambda c0: (lambda i, h: (i, c0 * nhb + h))
    return pl.pallas_call(
        functools.partial(_hgrn_sample_kernel, n_heads=n_heads, t_new=t_new),
        grid=(m // rows, nhb),
        in_specs=[
            pl.BlockSpec((rows, hw), col(0)),
            pl.BlockSpec((rows, hw), col(1)),
            pl.BlockSpec((rows, hw), col(2)),
            pl.BlockSpec((1, hw), lambda i, h: (0, h)),
            pl.BlockSpec((2, n_heads, HG_DK, HG_DK), lambda i, h: (i, h, 0, 0)),
        ],
        out_specs=[
            pl.BlockSpec((rows, hw), lambda i, h: (i, h)),
            pl.BlockSpec((2, n_heads, HG_DK, HG_DK), lambda i, h: (i, h, 0, 0)),
        ],
        out_shape=[SDS((m, D_MODEL), F32), SDS(s0.shape, F32)],
        compiler_params=_cparams(2),
        name="hgrn_sample",
    )(h_all, h_all, h_all, lb.reshape(1, D_MODEL), s0)


def _mem_attention(q, mk_ref, mv_ref, n_seq):
    rows = q.shape[0]
    second = lax.broadcasted_iota(jnp.int32, (rows, 1), 0) >= rows // 2

    def mem_head(ref, seq, h):
        if len(ref.shape) == 3:
            return ref[seq, :, h * MEM_HEAD_DIM:(h + 1) * MEM_HEAD_DIM].astype(BF16)
        return ref[0, seq, :, h, :].astype(BF16)

    hs = range(MEM_HEADS)
    qh = [q[:, h * MEM_HEAD_DIM:(h + 1) * MEM_HEAD_DIM] for h in hs]
    s_seq = [[_dot_nt(qh[h], mem_head(mk_ref, seq, h)) for h in hs] for seq in range(n_seq)]
    p = []
    for h in hs:
        s = s_seq[0][h] if n_seq == 1 else jnp.where(second, s_seq[1][h], s_seq[0][h])
        s = s * (MEM_HEAD_DIM ** -0.5)
        e = jnp.exp(s - jnp.max(s, axis=-1, keepdims=True))
        p.append((e / jnp.sum(e, axis=-1, keepdims=True)).astype(BF16))
    o_seq = [[_dot(p[h], mem_head(mv_ref, seq, h)) for h in hs] for seq in range(n_seq)]
    heads = [(o_seq[0][h] if n_seq == 1 else jnp.where(second, o_seq[1][h], o_seq[0][h])).astype(BF16) for h in hs]
    return jnp.concatenate(heads, axis=1)


def _xattn_kernel(y_ref, yb_ref, wq_ref, mk_ref, mv_ref, wo_ref, g_ref, b_ref, o_ref, ob_ref, *, n_seq):
    q = _dot(yb_ref[...], wq_ref[...]).astype(BF16)
    att = _mem_attention(q, mk_ref, mv_ref, n_seq)
    _proj_res_ln([att], wo_ref, y_ref, g_ref, b_ref, o_ref, ob_ref)


def _mem_attn_kernel(q_ref, mk_ref, mv_ref, o_ref):
    o_ref[...] = _mem_attention(q_ref[...], mk_ref, mv_ref, 2)


def mem_attn_cache(q, cache_k, cache_v, layer, *, rows_per_seq):
    m, dm = q.shape
    rows = 2 * rows_per_seq
    mem_spec = pl.BlockSpec((1, 2, N_MEM, MEM_HEADS, MEM_HEAD_DIM), lambda i: (layer, i, 0, 0, 0))
    return pl.pallas_call(
        _mem_attn_kernel,
        grid=(m // rows,),
        in_specs=[pl.BlockSpec((rows, dm), lambda i: (i, 0)), mem_spec, mem_spec],
        out_specs=pl.BlockSpec((rows, dm), lambda i: (i, 0)),
        out_shape=SDS((m, dm), BF16),
        compiler_params=_cparams(1),
        name="mem_attn_cache",
    )(q, cache_k, cache_v)


def xattn_ln(y, yb, wq, mk, mv, wo, g, b, *, bm, rows_per_seq):
    m, d = y.shape
    assert rows_per_seq % bm == 0
    row = lambda i: (i, 0)
    fixed = lambda i: (0, 0)
    mem_spec = pl.BlockSpec((1, N_MEM, D_MEM), lambda i: (i * bm // rows_per_seq, 0, 0))
    return pl.pallas_call(
        functools.partial(_xattn_kernel, n_seq=1),
        grid=(m // bm,),
        in_specs=[
            pl.BlockSpec((bm, d), row),
            pl.BlockSpec((bm, d), row),
            pl.BlockSpec((d, D_MEM), fixed),
            mem_spec,
            mem_spec,
            pl.BlockSpec((D_MEM, d), fixed),
            pl.BlockSpec((1, d), fixed),
            pl.BlockSpec((1, d), fixed),
        ],
        out_specs=[pl.BlockSpec((bm, d), row), pl.BlockSpec((bm, d), row)],
        out_shape=[SDS((m, d), F32), SDS((m, d), BF16)],
        compiler_params=_cparams(1),
        name="xattn_ln",
    )(y, yb, wq, mk, mv, wo, g.reshape(1, d), b.reshape(1, d))


def _s5_discretize(lam_re, lam_im, log_dt, b_re, b_im):
    lr = jnp.minimum(lam_re.astype(F32), -1e-4)
    li = lam_im.astype(F32)
    dt = jnp.exp(log_dt.astype(F32))[:, None]
    mag = jnp.exp(lr * dt)
    a_re = mag * jnp.cos(li * dt)
    a_im = mag * jnp.sin(li * dt)
    den = lr * lr + li * li
    fr = ((a_re - 1.0) * lr + a_im * li) / den
    fi = (a_im * lr - (a_re - 1.0) * li) / den
    br, bi = b_re.astype(F32), b_im.astype(F32)
    bb_re = fr[..., None] * br - fi[..., None] * bi
    bb_im = fr[..., None] * bi + fi[..., None] * br
    return a_re, a_im, bb_re, bb_im


def _s5_block_weights(a_re, a_im, bb_re, bb_im, c_re, c_im, d_skip):
    eye = jnp.eye(S5_GB, dtype=F32)
    shp_b = (S5_NGB, S5_GB, S5_STATE, S5_GROUP)
    shp_c = (S5_NGB, S5_GB, S5_GROUP, S5_STATE)
    blk_b = lambda w: jnp.einsum('bgph,gk->bghkp', w.reshape(shp_b), eye).reshape(S5_NGB, S5_UW, S5_HW)
    blk_c = lambda w: jnp.einsum('bghp,gk->bkpgh', w.reshape(shp_c), eye).reshape(S5_NGB, S5_HW, S5_UW)
    bw = jnp.concatenate([blk_b(bb_re), blk_b(bb_im)], axis=-1).astype(BF16)
    cw = jnp.concatenate([blk_c(c_re.astype(F32)), blk_c(-c_im.astype(F32))], axis=1).astype(BF16)
    return (bw, cw, a_re.reshape(S5_NGB, 1, S5_HW), a_im.reshape(S5_NGB, 1, S5_HW),
            d_skip.astype(F32).reshape(S5_NGB, 1, S5_UW))


def _s5_state_to_blocks(s_re, s_im):
    b = s_re.shape[0]
    return jnp.concatenate([s_re.reshape(b, S5_NGB, S5_HW), s_im.reshape(b, S5_NGB, S5_HW)],
                           axis=-1).reshape(b, S5_NGB * 2 * S5_HW)


def _s5_state_from_blocks(h):
    b = h.shape[0]
    return (h[:, :, :S5_HW].reshape(b, S5_GROUPS, S5_STATE), h[:, :, S5_HW:].reshape(b, S5_GROUPS, S5_STATE))


def kernel(x_prompt, x_sample, cache_mem_k, cache_mem_v, cache_swa_k, cache_swa_v, state_s5_re, state_s5_im, state_hgrn, mem_prompt, rel_bias, w_even_in, s5_lam_re, s5_lam_im, s5_log_dt, s5_b_re, s5_b_im, s5_c_re, s5_c_im, s5_d, s5_w_glu, swa_sinks, w_even_out, hg_lb_logits, w_odd_in, hg_norm_g, w_odd_out, w_mem_q, w_mem_k, w_mem_v, w_mem_o, w_ffn_gate, w_ffn_up, w_ffn_down, ln_g, ln_b):
    bsz, seq, d = x_prompt.shape
    dec_b, dec_t, _ = x_sample.shape
    mp, ms = bsz * seq, dec_b * dec_t
    w_buf = cache_swa_k.shape[2]
    bf = lambda w: w.astype(BF16)

    lb_soft = jax.nn.softmax(hg_lb_logits.astype(F32), axis=0)
    lower_bounds = jnp.cumsum(lb_soft, axis=0) - lb_soft[0]
    table = bias_table(rel_bias.astype(F32))
    bias_s = jnp.tile(table[:, :dec_t], (1, 2, 1))
    mem_b = bf(mem_prompt).reshape(bsz * N_MEM, d)

    yp = x_prompt.reshape(mp, d)
    ys = x_sample.reshape(ms, d)
    ypb, ysb = None, bf(ys)
    p_mem_k, p_mem_v = [], []
    p_swa_k, p_swa_v, p_s5_re, p_s5_im, p_hg = [], [], [], [], []
    s_swa_k, s_swa_v, s_s5_re, s_s5_im, s_hg = [], [], [], [], []

    for l in range(DEPTH):
        j = l // 2
        g0, b0 = ln_g[l, 0], ln_b[l, 0]
        if l % 2 == 0:
            a_re, a_im, bb_re, bb_im = _s5_discretize(s5_lam_re[j], s5_lam_im[j], s5_log_dt[j], s5_b_re[j], s5_b_im[j])
            s5w = _s5_block_weights(a_re, a_im, bb_re, bb_im, s5_c_re[j], s5_c_im[j], s5_d[j])
            w_glu, w_out = cast_layer(s5_w_glu, j), cast_layer(w_even_out, j)
            sinks = swa_sinks[j].astype(F32)
            kv0 = D_S5 + D_SWA_Q

            hs, w_in = matmul_cast(ysb, w_even_in, j, bn=512)
            hp = (matmul_x32(yp, w_in, bm=512, bn=1024) if ypb is None
                  else matmul(ypb, w_in, bm=1024, bn=1024)).reshape(bsz, seq, -1)
            z, hfin = s5_prompt(hp, *s5w, tb=512)
            s5_out = glu(z.reshape(mp, D_S5), w_glu, bm=1024, bn=1024)
            att = swa_prompt(hp, sinks, table).reshape(mp, D_SWA_Q)
            hr, hi = _s5_state_from_blocks(hfin.reshape(bsz, S5_NGB, 2 * S5_HW))
            p_s5_re.append(hr); p_s5_im.append(hi)
            p_swa_k.append(hp[:, seq - w_buf:, kv0:kv0 + D_SWA_KV].reshape(bsz, w_buf, SWA_KV_HEADS, HEAD_DIM))
            p_swa_v.append(hp[:, seq - w_buf:, kv0 + D_SWA_KV:].reshape(bsz, w_buf, SWA_KV_HEADS, HEAD_DIM))
            yp, ypb = proj_res_ln([s5_out, att], w_out, yp, g0, b0, bm=128)

            h0x = jnp.repeat(_s5_state_to_blocks(state_s5_re[j].astype(F32), state_s5_im[j].astype(F32)), dec_t, axis=0)
            z, h_steps = s5_sample(hs, h0x, *s5w, period=dec_t)
            s5_out = glu(z, w_glu, bm=ms, bn=1024)
            hs3 = hs.reshape(dec_b, dec_t, -1)
            kpad = jnp.zeros((dec_b, 2 * WINDOW - w_buf - dec_t, D_SWA_KV), F32)
            kk = jnp.concatenate([cache_swa_k[j].reshape(dec_b, w_buf, D_SWA_KV).astype(F32),
                                  hs3[:, :, kv0:kv0 + D_SWA_KV], kpad], axis=1)
            vv = jnp.concatenate([cache_swa_v[j].reshape(dec_b, w_buf, D_SWA_KV).astype(F32),
                                  hs3[:, :, kv0 + D_SWA_KV:], kpad], axis=1)
            att = swa_sample(hs, kk, vv, sinks, bias_s, t_new=dec_t)
            hr, hi = _s5_state_from_blocks(h_steps.reshape(dec_b, dec_t, S5_NGB, 2 * S5_HW)[:, dec_t - 1])
            s_s5_re.append(hr); s_s5_im.append(hi)
            s_swa_k.append(kk[:, dec_t:dec_t + w_buf].reshape(dec_b, w_buf, SWA_KV_HEADS, HEAD_DIM))
            s_swa_v.append(vv[:, dec_t:dec_t + w_buf].reshape(dec_b, w_buf, SWA_KV_HEADS, HEAD_DIM))
            ys, ysb = proj_res_ln([s5_out, att], w_out, ys, g0, b0, bm=ms)
        else:
            w_out = cast_layer(w_odd_out, j)
            lb = lower_bounds[l]

            hs, w_in = matmul_cast(ysb, w_odd_in, j, bn=512)
            hp = matmul(ypb, w_in, bm=1024, bn=1024)
            o, st = hgrn_prompt(hp.reshape(bsz, seq, -1), lb, tb=512, n_heads=4)
            p_hg.append(jnp.swapaxes(st, -1, -2))
            yp, ypb = hg_out_ln(o.reshape(mp, d), hp, hg_norm_g[j], w_out, yp, g0, b0, bm=128)

            o, s_new = hgrn_sample(hs, lb, state_hgrn[j].astype(F32), t_new=dec_t)
            s_hg.append(s_new)
            ys, ysb = hg_out_ln(o, hs, hg_norm_g[j], w_out, ys, g0, b0, bm=ms)

        wo = cast_layer(w_mem_o, l)
        mk = matmul_w32(mem_b, w_mem_k, l, bn=D_MEM).reshape(bsz, N_MEM, D_MEM)
        mv = matmul_w32(mem_b, w_mem_v, l, bn=D_MEM).reshape(bsz, N_MEM, D_MEM)
        p_mem_k.append(mk.reshape(bsz, N_MEM, MEM_HEADS, MEM_HEAD_DIM))
        p_mem_v.append(mv.reshape(bsz, N_MEM, MEM_HEADS, MEM_HEAD_DIM))
        qs, wq = matmul_cast(ysb, w_mem_q, l, bn=D_MEM, out_dtype=BF16)
        yp, ypb = xattn_ln(yp, ypb, wq, mk, mv, wo, ln_g[l, 1], ln_b[l, 1], bm=256, rows_per_seq=seq)
        att = mem_attn_cache(qs, cache_mem_k, cache_mem_v, l, rows_per_seq=dec_t)
        ys, ysb = proj_res_ln([att], wo, ys, ln_g[l, 1], ln_b[l, 1], bm=ms)

        ys, ysb, wgu, wd = ffn_ln_cast(ys, w_ffn_gate, w_ffn_up, w_ffn_down, l, ln_g[l, 2], ln_b[l, 2])
        yp, ypb = ffn_ln(yp, wgu, wd, ln_g[l, 2], ln_b[l, 2], bm=512)

    return (yp.reshape(bsz, seq, d), ys.reshape(dec_b, dec_t, d),
            jnp.stack(p_mem_k), jnp.stack(p_mem_v),
            jnp.stack(p_swa_k), jnp.stack(p_swa_v),
            jnp.stack(p_s5_re), jnp.stack(p_s5_im), jnp.stack(p_hg),
            jnp.stack(s_swa_k), jnp.stack(s_swa_v),
            jnp.stack(s_s5_re), jnp.stack(s_s5_im), jnp.stack(s_hg))
```

```python
import functools
import math

import jax
import jax.numpy as jnp
from jax import lax
from jax.experimental import pallas as pl
from jax.experimental.pallas import tpu as pltpu

F32 = jnp.float32
BF16 = jnp.bfloat16
SDS = jax.ShapeDtypeStruct

D_MODEL = 4096
DEPTH = 2
ALPHA = (2 * DEPTH) ** 0.25
LN_EPS = 1e-5
RMS_EPS = 1e-6
NEG_BIG = -1e30

S5_GROUPS, S5_GROUP, S5_STATE = 128, 16, 64
D_S5 = S5_GROUPS * S5_GROUP
S5_GB = 16
S5_NGB = S5_GROUPS // S5_GB
S5_UW = S5_GB * S5_GROUP
S5_HW = S5_GB * S5_STATE

HEAD_DIM, SWA_HEADS, SWA_KV_HEADS, SWA_GQ = 64, 32, 8, 4
D_SWA_Q, D_SWA_KV = 2048, 512
WINDOW = 128
N_BUCKETS, MAX_DISTANCE = 32, 128

HG_DK, HG_HEADS, HG_CHUNK = 128, 32, 32
N_MEM, MEM_HEADS, MEM_HEAD_DIM, D_MEM = 256, 4, 128, 512

VMEM_LIMIT = 56 * 1024 * 1024
VMEM_LIMIT_MAX = 60 * 1024 * 1024
PROJ_COL_CHUNKS = 4
LN_ROW_CHUNK = 64


def _cparams(n_axes, vmem=VMEM_LIMIT):
    return pltpu.CompilerParams(dimension_semantics=("arbitrary",) * n_axes, vmem_limit_bytes=vmem)


def _dot(a, b):
    return jnp.dot(a, b, preferred_element_type=F32)


def _dot_nt(a, b):
    return lax.dot_general(a, b, (((1,), (1,)), ((), ())), preferred_element_type=F32)


def _dot_tn(a, b):
    return lax.dot_general(a, b, (((0,), (0,)), ((), ())), preferred_element_type=F32)


def _layer_norm_rows(s, g, b):
    mu = jnp.mean(s, axis=-1, keepdims=True)
    c = s - mu
    var = jnp.mean(c * c, axis=-1, keepdims=True)
    return c * lax.rsqrt(var + LN_EPS) * g + b


def _proj_res_sum(xs, w_ref, res_ref, s_ref):
    d = s_ref.shape[1]
    cw = d // PROJ_COL_CHUNKS
    total = None
    for j in range(PROJ_COL_CHUNKS):
        cols = slice(j * cw, (j + 1) * cw)
        s = ALPHA * res_ref[:, cols]
        off = 0
        for x in xs:
            s = s + _dot(x, w_ref[off:off + x.shape[1], cols])
            off += x.shape[1]
        s_ref[:, cols] = s
        part = jnp.sum(s, axis=-1, keepdims=True)
        total = part if total is None else total + part
    return total


def _ln_rows_from(s_ref, total, g_ref, b_ref, y_ref, yb_ref):
    d = s_ref.shape[1]
    cw = d // PROJ_COL_CHUNKS
    mu = total * (1.0 / d)
    sq = None
    for j in range(PROJ_COL_CHUNKS):
        c = s_ref[:, j * cw:(j + 1) * cw] - mu
        part = jnp.sum(c * c, axis=-1, keepdims=True)
        sq = part if sq is None else sq + part
    rstd = lax.rsqrt(sq * (1.0 / d) + LN_EPS)
    for j in range(PROJ_COL_CHUNKS):
        cols = slice(j * cw, (j + 1) * cw)
        y = (s_ref[:, cols] - mu) * rstd * g_ref[:, cols] + b_ref[:, cols]
        y_ref[:, cols] = y
        yb_ref[:, cols] = y.astype(BF16)


def _proj_res_ln(xs, w_ref, res_ref, g_ref, b_ref, y_ref, yb_ref):
    total = _proj_res_sum(xs, w_ref, res_ref, y_ref)
    _ln_rows_from(y_ref, total, g_ref, b_ref, y_ref, yb_ref)


def _cast_kernel(w_ref, o_ref):
    o_ref[...] = w_ref[0].astype(o_ref.dtype)


CAST_BLOCK_BYTES = 8 * 1024 * 1024


def cast_layer(w, layer):
    _, k, n = w.shape
    rows = next(r for r in (4096, 2048, 1024, 512, 256, 128) if k % r == 0 and r * n * 4 <= CAST_BLOCK_BYTES)
    return pl.pallas_call(
        _cast_kernel,
        grid=(k // rows,),
        in_specs=[pl.BlockSpec((1, rows, n), lambda i: (layer, i, 0))],
        out_specs=pl.BlockSpec((rows, n), lambda i: (i, 0)),
        out_shape=SDS((k, n), BF16),
        compiler_params=_cparams(1),
        name="cast_layer",
    )(w)


def _mm_kernel(x_ref, w_ref, o_ref):
    o_ref[...] = _dot(x_ref[...], w_ref[...]).astype(o_ref.dtype)


def matmul(x, w, *, bm, bn, out_dtype=F32):
    m, k = x.shape
    n = w.shape[1]
    return pl.pallas_call(
        _mm_kernel,
        grid=(m // bm, n // bn),
        in_specs=[pl.BlockSpec((bm, k), lambda i, j: (i, 0)), pl.BlockSpec((k, bn), lambda i, j: (0, j))],
        out_specs=pl.BlockSpec((bm, bn), lambda i, j: (i, j)),
        out_shape=SDS((m, n), out_dtype),
        compiler_params=_cparams(2),
        name="matmul",
    )(x, w)


def _mm_x32_kernel(x_ref, w_ref, o_ref, xb_ref):
    @pl.when(pl.program_id(1) == 0)
    def _():
        xb_ref[...] = x_ref[...].astype(BF16)

    o_ref[...] = _dot(xb_ref[...], w_ref[...]).astype(o_ref.dtype)


def matmul_x32(x, w, *, bm, bn, out_dtype=F32):
    m, k = x.shape
    n = w.shape[1]
    return pl.pallas_call(
        _mm_x32_kernel,
        grid=(m // bm, n // bn),
        in_specs=[pl.BlockSpec((bm, k), lambda i, j: (i, 0)), pl.BlockSpec((k, bn), lambda i, j: (0, j))],
        out_specs=pl.BlockSpec((bm, bn), lambda i, j: (i, j)),
        out_shape=SDS((m, n), out_dtype),
        scratch_shapes=[pltpu.VMEM((bm, k), BF16)],
        compiler_params=_cparams(2),
        name="matmul_x32",
    )(x, w)


def _mm_w32_kernel(x_ref, w_ref, o_ref):
    o_ref[...] = _dot(x_ref[...], w_ref[0].astype(BF16)).astype(o_ref.dtype)


def matmul_w32(x, w, layer, *, bn, out_dtype=F32):
    m, k = x.shape
    n = w.shape[2]
    return pl.pallas_call(
        _mm_w32_kernel,
        grid=(n // bn,),
        in_specs=[pl.BlockSpec((m, k), lambda j: (0, 0)), pl.BlockSpec((1, k, bn), lambda j: (layer, 0, j))],
        out_specs=pl.BlockSpec((m, bn), lambda j: (0, j)),
        out_shape=SDS((m, n), out_dtype),
        compiler_params=_cparams(1),
        name="matmul_w32",
    )(x, w)


def _mm_cast_kernel(x_ref, w_ref, o_ref, wb_ref):
    wb_ref[...] = w_ref[0].astype(BF16)
    o_ref[...] = _dot(x_ref[...], wb_ref[...]).astype(o_ref.dtype)


def matmul_cast(x, w, layer, *, bn, out_dtype=F32):
    m, k = x.shape
    n = w.shape[2]
    return pl.pallas_call(
        _mm_cast_kernel,
        grid=(n // bn,),
        in_specs=[pl.BlockSpec((m, k), lambda j: (0, 0)), pl.BlockSpec((1, k, bn), lambda j: (layer, 0, j))],
        out_specs=[pl.BlockSpec((m, bn), lambda j: (0, j)), pl.BlockSpec((k, bn), lambda j: (0, j))],
        out_shape=[SDS((m, n), out_dtype), SDS((k, n), BF16)],
        compiler_params=_cparams(1),
        name="matmul_cast",
    )(x, w)


def _glu_kernel(z_ref, zt_ref, w_ref, o_ref):
    a = _dot(z_ref[...], w_ref[...])
    o_ref[...] = (zt_ref[...].astype(F32) * jax.nn.sigmoid(a)).astype(o_ref.dtype)


def glu(z, w, *, bm, bn):
    m, k = z.shape
    n = w.shape[1]
    return pl.pallas_call(
        _glu_kernel,
        grid=(m // bm, n // bn),
        in_specs=[pl.BlockSpec((bm, k), lambda i, j: (i, 0)), pl.BlockSpec((bm, bn), lambda i, j: (i, j)),
                  pl.BlockSpec((k, bn), lambda i, j: (0, j))],
        out_specs=pl.BlockSpec((bm, bn), lambda i, j: (i, j)),
        out_shape=SDS((m, n), BF16),
        compiler_params=_cparams(2),
        name="glu",
    )(z, z, w)


def _proj_ln_kernel(*refs, n_in):
    xs = refs[:n_in]
    w_ref, res_ref, g_ref, b_ref, y_ref, yb_ref = refs[n_in:]
    _proj_res_ln([x_ref[...] for x_ref in xs], w_ref, res_ref, g_ref, b_ref, y_ref, yb_ref)


def proj_res_ln(xs, w, res, g, b, *, bm):
    m, d = res.shape
    k = w.shape[0]
    row = lambda i: (i, 0)
    fixed = lambda i: (0, 0)
    return pl.pallas_call(
        functools.partial(_proj_ln_kernel, n_in=len(xs)),
        grid=(m // bm,),
        in_specs=[pl.BlockSpec((bm, x.shape[1]), row) for x in xs] + [
            pl.BlockSpec((k, d), fixed, pipeline_mode=pl.Buffered(1)),
            pl.BlockSpec((bm, d), row),
            pl.BlockSpec((1, d), fixed),
            pl.BlockSpec((1, d), fixed),
        ],
        out_specs=[pl.BlockSpec((bm, d), row), pl.BlockSpec((bm, d), row)],
        out_shape=[SDS((m, d), F32), SDS((m, d), BF16)],
        compiler_params=_cparams(1),
        name="proj_res_ln",
    )(*xs, w, res, g.reshape(1, d), b.reshape(1, d))


def _hg_out_kernel(o_ref, gate_ref, ng_ref, w_ref, res_ref, g_ref, b_ref, y_ref, yb_ref):
    o = o_ref[...]
    xn = o * lax.rsqrt(jnp.mean(o * o, axis=-1, keepdims=True) + RMS_EPS) * ng_ref[...]
    xn = xn * jax.nn.sigmoid(gate_ref[...])
    _proj_res_ln([xn.astype(BF16)], w_ref, res_ref, g_ref, b_ref, y_ref, yb_ref)


def hg_out_ln(o, h_all, norm_g, w, res, g, b, *, bm):
    m, d = res.shape
    row = lambda i: (i, 0)
    fixed = lambda i: (0, 0)
    return pl.pallas_call(
        _hg_out_kernel,
        grid=(m // bm,),
        in_specs=[
            pl.BlockSpec((bm, d), row),
            pl.BlockSpec((bm, d), lambda i: (i, 3)),
            pl.BlockSpec((1, d), fixed),
            pl.BlockSpec((d, d), fixed, pipeline_mode=pl.Buffered(1)),
            pl.BlockSpec((bm, d), row),
            pl.BlockSpec((1, d), fixed),
            pl.BlockSpec((1, d), fixed),
        ],
        out_specs=[pl.BlockSpec((bm, d), row), pl.BlockSpec((bm, d), row)],
        out_shape=[SDS((m, d), F32), SDS((m, d), BF16)],
        compiler_params=_cparams(1),
        name="hg_out_ln",
    )(o, h_all, norm_g.reshape(1, d), w, res, g.reshape(1, d), b.reshape(1, d))


def _ffn_step(f, x_ref, wgu_ref, wd_ref, g_ref, b_ref, y_ref, yb_ref):
    @pl.when(f == 0)
    def _():
        x = x_ref[...]
        yb_ref[...] = x.astype(BF16)
        y_ref[...] = ALPHA * x

    bf = wd_ref.shape[0]
    gu = _dot(yb_ref[...], wgu_ref[...])
    gate, up = gu[:, :bf], gu[:, bf:]
    h = (gate * jax.nn.sigmoid(gate) * up).astype(BF16)
    y_ref[...] += _dot(h, wd_ref[...])

    @pl.when(f == pl.num_programs(1) - 1)
    def _():
        g, b = g_ref[...], b_ref[...]
        chunk = math.gcd(y_ref.shape[0], LN_ROW_CHUNK)

        def ln_chunk(i, _):
            rows = pl.ds(pl.multiple_of(i * chunk, chunk), chunk)
            y = _layer_norm_rows(y_ref[rows, :], g, b)
            y_ref[rows, :] = y
            yb_ref[rows, :] = y.astype(BF16)
            return 0

        lax.fori_loop(0, y_ref.shape[0] // chunk, ln_chunk, 0)


def _ffn_kernel(x_ref, wgu_ref, wd_ref, g_ref, b_ref, y_ref, yb_ref):
    _ffn_step(pl.program_id(1), x_ref, wgu_ref.at[0], wd_ref, g_ref, b_ref, y_ref, yb_ref)


def ffn_ln(x, wgu, wd, g, b, *, bm):
    m, d = x.shape
    nf, _, bf2 = wgu.shape
    return pl.pallas_call(
        _ffn_kernel,
        grid=(m // bm, nf),
        in_specs=[
            pl.BlockSpec((bm, d), lambda i, f: (i, 0)),
            pl.BlockSpec((1, d, bf2), lambda i, f: (f, 0, 0)),
            pl.BlockSpec((bf2 // 2, d), lambda i, f: (f, 0)),
            pl.BlockSpec((1, d), lambda i, f: (0, 0)),
            pl.BlockSpec((1, d), lambda i, f: (0, 0)),
        ],
        out_specs=[pl.BlockSpec((bm, d), lambda i, f: (i, 0)), pl.BlockSpec((bm, d), lambda i, f: (i, 0))],
        out_shape=[SDS((m, d), F32), SDS((m, d), BF16)],
        compiler_params=_cparams(2, vmem=VMEM_LIMIT_MAX),
        name="ffn_ln",
    )(x, wgu, wd, g.reshape(1, d), b.reshape(1, d))


def _ffn_cast_kernel(x_ref, wg_ref, wu_ref, wd_ref, g_ref, b_ref, y_ref, yb_ref, wgub_ref, wdb_ref):
    bf = wd_ref.shape[1]
    wgub_ref[0, :, :bf] = wg_ref[0].astype(BF16)
    wgub_ref[0, :, bf:] = wu_ref[0].astype(BF16)
    wdb_ref[...] = wd_ref[0].astype(BF16)
    _ffn_step(pl.program_id(1), x_ref, wgub_ref.at[0], wdb_ref, g_ref, b_ref, y_ref, yb_ref)


def ffn_ln_cast(x, wg, wu, wd, layer, g, b, *, bf=256):
    m, d = x.shape
    dff = wg.shape[2]
    fixed = lambda i, f: (0, 0)
    return pl.pallas_call(
        _ffn_cast_kernel,
        grid=(1, dff // bf),
        in_specs=[
            pl.BlockSpec((m, d), fixed),
            pl.BlockSpec((1, d, bf), lambda i, f: (layer, 0, f)),
            pl.BlockSpec((1, d, bf), lambda i, f: (layer, 0, f)),
            pl.BlockSpec((1, bf, d), lambda i, f: (layer, f, 0)),
            pl.BlockSpec((1, d), fixed),
            pl.BlockSpec((1, d), fixed),
        ],
        out_specs=[
            pl.BlockSpec((m, d), fixed),
            pl.BlockSpec((m, d), fixed),
            pl.BlockSpec((1, d, 2 * bf), lambda i, f: (f, 0, 0)),
            pl.BlockSpec((bf, d), lambda i, f: (f, 0)),
        ],
        out_shape=[SDS((m, d), F32), SDS((m, d), BF16), SDS((dff // bf, d, 2 * bf), BF16), SDS((dff, d), BF16)],
        compiler_params=_cparams(2),
        name="ffn_ln_cast",
    )(x, wg, wu, wd, g.reshape(1, d), b.reshape(1, d))


def _cmul(ar, ai, br, bi):
    return ar * br - ai * bi, ar * bi + ai * br


def _s5_coefs(a_r, a_i, period):
    l = a_r.shape[1]
    pows = [(a_r, a_i)]
    for _ in range(period - 1):
        pows.append(_cmul(pows[-1][0], pows[-1][1], a_r, a_i))
    t = lax.broadcasted_iota(jnp.int32, (8, l), 0) & (period - 1)
    shifts = []
    k = 1
    while k < period:
        keep = t >= k
        shifts.append((jnp.where(keep, pows[k - 1][0], 0.0), jnp.where(keep, pows[k - 1][1], 0.0)))
        k *= 2
    p_r = jnp.broadcast_to(pows[0][0], (8, l))
    p_i = jnp.broadcast_to(pows[0][1], (8, l))
    for j in range(1, period):
        p_r = jnp.where(t == j, pows[j][0], p_r)
        p_i = jnp.where(t == j, pows[j][1], p_i)
    return shifts, (p_r, p_i)


def _s5_scan_tile(x_r, x_i, shifts):
    k = 1
    for c_r, c_i in shifts:
        s_r = pltpu.roll(x_r, k, 0)
        s_i = pltpu.roll(x_i, k, 0)
        x_r, x_i = x_r + c_r * s_r - c_i * s_i, x_i + c_r * s_i + c_i * s_r
        k *= 2
    return x_r, x_i


S5_CHAINS = 4
S5_LANE_BLOCKS = 2 * S5_HW // 128


def _s5_prompt_kernel(u_ref, bw_ref, cw_ref, ar_ref, ai_ref, d_ref, z_ref, hfin_ref, h_scr, carry_ref):
    tt = pl.program_id(2)
    n_tiles = h_scr.shape[1]
    tb = n_tiles * 8
    half = S5_LANE_BLOCKS // 2
    chains = range(S5_CHAINS)

    @pl.when(tt == 0)
    def _():
        carry_ref[...] = jnp.zeros_like(carry_ref)

    for ch in chains:
        bu = _dot(u_ref[0, :, ch * S5_UW:(ch + 1) * S5_UW].astype(BF16), bw_ref[ch])
        for s in range(S5_LANE_BLOCKS):
            h_scr[ch, :, s * 8:(s + 1) * 8, :] = bu[:, s * 128:(s + 1) * 128].reshape(n_tiles, 8, 128)
    a = [(ar_ref[ch], ai_ref[ch]) for ch in chains]

    def tile(i, carry):
        hs = list(carry)
        for r in range(8):
            for ch in chains:
                (a_r, a_i), (h_r, h_i) = a[ch], hs[ch]
                re_rows = pl.ds(r, half, stride=8)
                im_rows = pl.ds(8 * half + r, half, stride=8)
                n_r = a_r * h_r - a_i * h_i + h_scr[ch, i, re_rows, :]
                n_i = a_r * h_i + a_i * h_r + h_scr[ch, i, im_rows, :]
                h_scr[ch, i, re_rows, :] = n_r
                h_scr[ch, i, im_rows, :] = n_i
                hs[ch] = (n_r, n_i)
        return tuple(hs)

    final = lax.fori_loop(0, n_tiles, tile, tuple((carry_ref[ch, 0], carry_ref[ch, 1]) for ch in chains))
    for ch in chains:
        carry_ref[ch, 0], carry_ref[ch, 1] = final[ch]
    for ch in chains:
        cols = slice(ch * S5_UW, (ch + 1) * S5_UW)
        h = jnp.concatenate([h_scr[ch, :, s * 8:(s + 1) * 8, :].reshape(tb, 128) for s in range(S5_LANE_BLOCKS)],
                            axis=1)
        y = _dot(h.astype(BF16), cw_ref[ch]) + d_ref[ch] * u_ref[0, :, cols]
        z_ref[0, :, cols] = jax.nn.gelu(y).astype(z_ref.dtype)

    @pl.when(tt == pl.num_programs(2) - 1)
    def _():
        hfin_ref[0] = carry_ref[...]


def s5_prompt(h_all, bw, cw, a_r, a_i, d_skip, *, tb):
    bsz, t, _ = h_all.shape
    nc = S5_CHAINS
    blk = lambda b, g, s: (g, 0, 0)
    rows = pl.BlockSpec((1, tb, nc * S5_UW), lambda b, g, s: (b, s, g))
    return pl.pallas_call(
        _s5_prompt_kernel,
        grid=(bsz, S5_NGB // nc, t // tb),
        in_specs=[
            rows,
            pl.BlockSpec((nc, S5_UW, 2 * S5_HW), blk),
            pl.BlockSpec((nc, 2 * S5_HW, S5_UW), blk),
            pl.BlockSpec((nc, 8, 128), blk),
            pl.BlockSpec((nc, 8, 128), blk),
            pl.BlockSpec((nc, 1, S5_UW), blk),
        ],
        out_specs=[rows, pl.BlockSpec((1, nc, 2, 8, 128), lambda b, g, s: (b, g, 0, 0, 0))],
        out_shape=[SDS((bsz, t, D_S5), BF16), SDS((bsz, S5_NGB, 2, 8, 128), F32)],
        scratch_shapes=[pltpu.VMEM((nc, tb // 8, S5_LANE_BLOCKS * 8, 128), F32), pltpu.VMEM((nc, 2, 8, 128), F32)],
        compiler_params=_cparams(3),
        name="s5_prompt",
    )(h_all, bw, cw, a_r.reshape(S5_NGB, 8, 128), a_i.reshape(S5_NGB, 8, 128), d_skip)


def _s5_sample_kernel(u_ref, h0_ref, bw_ref, cw_ref, ar_ref, ai_ref, d_ref, z_ref, h_ref, *, period):
    u = u_ref[...]
    bu = _dot(u.astype(BF16), bw_ref[0])
    shifts, (p_r, p_i) = _s5_coefs(ar_ref[0], ai_ref[0], period)
    for i in range(u.shape[0] // 8):
        rows = slice(i * 8, (i + 1) * 8)
        x_r, x_i = _s5_scan_tile(bu[rows, 0:S5_HW], bu[rows, S5_HW:2 * S5_HW], shifts)
        h0_r = h0_ref[rows, 0:S5_HW]
        h0_i = h0_ref[rows, S5_HW:2 * S5_HW]
        h_ref[rows, 0:S5_HW] = x_r + p_r * h0_r - p_i * h0_i
        h_ref[rows, S5_HW:2 * S5_HW] = x_i + p_r * h0_i + p_i * h0_r
    y = _dot(h_ref[...].astype(BF16), cw_ref[0]) + d_ref[0] * u
    z_ref[...] = jax.nn.gelu(y).astype(z_ref.dtype)


def s5_sample(h_all, h0x, bw, cw, a_r, a_i, d_skip, *, period):
    m = h_all.shape[0]
    blk = lambda g: (g, 0, 0)
    return pl.pallas_call(
        functools.partial(_s5_sample_kernel, period=period),
        grid=(S5_NGB,),
        in_specs=[
            pl.BlockSpec((m, S5_UW), lambda g: (0, g)),
            pl.BlockSpec((m, 2 * S5_HW), lambda g: (0, g)),
            pl.BlockSpec((1, S5_UW, 2 * S5_HW), blk),
            pl.BlockSpec((1, 2 * S5_HW, S5_UW), blk),
            pl.BlockSpec((1, 1, S5_HW), blk),
            pl.BlockSpec((1, 1, S5_HW), blk),
            pl.BlockSpec((1, 1, S5_UW), blk),
        ],
        out_specs=[pl.BlockSpec((m, S5_UW), lambda g: (0, g)), pl.BlockSpec((m, 2 * S5_HW), lambda g: (0, g))],
        out_shape=[SDS((m, D_S5), BF16), SDS((m, S5_NGB * 2 * S5_HW), F32)],
        compiler_params=_cparams(1),
        name="s5_sample",
    )(h_all, h0x, bw, cw, a_r, a_i, d_skip)


def _bias_kernel(rb_ref, o_ref):
    h = pl.program_id(0)
    q = lax.broadcasted_iota(jnp.int32, (WINDOW, 2 * WINDOW), 0)
    k = lax.broadcasted_iota(jnp.int32, (WINDOW, 2 * WINDOW), 1)
    n = jnp.maximum(q + WINDOW - k, 0)
    max_exact = N_BUCKETS // 2
    nf = jnp.maximum(n, 1).astype(F32)
    large = max_exact + (jnp.log(nf / max_exact) / math.log(MAX_DISTANCE / max_exact)
                         * (N_BUCKETS - max_exact)).astype(jnp.int32)
    large = jnp.minimum(large, N_BUCKETS - 1)
    bucket = jnp.where(n < max_exact, n, large)
    out = jnp.zeros((WINDOW, 2 * WINDOW), F32)
    for b in range(N_BUCKETS):
        out = jnp.where(bucket == b, rb_ref[b, h], out)
    o_ref[0] = out


def bias_table(rel_bias):
    return pl.pallas_call(
        _bias_kernel,
        grid=(SWA_HEADS,),
        in_specs=[pl.BlockSpec(memory_space=pltpu.SMEM)],
        out_specs=pl.BlockSpec((1, WINDOW, 2 * WINDOW), lambda h: (h, 0, 0)),
        out_shape=SDS((SWA_HEADS, WINDOW, 2 * WINDOW), F32),
        compiler_params=_cparams(1),
        name="bias_table",
    )(rel_bias)


def _swa_softmax(s, valid, sink_col):
    s = jnp.where(valid, s, NEG_BIG)
    m = jnp.maximum(jnp.max(s, axis=-1, keepdims=True), sink_col)
    p = jnp.exp(s - m)
    return p / (jnp.sum(p, axis=-1, keepdims=True) + jnp.exp(sink_col - m))


def _sink_col(sinks_ref, first_head, rows_per_head):
    n = SWA_GQ * rows_per_head
    r = lax.broadcasted_iota(jnp.int32, (n, 1), 0)
    col = jnp.full((n, 1), sinks_ref[first_head], F32)
    for g in range(1, SWA_GQ):
        col = jnp.where(r >= g * rows_per_head, sinks_ref[first_head + g], col)
    return col


SWA_KV_PER_STEP = 8


def _swa_prompt_kernel(sinks_ref, q_ref, kp_ref, kc_ref, vp_ref, vc_ref, bias_ref, o_ref):
    nkv = SWA_KV_PER_STEP
    part = pl.program_id(0)
    i = pl.program_id(2)
    blk = WINDOW
    q = q_ref[0].astype(BF16)
    kk = jnp.concatenate([kp_ref[0], kc_ref[0]], axis=0).astype(BF16)
    vv = jnp.concatenate([vp_ref[0], vc_ref[0]], axis=0).astype(BF16)
    qi = lax.broadcasted_iota(jnp.int32, (SWA_GQ * blk, 2 * blk), 0) & (blk - 1)
    ki = lax.broadcasted_iota(jnp.int32, (SWA_GQ * blk, 2 * blk), 1)
    valid = (ki > qi) & (ki <= qi + blk) & ((ki >= blk) | (i > 0))
    heads = [slice(j * HEAD_DIM, (j + 1) * HEAD_DIM) for j in range(nkv)]
    q4 = [jnp.concatenate([q[:, (j * SWA_GQ + g) * HEAD_DIM:(j * SWA_GQ + g + 1) * HEAD_DIM]
                           for g in range(SWA_GQ)], axis=0) for j in range(nkv)]
    s = [_dot_nt(q4[j], kk[:, c]) * (HEAD_DIM ** -0.5) for j, c in enumerate(heads)]
    p = [_swa_softmax(s[j] + bias_ref[j * SWA_GQ:(j + 1) * SWA_GQ].reshape(SWA_GQ * blk, 2 * blk), valid,
                      _sink_col(sinks_ref, (part * nkv + j) * SWA_GQ, blk)).astype(BF16) for j in range(nkv)]
    o4 = [_dot(p[j], vv[:, c]) for j, c in enumerate(heads)]
    o_ref[0] = jnp.concatenate([o4[j][g * blk:(g + 1) * blk] for j in range(nkv) for g in range(SWA_GQ)],
                               axis=1).astype(o_ref.dtype)


def swa_prompt(h_all, sinks, table):
    bsz, t, _ = h_all.shape
    nblk = t // WINDOW
    nkv = SWA_KV_PER_STEP
    qw = nkv * SWA_GQ * HEAD_DIM
    kw = nkv * HEAD_DIM
    q0, k0, v0 = D_S5 // qw, (D_S5 + D_SWA_Q) // kw, (D_S5 + D_SWA_Q + D_SWA_KV) // kw
    cur = lambda c0: (lambda p, b, i: (b, i, c0 + p))
    prev = lambda c0: (lambda p, b, i: (b, jnp.maximum(i - 1, 0), c0 + p))
    return pl.pallas_call(
        _swa_prompt_kernel,
        grid=(SWA_KV_HEADS // nkv, bsz, nblk),
        in_specs=[
            pl.BlockSpec(memory_space=pltpu.SMEM),
            pl.BlockSpec((1, WINDOW, qw), cur(q0)),
            pl.BlockSpec((1, WINDOW, kw), prev(k0)),
            pl.BlockSpec((1, WINDOW, kw), cur(k0)),
            pl.BlockSpec((1, WINDOW, kw), prev(v0)),
            pl.BlockSpec((1, WINDOW, kw), cur(v0)),
            pl.BlockSpec((nkv * SWA_GQ, WINDOW, 2 * WINDOW), lambda p, b, i: (p, 0, 0)),
        ],
        out_specs=pl.BlockSpec((1, WINDOW, qw), lambda p, b, i: (b, i, p)),
        out_shape=SDS((bsz, t, D_SWA_Q), BF16),
        compiler_params=_cparams(3),
        name="swa_prompt",
    )(sinks, h_all, h_all, h_all, h_all, h_all, table)


def _swa_sample_kernel(sinks_ref, q_ref, kk_ref, vv_ref, bias_ref, o_ref, *, t_new):
    rows = q_ref.shape[0]
    n_keys = kk_ref.shape[1]
    q = q_ref[...].astype(BF16)
    r = lax.broadcasted_iota(jnp.int32, (SWA_GQ * rows, n_keys), 0)
    ti = r & (t_new - 1)
    ki = lax.broadcasted_iota(jnp.int32, (SWA_GQ * rows, n_keys), 1)
    valid = (ki > ti) & (ki <= ti + WINDOW)
    second = (r & (rows - 1)) >= t_new
    second_o = (lax.broadcasted_iota(jnp.int32, (SWA_GQ * rows, HEAD_DIM), 0) & (rows - 1)) >= t_new
    nkv = SWA_KV_HEADS
    heads = [slice(j * HEAD_DIM, (j + 1) * HEAD_DIM) for j in range(nkv)]
    kk = [kk_ref[bb].astype(BF16) for bb in range(2)]
    vv = [vv_ref[bb].astype(BF16) for bb in range(2)]
    q4 = [jnp.concatenate([q[:, (j * SWA_GQ + g) * HEAD_DIM:(j * SWA_GQ + g + 1) * HEAD_DIM]
                           for g in range(SWA_GQ)], axis=0) for j in range(nkv)]
    s_seq = [[_dot_nt(q4[j], kk[bb][:, c]) for j, c in enumerate(heads)] for bb in range(2)]
    p = []
    for j in range(nkv):
        s = jnp.where(second, s_seq[1][j], s_seq[0][j]) * (HEAD_DIM ** -0.5)
        s = s + bias_ref[j * SWA_GQ:(j + 1) * SWA_GQ].reshape(SWA_GQ * rows, n_keys)
        p.append(_swa_softmax(s, valid, _sink_col(sinks_ref, j * SWA_GQ, rows)).astype(BF16))
    o_seq = [[_dot(p[j], vv[bb][:, c]) for j, c in enumerate(heads)] for bb in range(2)]
    o4 = [jnp.where(second_o, o_seq[1][j], o_seq[0][j]) for j in range(nkv)]
    o_ref[...] = jnp.concatenate([o4[j][g * rows:(g + 1) * rows] for j in range(nkv) for g in range(SWA_GQ)],
                                 axis=1).astype(o_ref.dtype)


def swa_sample(h_all, kk, vv, sinks, bias_s, *, t_new):
    m = h_all.shape[0]
    n_keys = kk.shape[1]
    rows = 2 * t_new
    return pl.pallas_call(
        functools.partial(_swa_sample_kernel, t_new=t_new),
        grid=(m // rows,),
        in_specs=[
            pl.BlockSpec(memory_space=pltpu.SMEM),
            pl.BlockSpec((rows, D_SWA_Q), lambda i: (i, D_S5 // D_SWA_Q)),
            pl.BlockSpec((2, n_keys, D_SWA_KV), lambda i: (i, 0, 0)),
            pl.BlockSpec((2, n_keys, D_SWA_KV), lambda i: (i, 0, 0)),
            pl.BlockSpec((SWA_HEADS, rows, n_keys), lambda i: (0, 0, 0)),
        ],
        out_specs=pl.BlockSpec((rows, D_SWA_Q), lambda i: (i, 0)),
        out_shape=SDS((m, D_SWA_Q), BF16),
        compiler_params=_cparams(1),
        name="swa_sample",
    )(sinks, h_all, kk, vv, bias_s)


def _hg_gates(q, fz, lb):
    qs = q * jax.nn.sigmoid(q)
    f = lb + (1.0 - lb) * jax.nn.sigmoid(fz)
    return qs, jnp.log(f), 1.0 - f


HG_GROUP = 4


def _hgrn_prompt_kernel(q_ref, f_ref, v_ref, lb_ref, o_ref, st_ref, s_scr, *, n_heads):
    tt = pl.program_id(2)
    n_rows = q_ref.shape[1]
    c, ng = HG_CHUNK, HG_GROUP
    gr = c * ng

    @pl.when(tt == 0)
    def _():
        s_scr[...] = jnp.zeros_like(s_scr)

    ri = lax.broadcasted_iota(jnp.int32, (gr, gr), 0)
    ci = lax.broadcasted_iota(jnp.int32, (gr, gr), 1)
    causal = (ri >= ci) & ((ri // c) == (ci // c))
    t_in = lax.broadcasted_iota(jnp.int32, (gr, HG_DK), 0) & (c - 1)
    own = ((lax.broadcasted_iota(jnp.int32, (gr, ng * HG_DK), 0) // c)
           == (lax.broadcasted_iota(jnp.int32, (gr, ng * HG_DK), 1) // HG_DK)).astype(BF16)

    def body(gi, _):
        r0 = pl.multiple_of(gi * gr, gr)
        rows = pl.ds(r0, gr)
        hs = range(n_heads)
        heads = [slice(hd * HG_DK, (hd + 1) * HG_DK) for hd in hs]
        v, qt, kt, kl, decay = [], [], [], [], []
        for cols in heads:
            qs, lf, k = _hg_gates(q_ref[0, rows, cols], f_ref[0, rows, cols], lb_ref[:, cols])
            v.append(v_ref[0, rows, cols].astype(BF16))
            cum = lf
            sh = 1
            while sh < c:
                cum = cum + jnp.where(t_in >= sh, pltpu.roll(cum, sh, 0), 0.0)
                sh *= 2
            last3 = cum.reshape(ng, c, HG_DK)[:, c - 1:c, :]
            last = jnp.broadcast_to(last3, (ng, c, HG_DK)).reshape(gr, HG_DK)
            qt.append((qs * jnp.exp(cum)).astype(BF16))
            kt.append((k * jnp.exp(-cum)).astype(BF16))
            kl.append((k * jnp.exp(last - cum)).astype(BF16))
            decay.append(jnp.exp(last3))
        kv = [_dot_tn(v[h], jnp.concatenate([kl[h]] * ng, axis=1) * own) for h in hs]
        attn = [jnp.where(causal, _dot_nt(qt[h], kt[h]), 0.0).astype(BF16) for h in hs]
        s_cat = []
        for h in hs:
            st = s_scr[h]
            starts = []
            for j in range(ng):
                starts.append(st)
                st = st * decay[h][j] + kv[h][:, j * HG_DK:(j + 1) * HG_DK]
            s_scr[h] = st
            s_cat.append(jnp.concatenate(starts, axis=1).astype(BF16))
        intra = [_dot(attn[h], v[h]) for h in hs]
        inter = [_dot_nt(jnp.concatenate([qt[h]] * ng, axis=1) * own, s_cat[h]) for h in hs]
        for h, cols in enumerate(heads):
            o_ref[0, rows, cols] = intra[h] + inter[h]
        return 0

    lax.fori_loop(0, n_rows // gr, body, 0, unroll=True)

    @pl.when(tt == pl.num_programs(2) - 1)
    def _():
        st_ref[0] = s_scr[...]


def hgrn_prompt(h_all, lb, *, tb, n_heads=2):
    bsz, t, _ = h_all.shape
    hw = n_heads * HG_DK
    nhb = D_MODEL // hw
    col = lambda c0: (lambda b, h, s: (b, s, c0 * nhb + h))
    return pl.pallas_call(
        functools.partial(_hgrn_prompt_kernel, n_heads=n_heads),
        grid=(bsz, nhb, t // tb),
        in_specs=[
            pl.BlockSpec((1, tb, hw), col(0)),
            pl.BlockSpec((1, tb, hw), col(1)),
            pl.BlockSpec((1, tb, hw), col(2)),
            pl.BlockSpec((1, hw), lambda b, h, s: (0, h)),
        ],
        out_specs=[
            pl.BlockSpec((1, tb, hw), lambda b, h, s: (b, s, h)),
            pl.BlockSpec((1, n_heads, HG_DK, HG_DK), lambda b, h, s: (b, h, 0, 0)),
        ],
        out_shape=[SDS((bsz, t, D_MODEL), F32), SDS((bsz, HG_HEADS, HG_DK, HG_DK), F32)],
        scratch_shapes=[pltpu.VMEM((n_heads, HG_DK, HG_DK), F32)],
        compiler_params=_cparams(3),
        name="hgrn_prompt",
    )(h_all, h_all, h_all, lb.reshape(1, D_MODEL))


def _hgrn_sample_kernel(q_ref, f_ref, v_ref, lb_ref, s0_ref, o_ref, s_ref, *, n_heads, t_new):
    rows = q_ref.shape[0]
    ri = lax.broadcasted_iota(jnp.int32, (rows, rows), 0)
    ci = lax.broadcasted_iota(jnp.int32, (rows, rows), 1)
    causal = (ri >= ci) & ((ri >= t_new) == (ci >= t_new))
    tril = causal.astype(F32)
    hw = n_heads * HG_DK
    second_w = lax.broadcasted_iota(jnp.int32, (rows, hw), 0) >= t_new
    second = second_w[:, :HG_DK]
    eye = (lax.broadcasted_iota(jnp.int32, (HG_DK, HG_DK), 0)
           == lax.broadcasted_iota(jnp.int32, (HG_DK, HG_DK), 1))
    heads = [slice(hd * HG_DK, (hd + 1) * HG_DK) for hd in range(n_heads)]
    qs, lf, k = _hg_gates(q_ref[...], f_ref[...], lb_ref[...])
    v = v_ref[...].astype(BF16)
    cum = jnp.dot(tril, lf, preferred_element_type=F32, precision=lax.Precision.HIGHEST)
    last0 = cum[t_new - 1:t_new]
    last1 = cum[rows - 1:rows]
    qt = (qs * jnp.exp(cum)).astype(BF16)
    kt = (k * jnp.exp(-cum)).astype(BF16)
    kl = k * jnp.exp(jnp.where(second_w, last1, last0) - cum)
    kl_seq = (jnp.where(second_w, 0.0, kl).astype(BF16), jnp.where(second_w, kl, 0.0).astype(BF16))
    decay_seq = (jnp.exp(last0), jnp.exp(last1))
    attn = [jnp.where(causal, _dot_nt(qt[:, c], kt[:, c]), 0.0).astype(BF16) for c in heads]
    inter = [[_dot(qt[:, c], s0_ref[bb, hd].astype(BF16)) for hd, c in enumerate(heads)] for bb in range(2)]
    intra = [_dot(a, v[:, c]) for a, c in zip(attn, heads)]
    for hd, c in enumerate(heads):
        o_ref[:, c] = intra[hd] + jnp.where(second, inter[1][hd], inter[0][hd])
    for bb in range(2):
        kv = [_dot_tn(kl_seq[bb][:, c], v[:, c]) for c in heads]
        for hd, c in enumerate(heads):
            decay = jnp.sum(jnp.where(eye, decay_seq[bb][:, c], 0.0), axis=1, keepdims=True)
            s_ref[bb, hd] = decay * s0_ref[bb, hd] + kv[hd]


def hgrn_sample(h_all, lb, s0, *, t_new, n_heads=8):
    m = h_all.shape[0]
    rows = 2 * t_new
    hw = n_heads * HG_DK
    nhb = D_MODEL // hw
    col = lambda c0: (lambda i, h: (i, c0 * nhb + h))
    return pl.pallas_call(
        functools.partial(_hgrn_sample_kernel, n_heads=n_heads, t_new=t_new),
        grid=(m // rows, nhb),
        in_specs=[
            pl.BlockSpec((rows, hw), col(0)),
            pl.BlockSpec((rows, hw), col(1)),
            pl.BlockSpec((rows, hw), col(2)),
            pl.BlockSpec((1, hw), lambda i, h: (0, h)),
            pl.BlockSpec((2, n_heads, HG_DK, HG_DK), lambda i, h: (i, h, 0, 0)),
        ],
        out_specs=[
            pl.BlockSpec((rows, hw), lambda i, h: (i, h)),
            pl.BlockSpec((2, n_heads, HG_DK, HG_DK), lambda i, h: (i, h, 0, 0)),
        ],
        out_shape=[SDS((m, D_MODEL), F32), SDS(s0.shape, F32)],
        compiler_params=_cparams(2),
        name="hgrn_sample",
    )(h_all, h_all, h_all, lb.reshape(1, D_MODEL), s0)


def _mem_attention(q, mk_ref, mv_ref, n_seq):
    rows = q.shape[0]
    second = lax.broadcasted_iota(jnp.int32, (rows, 1), 0) >= rows // 2

    def mem_head(ref, seq, h):
        if len(ref.shape) == 3:
            return ref[seq, :, h * MEM_HEAD_DIM:(h + 1) * MEM_HEAD_DIM].astype(BF16)
        return ref[0, seq, :, h, :].astype(BF16)

    hs = range(MEM_HEADS)
    qh = [q[:, h * MEM_HEAD_DIM:(h + 1) * MEM_HEAD_DIM] for h in hs]
    s_seq = [[_dot_nt(qh[h], mem_head(mk_ref, seq, h)) for h in hs] for seq in range(n_seq)]
    p = []
    for h in hs:
        s = s_seq[0][h] if n_seq == 1 else jnp.where(second, s_seq[1][h], s_seq[0][h])
        s = s * (MEM_HEAD_DIM ** -0.5)
        e = jnp.exp(s - jnp.max(s, axis=-1, keepdims=True))
        p.append((e / jnp.sum(e, axis=-1, keepdims=True)).astype(BF16))
    o_seq = [[_dot(p[h], mem_head(mv_ref, seq, h)) for h in hs] for seq in range(n_seq)]
    heads = [(o_seq[0][h] if n_seq == 1 else jnp.where(second, o_seq[1][h], o_seq[0][h])).astype(BF16) for h in hs]
    return jnp.concatenate(heads, axis=1)


def _xattn_kernel(y_ref, yb_ref, wq_ref, mk_ref, mv_ref, wo_ref, g_ref, b_ref, o_ref, ob_ref, *, n_seq):
    q = _dot(yb_ref[...], wq_ref[...]).astype(BF16)
    att = _mem_attention(q, mk_ref, mv_ref, n_seq)
    _proj_res_ln([att], wo_ref, y_ref, g_ref, b_ref, o_ref, ob_ref)


def _mem_attn_kernel(q_ref, mk_ref, mv_ref, o_ref):
    o_ref[...] = _mem_attention(q_ref[...], mk_ref, mv_ref, 2)


def mem_attn_cache(q, cache_k, cache_v, layer, *, rows_per_seq):
    m, dm = q.shape
    rows = 2 * rows_per_seq
    mem_spec = pl.BlockSpec((1, 2, N_MEM, MEM_HEADS, MEM_HEAD_DIM), lambda i: (layer, i, 0, 0, 0))
    return pl.pallas_call(
        _mem_attn_kernel,
        grid=(m // rows,),
        in_specs=[pl.BlockSpec((rows, dm), lambda i: (i, 0)), mem_spec, mem_spec],
        out_specs=pl.BlockSpec((rows, dm), lambda i: (i, 0)),
        out_shape=SDS((m, dm), BF16),
        compiler_params=_cparams(1),
        name="mem_attn_cache",
    )(q, cache_k, cache_v)


def xattn_ln(y, yb, wq, mk, mv, wo, g, b, *, bm, rows_per_seq):
    m, d = y.shape
    assert rows_per_seq % bm == 0
    row = lambda i: (i, 0)
    fixed = lambda i: (0, 0)
    mem_spec = pl.BlockSpec((1, N_MEM, D_MEM), lambda i: (i * bm // rows_per_seq, 0, 0))
    return pl.pallas_call(
        functools.partial(_xattn_kernel, n_seq=1),
        grid=(m // bm,),
        in_specs=[
            pl.BlockSpec((bm, d), row),
            pl.BlockSpec((bm, d), row),
            pl.BlockSpec((d, D_MEM), fixed),
            mem_spec,
            mem_spec,
            pl.BlockSpec((D_MEM, d), fixed),
            pl.BlockSpec((1, d), fixed),
            pl.BlockSpec((1, d), fixed),
        ],
        out_specs=[pl.BlockSpec((bm, d), row), pl.BlockSpec((bm, d), row)],
        out_shape=[SDS((m, d), F32), SDS((m, d), BF16)],
        compiler_params=_cparams(1),
        name="xattn_ln",
    )(y, yb, wq, mk, mv, wo, g.reshape(1, d), b.reshape(1, d))


def _s5_discretize(lam_re, lam_im, log_dt, b_re, b_im):
    lr = jnp.minimum(lam_re.astype(F32), -1e-4)
    li = lam_im.astype(F32)
    dt = jnp.exp(log_dt.astype(F32))[:, None]
    mag = jnp.exp(lr * dt)
    a_re = mag * jnp.cos(li * dt)
    a_im = mag * jnp.sin(li * dt)
    den = lr * lr + li * li
    fr = ((a_re - 1.0) * lr + a_im * li) / den
    fi = (a_im * lr - (a_re - 1.0) * li) / den
    br, bi = b_re.astype(F32), b_im.astype(F32)
    bb_re = fr[..., None] * br - fi[..., None] * bi
    bb_im = fr[..., None] * bi + fi[..., None] * br
    return a_re, a_im, bb_re, bb_im


def _s5_block_weights(a_re, a_im, bb_re, bb_im, c_re, c_im, d_skip):
    eye = jnp.eye(S5_GB, dtype=F32)
    shp_b = (S5_NGB, S5_GB, S5_STATE, S5_GROUP)
    shp_c = (S5_NGB, S5_GB, S5_GROUP, S5_STATE)
    blk_b = lambda w: jnp.einsum('bgph,gk->bghkp', w.reshape(shp_b), eye).reshape(S5_NGB, S5_UW, S5_HW)
    blk_c = lambda w: jnp.einsum('bghp,gk->bkpgh', w.reshape(shp_c), eye).reshape(S5_NGB, S5_HW, S5_UW)
    bw = jnp.concatenate([blk_b(bb_re), blk_b(bb_im)], axis=-1).astype(BF16)
    cw = jnp.concatenate([blk_c(c_re.astype(F32)), blk_c(-c_im.astype(F32))], axis=1).astype(BF16)
    return (bw, cw, a_re.reshape(S5_NGB, 1, S5_HW), a_im.reshape(S5_NGB, 1, S5_HW),
            d_skip.astype(F32).reshape(S5_NGB, 1, S5_UW))


def _s5_state_to_blocks(s_re, s_im):
    b = s_re.shape[0]
    return jnp.concatenate([s_re.reshape(b, S5_NGB, S5_HW), s_im.reshape(b, S5_NGB, S5_HW)],
                           axis=-1).reshape(b, S5_NGB * 2 * S5_HW)


def _s5_state_from_blocks(h):
    b = h.shape[0]
    return (h[:, :, :S5_HW].reshape(b, S5_GROUPS, S5_STATE), h[:, :, S5_HW:].reshape(b, S5_GROUPS, S5_STATE))


def kernel(x_prompt, x_sample, cache_mem_k, cache_mem_v, cache_swa_k, cache_swa_v, state_s5_re, state_s5_im, state_hgrn, mem_prompt, rel_bias, w_even_in, s5_lam_re, s5_lam_im, s5_log_dt, s5_b_re, s5_b_im, s5_c_re, s5_c_im, s5_d, s5_w_glu, swa_sinks, w_even_out, hg_lb_logits, w_odd_in, hg_norm_g, w_odd_out, w_mem_q, w_mem_k, w_mem_v, w_mem_o, w_ffn_gate, w_ffn_up, w_ffn_down, ln_g, ln_b):
    bsz, seq, d = x_prompt.shape
    dec_b, dec_t, _ = x_sample.shape
    mp, ms = bsz * seq, dec_b * dec_t
    w_buf = cache_swa_k.shape[2]
    bf = lambda w: w.astype(BF16)

    lb_soft = jax.nn.softmax(hg_lb_logits.astype(F32), axis=0)
    lower_bounds = jnp.cumsum(lb_soft, axis=0) - lb_soft[0]
    table = bias_table(rel_bias.astype(F32))
    bias_s = jnp.tile(table[:, :dec_t], (1, 2, 1))
    mem_b = bf(mem_prompt).reshape(bsz * N_MEM, d)

    yp = x_prompt.reshape(mp, d)
    ys = x_sample.reshape(ms, d)
    ypb, ysb = None, bf(ys)
    p_mem_k, p_mem_v = [], []
    p_swa_k, p_swa_v, p_s5_re, p_s5_im, p_hg = [], [], [], [], []
    s_swa_k, s_swa_v, s_s5_re, s_s5_im, s_hg = [], [], [], [], []

    for l in range(DEPTH):
        j = l // 2
        g0, b0 = ln_g[l, 0], ln_b[l, 0]
        if l % 2 == 0:
            a_re, a_im, bb_re, bb_im = _s5_discretize(s5_lam_re[j], s5_lam_im[j], s5_log_dt[j], s5_b_re[j], s5_b_im[j])
            s5w = _s5_block_weights(a_re, a_im, bb_re, bb_im, s5_c_re[j], s5_c_im[j], s5_d[j])
            w_glu, w_out = cast_layer(s5_w_glu, j), cast_layer(w_even_out, j)
            sinks = swa_sinks[j].astype(F32)
            kv0 = D_S5 + D_SWA_Q

            hs, w_in = matmul_cast(ysb, w_even_in, j, bn=512)
            hp = (matmul_x32(yp, w_in, bm=512, bn=1024) if ypb is None
                  else matmul(ypb, w_in, bm=1024, bn=1024)).reshape(bsz, seq, -1)
            z, hfin = s5_prompt(hp, *s5w, tb=512)
            s5_out = glu(z.reshape(mp, D_S5), w_glu, bm=1024, bn=1024)
            att = swa_prompt(hp, sinks, table).reshape(mp, D_SWA_Q)
            hr, hi = _s5_state_from_blocks(hfin.reshape(bsz, S5_NGB, 2 * S5_HW))
            p_s5_re.append(hr); p_s5_im.append(hi)
            p_swa_k.append(hp[:, seq - w_buf:, kv0:kv0 + D_SWA_KV].reshape(bsz, w_buf, SWA_KV_HEADS, HEAD_DIM))
            p_swa_v.append(hp[:, seq - w_buf:, kv0 + D_SWA_KV:].reshape(bsz, w_buf, SWA_KV_HEADS, HEAD_DIM))
            yp, ypb = proj_res_ln([s5_out, att], w_out, yp, g0, b0, bm=128)

            h0x = jnp.repeat(_s5_state_to_blocks(state_s5_re[j].astype(F32), state_s5_im[j].astype(F32)), dec_t, axis=0)
            z, h_steps = s5_sample(hs, h0x, *s5w, period=dec_t)
            s5_out = glu(z, w_glu, bm=ms, bn=1024)
            hs3 = hs.reshape(dec_b, dec_t, -1)
            kpad = jnp.zeros((dec_b, 2 * WINDOW - w_buf - dec_t, D_SWA_KV), F32)
            kk = jnp.concatenate([cache_swa_k[j].reshape(dec_b, w_buf, D_SWA_KV).astype(F32),
                                  hs3[:, :, kv0:kv0 + D_SWA_KV], kpad], axis=1)
            vv = jnp.concatenate([cache_swa_v[j].reshape(dec_b, w_buf, D_SWA_KV).astype(F32),
                                  hs3[:, :, kv0 + D_SWA_KV:], kpad], axis=1)
            att = swa_sample(hs, kk, vv, sinks, bias_s, t_new=dec_t)
            hr, hi = _s5_state_from_blocks(h_steps.reshape(dec_b, dec_t, S5_NGB, 2 * S5_HW)[:, dec_t - 1])
            s_s5_re.append(hr); s_s5_im.append(hi)
            s_swa_k.append(kk[:, dec_t:dec_t + w_buf].reshape(dec_b, w_buf, SWA_KV_HEADS, HEAD_DIM))
            s_swa_v.append(vv[:, dec_t:dec_t + w_buf].reshape(dec_b, w_buf, SWA_KV_HEADS, HEAD_DIM))
            ys, ysb = proj_res_ln([s5_out, att], w_out, ys, g0, b0, bm=ms)
        else:
            w_out = cast_layer(w_odd_out, j)
            lb = lower_bounds[l]

            hs, w_in = matmul_cast(ysb, w_odd_in, j, bn=512)
            hp = matmul(ypb, w_in, bm=1024, bn=1024)
            o, st = hgrn_prompt(hp.reshape(bsz, seq, -1), lb, tb=512, n_heads=4)
            p_hg.append(jnp.swapaxes(st, -1, -2))
            yp, ypb = hg_out_ln(o.reshape(mp, d), hp, hg_norm_g[j], w_out, yp, g0, b0, bm=128)

            o, s_new = hgrn_sample(hs, lb, state_hgrn[j].astype(F32), t_new=dec_t, n_heads=16)
            s_hg.append(s_new)
            ys, ysb = hg_out_ln(o, hs, hg_norm_g[j], w_out, ys, g0, b0, bm=ms)

        wo = cast_layer(w_mem_o, l)
        mk = matmul_w32(mem_b, w_mem_k, l, bn=D_MEM).reshape(bsz, N_MEM, D_MEM)
        mv = matmul_w32(mem_b, w_mem_v, l, bn=D_MEM).reshape(bsz, N_MEM, D_MEM)
        p_mem_k.append(mk.reshape(bsz, N_MEM, MEM_HEADS, MEM_HEAD_DIM))
        p_mem_v.append(mv.reshape(bsz, N_MEM, MEM_HEADS, MEM_HEAD_DIM))
        qs, wq = matmul_cast(ysb, w_mem_q, l, bn=D_MEM, out_dtype=BF16)
        yp, ypb = xattn_ln(yp, ypb, wq, mk, mv, wo, ln_g[l, 1], ln_b[l, 1], bm=256, rows_per_seq=seq)
        att = mem_attn_cache(qs, cache_mem_k, cache_mem_v, l, rows_per_seq=dec_t)
        ys, ysb = proj_res_ln([att], wo, ys, ln_g[l, 1], ln_b[l, 1], bm=ms)

        ys, ysb, wgu, wd = ffn_ln_cast(ys, w_ffn_gate, w_ffn_up, w_ffn_down, l, ln_g[l, 2], ln_b[l, 2])
        yp, ypb = ffn_ln(yp, wgu, wd, ln_g[l, 2], ln_b[l, 2], bm=512)

    return (yp.reshape(bsz, seq, d), ys.reshape(dec_b, dec_t, d),
            jnp.stack(p_mem_k), jnp.stack(p_mem_v),
            jnp.stack(p_swa_k), jnp.stack(p_swa_v),
            jnp.stack(p_s5_re), jnp.stack(p_s5_im), jnp.stack(p_hg),
            jnp.stack(s_swa_k), jnp.stack(s_swa_v),
            jnp.stack(s_s5_re), jnp.stack(s_s5_im), jnp.stack(s_hg))
```

```python
import functools
import math

import jax
import jax.numpy as jnp
from jax import lax
from jax.experimental import pallas as pl
from jax.experimental.pallas import tpu as pltpu

F32 = jnp.float32
BF16 = jnp.bfloat16
SDS = jax.ShapeDtypeStruct

D_MODEL = 4096
DEPTH = 2
ALPHA = (2 * DEPTH) ** 0.25
LN_EPS = 1e-5
RMS_EPS = 1e-6
NEG_BIG = -1e30

S5_GROUPS, S5_GROUP, S5_STATE = 128, 16, 64
D_S5 = S5_GROUPS * S5_GROUP
S5_GB = 16
S5_NGB = S5_GROUPS // S5_GB
S5_UW = S5_GB * S5_GROUP
S5_HW = S5_GB * S5_STATE

HEAD_DIM, SWA_HEADS, SWA_KV_HEADS, SWA_GQ = 64, 32, 8, 4
D_SWA_Q, D_SWA_KV = 2048, 512
WINDOW = 128
N_BUCKETS, MAX_DISTANCE = 32, 128

HG_DK, HG_HEADS, HG_CHUNK = 128, 32, 32
N_MEM, MEM_HEADS, MEM_HEAD_DIM, D_MEM = 256, 4, 128, 512

VMEM_LIMIT = 56 * 1024 * 1024
VMEM_LIMIT_MAX = 60 * 1024 * 1024
PROJ_COL_CHUNKS = 4
LN_ROW_CHUNK = 128


def _cparams(n_axes, vmem=VMEM_LIMIT):
    return pltpu.CompilerParams(dimension_semantics=("arbitrary",) * n_axes, vmem_limit_bytes=vmem)


def _dot(a, b):
    return jnp.dot(a, b, preferred_element_type=F32)


def _dot_nt(a, b):
    return lax.dot_general(a, b, (((1,), (1,)), ((), ())), preferred_element_type=F32)


def _dot_tn(a, b):
    return lax.dot_general(a, b, (((0,), (0,)), ((), ())), preferred_element_type=F32)


def _layer_norm_rows(s, g, b):
    mu = jnp.mean(s, axis=-1, keepdims=True)
    c = s - mu
    var = jnp.mean(c * c, axis=-1, keepdims=True)
    return c * lax.rsqrt(var + LN_EPS) * g + b


def _proj_res_sum(xs, w_ref, res_ref, s_ref):
    d = s_ref.shape[1]
    cw = d // PROJ_COL_CHUNKS
    total = None
    for j in range(PROJ_COL_CHUNKS):
        cols = slice(j * cw, (j + 1) * cw)
        s = ALPHA * res_ref[:, cols]
        off = 0
        for x in xs:
            s = s + _dot(x, w_ref[off:off + x.shape[1], cols])
            off += x.shape[1]
        s_ref[:, cols] = s
        part = jnp.sum(s, axis=-1, keepdims=True)
        total = part if total is None else total + part
    return total


def _ln_rows_from(s_ref, total, g_ref, b_ref, y_ref, yb_ref):
    d = s_ref.shape[1]
    cw = d // PROJ_COL_CHUNKS
    mu = total * (1.0 / d)
    sq = None
    for j in range(PROJ_COL_CHUNKS):
        c = s_ref[:, j * cw:(j + 1) * cw] - mu
        part = jnp.sum(c * c, axis=-1, keepdims=True)
        sq = part if sq is None else sq + part
    rstd = lax.rsqrt(sq * (1.0 / d) + LN_EPS)
    for j in range(PROJ_COL_CHUNKS):
        cols = slice(j * cw, (j + 1) * cw)
        y = (s_ref[:, cols] - mu) * rstd * g_ref[:, cols] + b_ref[:, cols]
        y_ref[:, cols] = y
        yb_ref[:, cols] = y.astype(BF16)


def _proj_res_ln(xs, w_ref, res_ref, g_ref, b_ref, y_ref, yb_ref):
    total = _proj_res_sum(xs, w_ref, res_ref, y_ref)
    _ln_rows_from(y_ref, total, g_ref, b_ref, y_ref, yb_ref)


def _cast_kernel(w_ref, o_ref):
    o_ref[...] = w_ref[0].astype(o_ref.dtype)


CAST_BLOCK_BYTES = 8 * 1024 * 1024


def cast_layer(w, layer):
    _, k, n = w.shape
    rows = next(r for r in (4096, 2048, 1024, 512, 256, 128) if k % r == 0 and r * n * 4 <= CAST_BLOCK_BYTES)
    return pl.pallas_call(
        _cast_kernel,
        grid=(k // rows,),
        in_specs=[pl.BlockSpec((1, rows, n), lambda i: (layer, i, 0))],
        out_specs=pl.BlockSpec((rows, n), lambda i: (i, 0)),
        out_shape=SDS((k, n), BF16),
        compiler_params=_cparams(1),
        name="cast_layer",
    )(w)


def _mm_kernel(x_ref, w_ref, o_ref):
    o_ref[...] = _dot(x_ref[...], w_ref[...]).astype(o_ref.dtype)


def matmul(x, w, *, bm, bn, out_dtype=F32):
    m, k = x.shape
    n = w.shape[1]
    return pl.pallas_call(
        _mm_kernel,
        grid=(m // bm, n // bn),
        in_specs=[pl.BlockSpec((bm, k), lambda i, j: (i, 0)), pl.BlockSpec((k, bn), lambda i, j: (0, j))],
        out_specs=pl.BlockSpec((bm, bn), lambda i, j: (i, j)),
        out_shape=SDS((m, n), out_dtype),
        compiler_params=_cparams(2),
        name="matmul",
    )(x, w)


def _mm_x32_kernel(x_ref, w_ref, o_ref, xb_ref):
    @pl.when(pl.program_id(1) == 0)
    def _():
        xb_ref[...] = x_ref[...].astype(BF16)

    o_ref[...] = _dot(xb_ref[...], w_ref[...]).astype(o_ref.dtype)


def matmul_x32(x, w, *, bm, bn, out_dtype=F32):
    m, k = x.shape
    n = w.shape[1]
    return pl.pallas_call(
        _mm_x32_kernel,
        grid=(m // bm, n // bn),
        in_specs=[pl.BlockSpec((bm, k), lambda i, j: (i, 0)), pl.BlockSpec((k, bn), lambda i, j: (0, j))],
        out_specs=pl.BlockSpec((bm, bn), lambda i, j: (i, j)),
        out_shape=SDS((m, n), out_dtype),
        scratch_shapes=[pltpu.VMEM((bm, k), BF16)],
        compiler_params=_cparams(2),
        name="matmul_x32",
    )(x, w)


def _mm_w32_kernel(x_ref, w_ref, o_ref):
    o_ref[...] = _dot(x_ref[...], w_ref[0].astype(BF16)).astype(o_ref.dtype)


def matmul_w32(x, w, layer, *, bn, out_dtype=F32):
    m, k = x.shape
    n = w.shape[2]
    return pl.pallas_call(
        _mm_w32_kernel,
        grid=(n // bn,),
        in_specs=[pl.BlockSpec((m, k), lambda j: (0, 0)), pl.BlockSpec((1, k, bn), lambda j: (layer, 0, j))],
        out_specs=pl.BlockSpec((m, bn), lambda j: (0, j)),
        out_shape=SDS((m, n), out_dtype),
        compiler_params=_cparams(1),
        name="matmul_w32",
    )(x, w)


def _mm_cast_kernel(x_ref, w_ref, o_ref, wb_ref):
    wb_ref[...] = w_ref[0].astype(BF16)
    o_ref[...] = _dot(x_ref[...], wb_ref[...]).astype(o_ref.dtype)


def matmul_cast(x, w, layer, *, bn, out_dtype=F32):
    m, k = x.shape
    n = w.shape[2]
    return pl.pallas_call(
        _mm_cast_kernel,
        grid=(n // bn,),
        in_specs=[pl.BlockSpec((m, k), lambda j: (0, 0)), pl.BlockSpec((1, k, bn), lambda j: (layer, 0, j))],
        out_specs=[pl.BlockSpec((m, bn), lambda j: (0, j)), pl.BlockSpec((k, bn), lambda j: (0, j))],
        out_shape=[SDS((m, n), out_dtype), SDS((k, n), BF16)],
        compiler_params=_cparams(1),
        name="matmul_cast",
    )(x, w)


def _glu_kernel(z_ref, zt_ref, w_ref, o_ref):
    a = _dot(z_ref[...], w_ref[...])
    o_ref[...] = (zt_ref[...].astype(F32) * jax.nn.sigmoid(a)).astype(o_ref.dtype)


def glu(z, w, *, bm, bn):
    m, k = z.shape
    n = w.shape[1]
    return pl.pallas_call(
        _glu_kernel,
        grid=(m // bm, n // bn),
        in_specs=[pl.BlockSpec((bm, k), lambda i, j: (i, 0)), pl.BlockSpec((bm, bn), lambda i, j: (i, j)),
                  pl.BlockSpec((k, bn), lambda i, j: (0, j))],
        out_specs=pl.BlockSpec((bm, bn), lambda i, j: (i, j)),
        out_shape=SDS((m, n), BF16),
        compiler_params=_cparams(2),
        name="glu",
    )(z, z, w)


def _proj_ln_kernel(*refs, n_in):
    xs = refs[:n_in]
    w_ref, res_ref, g_ref, b_ref, y_ref, yb_ref = refs[n_in:]
    _proj_res_ln([x_ref[...] for x_ref in xs], w_ref, res_ref, g_ref, b_ref, y_ref, yb_ref)


def proj_res_ln(xs, w, res, g, b, *, bm):
    m, d = res.shape
    k = w.shape[0]
    row = lambda i: (i, 0)
    fixed = lambda i: (0, 0)
    return pl.pallas_call(
        functools.partial(_proj_ln_kernel, n_in=len(xs)),
        grid=(m // bm,),
        in_specs=[pl.BlockSpec((bm, x.shape[1]), row) for x in xs] + [
            pl.BlockSpec((k, d), fixed, pipeline_mode=pl.Buffered(1)),
            pl.BlockSpec((bm, d), row),
            pl.BlockSpec((1, d), fixed),
            pl.BlockSpec((1, d), fixed),
        ],
        out_specs=[pl.BlockSpec((bm, d), row), pl.BlockSpec((bm, d), row)],
        out_shape=[SDS((m, d), F32), SDS((m, d), BF16)],
        compiler_params=_cparams(1),
        name="proj_res_ln",
    )(*xs, w, res, g.reshape(1, d), b.reshape(1, d))


def _hg_out_kernel(o_ref, gate_ref, ng_ref, w_ref, res_ref, g_ref, b_ref, y_ref, yb_ref):
    o = o_ref[...]
    xn = o * lax.rsqrt(jnp.mean(o * o, axis=-1, keepdims=True) + RMS_EPS) * ng_ref[...]
    xn = xn * jax.nn.sigmoid(gate_ref[...])
    _proj_res_ln([xn.astype(BF16)], w_ref, res_ref, g_ref, b_ref, y_ref, yb_ref)


def hg_out_ln(o, h_all, norm_g, w, res, g, b, *, bm):
    m, d = res.shape
    row = lambda i: (i, 0)
    fixed = lambda i: (0, 0)
    return pl.pallas_call(
        _hg_out_kernel,
        grid=(m // bm,),
        in_specs=[
            pl.BlockSpec((bm, d), row),
            pl.BlockSpec((bm, d), lambda i: (i, 3)),
            pl.BlockSpec((1, d), fixed),
            pl.BlockSpec((d, d), fixed, pipeline_mode=pl.Buffered(1)),
            pl.BlockSpec((bm, d), row),
            pl.BlockSpec((1, d), fixed),
            pl.BlockSpec((1, d), fixed),
        ],
        out_specs=[pl.BlockSpec((bm, d), row), pl.BlockSpec((bm, d), row)],
        out_shape=[SDS((m, d), F32), SDS((m, d), BF16)],
        compiler_params=_cparams(1),
        name="hg_out_ln",
    )(o, h_all, norm_g.reshape(1, d), w, res, g.reshape(1, d), b.reshape(1, d))


def _ffn_step(f, x_ref, wgu_ref, wd_ref, g_ref, b_ref, y_ref, yb_ref):
    @pl.when(f == 0)
    def _():
        x = x_ref[...]
        yb_ref[...] = x.astype(BF16)
        y_ref[...] = ALPHA * x

    bf = wd_ref.shape[0]
    gu = _dot(yb_ref[...], wgu_ref[...])
    gate, up = gu[:, :bf], gu[:, bf:]
    h = (gate * jax.nn.sigmoid(gate) * up).astype(BF16)
    y_ref[...] += _dot(h, wd_ref[...])

    @pl.when(f == pl.num_programs(1) - 1)
    def _():
        g, b = g_ref[...], b_ref[...]
        chunk = math.gcd(y_ref.shape[0], LN_ROW_CHUNK)

        def ln_chunk(i, _):
            rows = pl.ds(pl.multiple_of(i * chunk, chunk), chunk)
            y = _layer_norm_rows(y_ref[rows, :], g, b)
            y_ref[rows, :] = y
            yb_ref[rows, :] = y.astype(BF16)
            return 0

        lax.fori_loop(0, y_ref.shape[0] // chunk, ln_chunk, 0)


def _ffn_kernel(x_ref, wgu_ref, wd_ref, g_ref, b_ref, y_ref, yb_ref):
    _ffn_step(pl.program_id(1), x_ref, wgu_ref.at[0], wd_ref, g_ref, b_ref, y_ref, yb_ref)


def ffn_ln(x, wgu, wd, g, b, *, bm):
    m, d = x.shape
    nf, _, bf2 = wgu.shape
    return pl.pallas_call(
        _ffn_kernel,
        grid=(m // bm, nf),
        in_specs=[
            pl.BlockSpec((bm, d), lambda i, f: (i, 0)),
            pl.BlockSpec((1, d, bf2), lambda i, f: (f, 0, 0)),
            pl.BlockSpec((bf2 // 2, d), lambda i, f: (f, 0)),
            pl.BlockSpec((1, d), lambda i, f: (0, 0)),
            pl.BlockSpec((1, d), lambda i, f: (0, 0)),
        ],
        out_specs=[pl.BlockSpec((bm, d), lambda i, f: (i, 0)), pl.BlockSpec((bm, d), lambda i, f: (i, 0))],
        out_shape=[SDS((m, d), F32), SDS((m, d), BF16)],
        compiler_params=_cparams(2, vmem=VMEM_LIMIT_MAX),
        name="ffn_ln",
    )(x, wgu, wd, g.reshape(1, d), b.reshape(1, d))


def _ffn_cast_kernel(x_ref, wg_ref, wu_ref, wd_ref, g_ref, b_ref, y_ref, yb_ref, wgub_ref, wdb_ref):
    bf = wd_ref.shape[1]
    wgub_ref[0, :, :bf] = wg_ref[0].astype(BF16)
    wgub_ref[0, :, bf:] = wu_ref[0].astype(BF16)
    wdb_ref[...] = wd_ref[0].astype(BF16)
    _ffn_step(pl.program_id(1), x_ref, wgub_ref.at[0], wdb_ref, g_ref, b_ref, y_ref, yb_ref)


def ffn_ln_cast(x, wg, wu, wd, layer, g, b, *, bf=256):
    m, d = x.shape
    dff = wg.shape[2]
    fixed = lambda i, f: (0, 0)
    return pl.pallas_call(
        _ffn_cast_kernel,
        grid=(1, dff // bf),
        in_specs=[
            pl.BlockSpec((m, d), fixed),
            pl.BlockSpec((1, d, bf), lambda i, f: (layer, 0, f)),
            pl.BlockSpec((1, d, bf), lambda i, f: (layer, 0, f)),
            pl.BlockSpec((1, bf, d), lambda i, f: (layer, f, 0)),
            pl.BlockSpec((1, d), fixed),
            pl.BlockSpec((1, d), fixed),
        ],
        out_specs=[
            pl.BlockSpec((m, d), fixed),
            pl.BlockSpec((m, d), fixed),
            pl.BlockSpec((1, d, 2 * bf), lambda i, f: (f, 0, 0)),
            pl.BlockSpec((bf, d), lambda i, f: (f, 0)),
        ],
        out_shape=[SDS((m, d), F32), SDS((m, d), BF16), SDS((dff // bf, d, 2 * bf), BF16), SDS((dff, d), BF16)],
        compiler_params=_cparams(2),
        name="ffn_ln_cast",
    )(x, wg, wu, wd, g.reshape(1, d), b.reshape(1, d))


def _cmul(ar, ai, br, bi):
    return ar * br - ai * bi, ar * bi + ai * br


def _s5_coefs(a_r, a_i, period):
    l = a_r.shape[1]
    pows = [(a_r, a_i)]
    for _ in range(period - 1):
        pows.append(_cmul(pows[-1][0], pows[-1][1], a_r, a_i))
    t = lax.broadcasted_iota(jnp.int32, (8, l), 0) & (period - 1)
    shifts = []
    k = 1
    while k < period:
        keep = t >= k
        shifts.append((jnp.where(keep, pows[k - 1][0], 0.0), jnp.where(keep, pows[k - 1][1], 0.0)))
        k *= 2
    p_r = jnp.broadcast_to(pows[0][0], (8, l))
    p_i = jnp.broadcast_to(pows[0][1], (8, l))
    for j in range(1, period):
        p_r = jnp.where(t == j, pows[j][0], p_r)
        p_i = jnp.where(t == j, pows[j][1], p_i)
    return shifts, (p_r, p_i)


def _s5_scan_tile(x_r, x_i, shifts):
    k = 1
    for c_r, c_i in shifts:
        s_r = pltpu.roll(x_r, k, 0)
        s_i = pltpu.roll(x_i, k, 0)
        x_r, x_i = x_r + c_r * s_r - c_i * s_i, x_i + c_r * s_i + c_i * s_r
        k *= 2
    return x_r, x_i


S5_CHAINS = 4
S5_LANE_BLOCKS = 2 * S5_HW // 128


def _s5_prompt_kernel(u_ref, bw_ref, cw_ref, ar_ref, ai_ref, d_ref, z_ref, hfin_ref, h_scr, carry_ref):
    tt = pl.program_id(2)
    n_tiles = h_scr.shape[1]
    tb = n_tiles * 8
    half = S5_LANE_BLOCKS // 2
    chains = range(S5_CHAINS)

    @pl.when(tt == 0)
    def _():
        carry_ref[...] = jnp.zeros_like(carry_ref)

    for ch in chains:
        bu = _dot(u_ref[0, :, ch * S5_UW:(ch + 1) * S5_UW].astype(BF16), bw_ref[ch])
        for s in range(S5_LANE_BLOCKS):
            h_scr[ch, :, s * 8:(s + 1) * 8, :] = bu[:, s * 128:(s + 1) * 128].reshape(n_tiles, 8, 128)
    a = [(ar_ref[ch], ai_ref[ch]) for ch in chains]

    def tile(i, carry):
        hs = list(carry)
        for r in range(8):
            for ch in chains:
                (a_r, a_i), (h_r, h_i) = a[ch], hs[ch]
                re_rows = pl.ds(r, half, stride=8)
                im_rows = pl.ds(8 * half + r, half, stride=8)
                n_r = a_r * h_r - a_i * h_i + h_scr[ch, i, re_rows, :]
                n_i = a_r * h_i + a_i * h_r + h_scr[ch, i, im_rows, :]
                h_scr[ch, i, re_rows, :] = n_r
                h_scr[ch, i, im_rows, :] = n_i
                hs[ch] = (n_r, n_i)
        return tuple(hs)

    final = lax.fori_loop(0, n_tiles, tile, tuple((carry_ref[ch, 0], carry_ref[ch, 1]) for ch in chains))
    for ch in chains:
        carry_ref[ch, 0], carry_ref[ch, 1] = final[ch]
    for ch in chains:
        cols = slice(ch * S5_UW, (ch + 1) * S5_UW)
        h = jnp.concatenate([h_scr[ch, :, s * 8:(s + 1) * 8, :].reshape(tb, 128) for s in range(S5_LANE_BLOCKS)],
                            axis=1)
        y = _dot(h.astype(BF16), cw_ref[ch]) + d_ref[ch] * u_ref[0, :, cols]
        z_ref[0, :, cols] = jax.nn.gelu(y).astype(z_ref.dtype)

    @pl.when(tt == pl.num_programs(2) - 1)
    def _():
        hfin_ref[0] = carry_ref[...]


def s5_prompt(h_all, bw, cw, a_r, a_i, d_skip, *, tb):
    bsz, t, _ = h_all.shape
    nc = S5_CHAINS
    blk = lambda b, g, s: (g, 0, 0)
    rows = pl.BlockSpec((1, tb, nc * S5_UW), lambda b, g, s: (b, s, g))
    return pl.pallas_call(
        _s5_prompt_kernel,
        grid=(bsz, S5_NGB // nc, t // tb),
        in_specs=[
            rows,
            pl.BlockSpec((nc, S5_UW, 2 * S5_HW), blk),
            pl.BlockSpec((nc, 2 * S5_HW, S5_UW), blk),
            pl.BlockSpec((nc, 8, 128), blk),
            pl.BlockSpec((nc, 8, 128), blk),
            pl.BlockSpec((nc, 1, S5_UW), blk),
        ],
        out_specs=[rows, pl.BlockSpec((1, nc, 2, 8, 128), lambda b, g, s: (b, g, 0, 0, 0))],
        out_shape=[SDS((bsz, t, D_S5), BF16), SDS((bsz, S5_NGB, 2, 8, 128), F32)],
        scratch_shapes=[pltpu.VMEM((nc, tb // 8, S5_LANE_BLOCKS * 8, 128), F32), pltpu.VMEM((nc, 2, 8, 128), F32)],
        compiler_params=_cparams(3),
        name="s5_prompt",
    )(h_all, bw, cw, a_r.reshape(S5_NGB, 8, 128), a_i.reshape(S5_NGB, 8, 128), d_skip)


def _s5_sample_kernel(u_ref, h0_ref, bw_ref, cw_ref, ar_ref, ai_ref, d_ref, z_ref, h_ref, *, period):
    u = u_ref[...]
    bu = _dot(u.astype(BF16), bw_ref[0])
    shifts, (p_r, p_i) = _s5_coefs(ar_ref[0], ai_ref[0], period)
    for i in range(u.shape[0] // 8):
        rows = slice(i * 8, (i + 1) * 8)
        x_r, x_i = _s5_scan_tile(bu[rows, 0:S5_HW], bu[rows, S5_HW:2 * S5_HW], shifts)
        h0_r = h0_ref[rows, 0:S5_HW]
        h0_i = h0_ref[rows, S5_HW:2 * S5_HW]
        h_ref[rows, 0:S5_HW] = x_r + p_r * h0_r - p_i * h0_i
        h_ref[rows, S5_HW:2 * S5_HW] = x_i + p_r * h0_i + p_i * h0_r
    y = _dot(h_ref[...].astype(BF16), cw_ref[0]) + d_ref[0] * u
    z_ref[...] = jax.nn.gelu(y).astype(z_ref.dtype)


def s5_sample(h_all, h0x, bw, cw, a_r, a_i, d_skip, *, period):
    m = h_all.shape[0]
    blk = lambda g: (g, 0, 0)
    return pl.pallas_call(
        functools.partial(_s5_sample_kernel, period=period),
        grid=(S5_NGB,),
        in_specs=[
            pl.BlockSpec((m, S5_UW), lambda g: (0, g)),
            pl.BlockSpec((m, 2 * S5_HW), lambda g: (0, g)),
            pl.BlockSpec((1, S5_UW, 2 * S5_HW), blk),
            pl.BlockSpec((1, 2 * S5_HW, S5_UW), blk),
            pl.BlockSpec((1, 1, S5_HW), blk),
            pl.BlockSpec((1, 1, S5_HW), blk),
            pl.BlockSpec((1, 1, S5_UW), blk),
        ],
        out_specs=[pl.BlockSpec((m, S5_UW), lambda g: (0, g)), pl.BlockSpec((m, 2 * S5_HW), lambda g: (0, g))],
        out_shape=[SDS((m, D_S5), BF16), SDS((m, S5_NGB * 2 * S5_HW), F32)],
        compiler_params=_cparams(1),
        name="s5_sample",
    )(h_all, h0x, bw, cw, a_r, a_i, d_skip)


def _bias_kernel(rb_ref, o_ref):
    h = pl.program_id(0)
    q = lax.broadcasted_iota(jnp.int32, (WINDOW, 2 * WINDOW), 0)
    k = lax.broadcasted_iota(jnp.int32, (WINDOW, 2 * WINDOW), 1)
    n = jnp.maximum(q + WINDOW - k, 0)
    max_exact = N_BUCKETS // 2
    nf = jnp.maximum(n, 1).astype(F32)
    large = max_exact + (jnp.log(nf / max_exact) / math.log(MAX_DISTANCE / max_exact)
                         * (N_BUCKETS - max_exact)).astype(jnp.int32)
    large = jnp.minimum(large, N_BUCKETS - 1)
    bucket = jnp.where(n < max_exact, n, large)
    out = jnp.zeros((WINDOW, 2 * WINDOW), F32)
    for b in range(N_BUCKETS):
        out = jnp.where(bucket == b, rb_ref[b, h], out)
    o_ref[0] = out


def bias_table(rel_bias):
    return pl.pallas_call(
        _bias_kernel,
        grid=(SWA_HEADS,),
        in_specs=[pl.BlockSpec(memory_space=pltpu.SMEM)],
        out_specs=pl.BlockSpec((1, WINDOW, 2 * WINDOW), lambda h: (h, 0, 0)),
        out_shape=SDS((SWA_HEADS, WINDOW, 2 * WINDOW), F32),
        compiler_params=_cparams(1),
        name="bias_table",
    )(rel_bias)


def _swa_softmax(s, valid, sink_col):
    s = jnp.where(valid, s, NEG_BIG)
    m = jnp.maximum(jnp.max(s, axis=-1, keepdims=True), sink_col)
    p = jnp.exp(s - m)
    return p / (jnp.sum(p, axis=-1, keepdims=True) + jnp.exp(sink_col - m))


def _sink_col(sinks_ref, first_head, rows_per_head):
    n = SWA_GQ * rows_per_head
    r = lax.broadcasted_iota(jnp.int32, (n, 1), 0)
    col = jnp.full((n, 1), sinks_ref[first_head], F32)
    for g in range(1, SWA_GQ):
        col = jnp.where(r >= g * rows_per_head, sinks_ref[first_head + g], col)
    return col


SWA_KV_PER_STEP = 8


def _swa_prompt_kernel(sinks_ref, q_ref, kp_ref, kc_ref, vp_ref, vc_ref, bias_ref, o_ref):
    nkv = SWA_KV_PER_STEP
    part = pl.program_id(0)
    i = pl.program_id(2)
    blk = WINDOW
    q = q_ref[0].astype(BF16)
    kk = jnp.concatenate([kp_ref[0], kc_ref[0]], axis=0).astype(BF16)
    vv = jnp.concatenate([vp_ref[0], vc_ref[0]], axis=0).astype(BF16)
    qi = lax.broadcasted_iota(jnp.int32, (SWA_GQ * blk, 2 * blk), 0) & (blk - 1)
    ki = lax.broadcasted_iota(jnp.int32, (SWA_GQ * blk, 2 * blk), 1)
    valid = (ki > qi) & (ki <= qi + blk) & ((ki >= blk) | (i > 0))
    heads = [slice(j * HEAD_DIM, (j + 1) * HEAD_DIM) for j in range(nkv)]
    q4 = [jnp.concatenate([q[:, (j * SWA_GQ + g) * HEAD_DIM:(j * SWA_GQ + g + 1) * HEAD_DIM]
                           for g in range(SWA_GQ)], axis=0) for j in range(nkv)]
    s = [_dot_nt(q4[j], kk[:, c]) * (HEAD_DIM ** -0.5) for j, c in enumerate(heads)]
    p = [_swa_softmax(s[j] + bias_ref[j * SWA_GQ:(j + 1) * SWA_GQ].reshape(SWA_GQ * blk, 2 * blk), valid,
                      _sink_col(sinks_ref, (part * nkv + j) * SWA_GQ, blk)).astype(BF16) for j in range(nkv)]
    o4 = [_dot(p[j], vv[:, c]) for j, c in enumerate(heads)]
    o_ref[0] = jnp.concatenate([o4[j][g * blk:(g + 1) * blk] for j in range(nkv) for g in range(SWA_GQ)],
                               axis=1).astype(o_ref.dtype)


def swa_prompt(h_all, sinks, table):
    bsz, t, _ = h_all.shape
    nblk = t // WINDOW
    nkv = SWA_KV_PER_STEP
    qw = nkv * SWA_GQ * HEAD_DIM
    kw = nkv * HEAD_DIM
    q0, k0, v0 = D_S5 // qw, (D_S5 + D_SWA_Q) // kw, (D_S5 + D_SWA_Q + D_SWA_KV) // kw
    cur = lambda c0: (lambda p, b, i: (b, i, c0 + p))
    prev = lambda c0: (lambda p, b, i: (b, jnp.maximum(i - 1, 0), c0 + p))
    return pl.pallas_call(
        _swa_prompt_kernel,
        grid=(SWA_KV_HEADS // nkv, bsz, nblk),
        in_specs=[
            pl.BlockSpec(memory_space=pltpu.SMEM),
            pl.BlockSpec((1, WINDOW, qw), cur(q0)),
            pl.BlockSpec((1, WINDOW, kw), prev(k0)),
            pl.BlockSpec((1, WINDOW, kw), cur(k0)),
            pl.BlockSpec((1, WINDOW, kw), prev(v0)),
            pl.BlockSpec((1, WINDOW, kw), cur(v0)),
            pl.BlockSpec((nkv * SWA_GQ, WINDOW, 2 * WINDOW), lambda p, b, i: (p, 0, 0)),
        ],
        out_specs=pl.BlockSpec((1, WINDOW, qw), lambda p, b, i: (b, i, p)),
        out_shape=SDS((bsz, t, D_SWA_Q), BF16),
        compiler_params=_cparams(3),
        name="swa_prompt",
    )(sinks, h_all, h_all, h_all, h_all, h_all, table)


def _swa_sample_kernel(sinks_ref, q_ref, kk_ref, vv_ref, bias_ref, o_ref, *, t_new):
    rows = q_ref.shape[0]
    n_keys = kk_ref.shape[1]
    q = q_ref[...].astype(BF16)
    r = lax.broadcasted_iota(jnp.int32, (SWA_GQ * rows, n_keys), 0)
    ti = r & (t_new - 1)
    ki = lax.broadcasted_iota(jnp.int32, (SWA_GQ * rows, n_keys), 1)
    valid = (ki > ti) & (ki <= ti + WINDOW)
    second = (r & (rows - 1)) >= t_new
    second_o = (lax.broadcasted_iota(jnp.int32, (SWA_GQ * rows, HEAD_DIM), 0) & (rows - 1)) >= t_new
    nkv = SWA_KV_HEADS
    heads = [slice(j * HEAD_DIM, (j + 1) * HEAD_DIM) for j in range(nkv)]
    kk = [kk_ref[bb].astype(BF16) for bb in range(2)]
    vv = [vv_ref[bb].astype(BF16) for bb in range(2)]
    q4 = [jnp.concatenate([q[:, (j * SWA_GQ + g) * HEAD_DIM:(j * SWA_GQ + g + 1) * HEAD_DIM]
                           for g in range(SWA_GQ)], axis=0) for j in range(nkv)]
    s_seq = [[_dot_nt(q4[j], kk[bb][:, c]) for j, c in enumerate(heads)] for bb in range(2)]
    p = []
    for j in range(nkv):
        s = jnp.where(second, s_seq[1][j], s_seq[0][j]) * (HEAD_DIM ** -0.5)
        s = s + bias_ref[j * SWA_GQ:(j + 1) * SWA_GQ].reshape(SWA_GQ * rows, n_keys)
        p.append(_swa_softmax(s, valid, _sink_col(sinks_ref, j * SWA_GQ, rows)).astype(BF16))
    o_seq = [[_dot(p[j], vv[bb][:, c]) for j, c in enumerate(heads)] for bb in range(2)]
    o4 = [jnp.where(second_o, o_seq[1][j], o_seq[0][j]) for j in range(nkv)]
    o_ref[...] = jnp.concatenate([o4[j][g * rows:(g + 1) * rows] for j in range(nkv) for g in range(SWA_GQ)],
                                 axis=1).astype(o_ref.dtype)


def swa_sample(h_all, kk, vv, sinks, bias_s, *, t_new):
    m = h_all.shape[0]
    n_keys = kk.shape[1]
    rows = 2 * t_new
    return pl.pallas_call(
        functools.partial(_swa_sample_kernel, t_new=t_new),
        grid=(m // rows,),
        in_specs=[
            pl.BlockSpec(memory_space=pltpu.SMEM),
            pl.BlockSpec((rows, D_SWA_Q), lambda i: (i, D_S5 // D_SWA_Q)),
            pl.BlockSpec((2, n_keys, D_SWA_KV), lambda i: (i, 0, 0)),
            pl.BlockSpec((2, n_keys, D_SWA_KV), lambda i: (i, 0, 0)),
            pl.BlockSpec((SWA_HEADS, rows, n_keys), lambda i: (0, 0, 0)),
        ],
        out_specs=pl.BlockSpec((rows, D_SWA_Q), lambda i: (i, 0)),
        out_shape=SDS((m, D_SWA_Q), BF16),
        compiler_params=_cparams(1),
        name="swa_sample",
    )(sinks, h_all, kk, vv, bias_s)


def _hg_gates(q, fz, lb):
    qs = q * jax.nn.sigmoid(q)
    f = lb + (1.0 - lb) * jax.nn.sigmoid(fz)
    return qs, jnp.log(f), 1.0 - f


HG_GROUP = 4


def _hgrn_prompt_kernel(q_ref, f_ref, v_ref, lb_ref, o_ref, st_ref, s_scr, *, n_heads):
    tt = pl.program_id(2)
    n_rows = q_ref.shape[1]
    c, ng = HG_CHUNK, HG_GROUP
    gr = c * ng

    @pl.when(tt == 0)
    def _():
        s_scr[...] = jnp.zeros_like(s_scr)

    ri = lax.broadcasted_iota(jnp.int32, (gr, gr), 0)
    ci = lax.broadcasted_iota(jnp.int32, (gr, gr), 1)
    causal = (ri >= ci) & ((ri // c) == (ci // c))
    t_in = lax.broadcasted_iota(jnp.int32, (gr, HG_DK), 0) & (c - 1)
    own = ((lax.broadcasted_iota(jnp.int32, (gr, ng * HG_DK), 0) // c)
           == (lax.broadcasted_iota(jnp.int32, (gr, ng * HG_DK), 1) // HG_DK)).astype(BF16)

    def body(gi, _):
        r0 = pl.multiple_of(gi * gr, gr)
        rows = pl.ds(r0, gr)
        hs = range(n_heads)
        heads = [slice(hd * HG_DK, (hd + 1) * HG_DK) for hd in hs]
        v, qt, kt, kl, decay = [], [], [], [], []
        for cols in heads:
            qs, lf, k = _hg_gates(q_ref[0, rows, cols], f_ref[0, rows, cols], lb_ref[:, cols])
            v.append(v_ref[0, rows, cols].astype(BF16))
            cum = lf
            sh = 1
            while sh < c:
                cum = cum + jnp.where(t_in >= sh, pltpu.roll(cum, sh, 0), 0.0)
                sh *= 2
            last3 = cum.reshape(ng, c, HG_DK)[:, c - 1:c, :]
            last = jnp.broadcast_to(last3, (ng, c, HG_DK)).reshape(gr, HG_DK)
            qt.append((qs * jnp.exp(cum)).astype(BF16))
            kt.append((k * jnp.exp(-cum)).astype(BF16))
            kl.append((k * jnp.exp(last - cum)).astype(BF16))
            decay.append(jnp.exp(last3))
        kv = [_dot_tn(v[h], jnp.concatenate([kl[h]] * ng, axis=1) * own) for h in hs]
        attn = [jnp.where(causal, _dot_nt(qt[h], kt[h]), 0.0).astype(BF16) for h in hs]
        s_cat = []
        for h in hs:
            st = s_scr[h]
            starts = []
            for j in range(ng):
                starts.append(st)
                st = st * decay[h][j] + kv[h][:, j * HG_DK:(j + 1) * HG_DK]
            s_scr[h] = st
            s_cat.append(jnp.concatenate(starts, axis=1).astype(BF16))
        intra = [_dot(attn[h], v[h]) for h in hs]
        inter = [_dot_nt(jnp.concatenate([qt[h]] * ng, axis=1) * own, s_cat[h]) for h in hs]
        for h, cols in enumerate(heads):
            o_ref[0, rows, cols] = intra[h] + inter[h]
        return 0

    lax.fori_loop(0, n_rows // gr, body, 0, unroll=True)

    @pl.when(tt == pl.num_programs(2) - 1)
    def _():
        st_ref[0] = s_scr[...]


def hgrn_prompt(h_all, lb, *, tb, n_heads=2):
    bsz, t, _ = h_all.shape
    hw = n_heads * HG_DK
    nhb = D_MODEL // hw
    col = lambda c0: (lambda b, h, s: (b, s, c0 * nhb + h))
    return pl.pallas_call(
        functools.partial(_hgrn_prompt_kernel, n_heads=n_heads),
        grid=(bsz, nhb, t // tb),
        in_specs=[
            pl.BlockSpec((1, tb, hw), col(0)),
            pl.BlockSpec((1, tb, hw), col(1)),
            pl.BlockSpec((1, tb, hw), col(2)),
            pl.BlockSpec((1, hw), lambda b, h, s: (0, h)),
        ],
        out_specs=[
            pl.BlockSpec((1, tb, hw), lambda b, h, s: (b, s, h)),
            pl.BlockSpec((1, n_heads, HG_DK, HG_DK), lambda b, h, s: (b, h, 0, 0)),
        ],
        out_shape=[SDS((bsz, t, D_MODEL), F32), SDS((bsz, HG_HEADS, HG_DK, HG_DK), F32)],
        scratch_shapes=[pltpu.VMEM((n_heads, HG_DK, HG_DK), F32)],
        compiler_params=_cparams(3),
        name="hgrn_prompt",
    )(h_all, h_all, h_all, lb.reshape(1, D_MODEL))


def _hgrn_sample_kernel(q_ref, f_ref, v_ref, lb_ref, s0_ref, o_ref, s_ref, *, n_heads, t_new):
    rows = q_ref.shape[0]
    ri = lax.broadcasted_iota(jnp.int32, (rows, rows), 0)
    ci = lax.broadcasted_iota(jnp.int32, (rows, rows), 1)
    causal = (ri >= ci) & ((ri >= t_new) == (ci >= t_new))
    tril = causal.astype(F32)
    hw = n_heads * HG_DK
    second_w = lax.broadcasted_iota(jnp.int32, (rows, hw), 0) >= t_new
    second = second_w[:, :HG_DK]
    eye = (lax.broadcasted_iota(jnp.int32, (HG_DK, HG_DK), 0)
           == lax.broadcasted_iota(jnp.int32, (HG_DK, HG_DK), 1))
    heads = [slice(hd * HG_DK, (hd + 1) * HG_DK) for hd in range(n_heads)]
    qs, lf, k = _hg_gates(q_ref[...], f_ref[...], lb_ref[...])
    v = v_ref[...].astype(BF16)
    cum = jnp.dot(tril, lf, preferred_element_type=F32, precision=lax.Precision.HIGHEST)
    last0 = cum[t_new - 1:t_new]
    last1 = cum[rows - 1:rows]
    qt = (qs * jnp.exp(cum)).astype(BF16)
    kt = (k * jnp.exp(-cum)).astype(BF16)
    kl = k * jnp.exp(jnp.where(second_w, last1, last0) - cum)
    kl_seq = (jnp.where(second_w, 0.0, kl).astype(BF16), jnp.where(second_w, kl, 0.0).astype(BF16))
    decay_seq = (jnp.exp(last0), jnp.exp(last1))
    attn = [jnp.where(causal, _dot_nt(qt[:, c], kt[:, c]), 0.0).astype(BF16) for c in heads]
    inter = [[_dot(qt[:, c], s0_ref[bb, hd].astype(BF16)) for hd, c in enumerate(heads)] for bb in range(2)]
    intra = [_dot(a, v[:, c]) for a, c in zip(attn, heads)]
    for hd, c in enumerate(heads):
        o_ref[:, c] = intra[hd] + jnp.where(second, inter[1][hd], inter[0][hd])
    for bb in range(2):
        kv = [_dot_tn(kl_seq[bb][:, c], v[:, c]) for c in heads]
        for hd, c in enumerate(heads):
            decay = jnp.sum(jnp.where(eye, decay_seq[bb][:, c], 0.0), axis=1, keepdims=True)
            s_ref[bb, hd] = decay * s0_ref[bb, hd] + kv[hd]


def hgrn_sample(h_all, lb, s0, *, t_new, n_heads=8):
    m = h_all.shape[0]
    rows = 2 * t_new
    hw = n_heads * HG_DK
    nhb = D_MODEL // hw
    col = lambda c0: (lambda i, h: (i, c0 * nhb + h))
    return pl.pallas_call(
        functools.partial(_hgrn_sample_kernel, n_heads=n_heads, t_new=t_new),
        grid=(m // rows, nhb),
        in_specs=[
            pl.BlockSpec((rows, hw), col(0)),
            pl.BlockSpec((rows, hw), col(1)),
            pl.BlockSpec((rows, hw), col(2)),
            pl.BlockSpec((1, hw), lambda i, h: (0, h)),
            pl.BlockSpec((2, n_heads, HG_DK, HG_DK), lambda i, h: (i, h, 0, 0)),
        ],
        out_specs=[
            pl.BlockSpec((rows, hw), lambda i, h: (i, h)),
            pl.BlockSpec((2, n_heads, HG_DK, HG_DK), lambda i, h: (i, h, 0, 0)),
        ],
        out_shape=[SDS((m, D_MODEL), F32), SDS(s0.shape, F32)],
        compiler_params=_cparams(2),
        name="hgrn_sample",
    )(h_all, h_all, h_all, lb.reshape(1, D_MODEL), s0)


def _mem_attention(q, mk_ref, mv_ref, n_seq):
    rows = q.shape[0]
    second = lax.broadcasted_iota(jnp.int32, (rows, 1), 0) >= rows // 2

    def mem_head(ref, seq, h):
        if len(ref.shape) == 3:
            return ref[seq, :, h * MEM_HEAD_DIM:(h + 1) * MEM_HEAD_DIM].astype(BF16)
        return ref[0, seq, :, h, :].astype(BF16)

    hs = range(MEM_HEADS)
    qh = [q[:, h * MEM_HEAD_DIM:(h + 1) * MEM_HEAD_DIM] for h in hs]
    s_seq = [[_dot_nt(qh[h], mem_head(mk_ref, seq, h)) for h in hs] for seq in range(n_seq)]
    p = []
    for h in hs:
        s = s_seq[0][h] if n_seq == 1 else jnp.where(second, s_seq[1][h], s_seq[0][h])
        s = s * (MEM_HEAD_DIM ** -0.5)
        e = jnp.exp(s - jnp.max(s, axis=-1, keepdims=True))
        p.append((e / jnp.sum(e, axis=-1, keepdims=True)).astype(BF16))
    o_seq = [[_dot(p[h], mem_head(mv_ref, seq, h)) for h in hs] for seq in range(n_seq)]
    heads = [(o_seq[0][h] if n_seq == 1 else jnp.where(second, o_seq[1][h], o_seq[0][h])).astype(BF16) for h in hs]
    return jnp.concatenate(heads, axis=1)


def _xattn_kernel(y_ref, yb_ref, wq_ref, mk_ref, mv_ref, wo_ref, g_ref, b_ref, o_ref, ob_ref, *, n_seq):
    q = _dot(yb_ref[...], wq_ref[...]).astype(BF16)
    att = _mem_attention(q, mk_ref, mv_ref, n_seq)
    _proj_res_ln([att], wo_ref, y_ref, g_ref, b_ref, o_ref, ob_ref)


def _mem_attn_kernel(q_ref, mk_ref, mv_ref, o_ref):
    o_ref[...] = _mem_attention(q_ref[...], mk_ref, mv_ref, 2)


def mem_attn_cache(q, cache_k, cache_v, layer, *, rows_per_seq):
    m, dm = q.shape
    rows = 2 * rows_per_seq
    mem_spec = pl.BlockSpec((1, 2, N_MEM, MEM_HEADS, MEM_HEAD_DIM), lambda i: (layer, i, 0, 0, 0))
    return pl.pallas_call(
        _mem_attn_kernel,
        grid=(m // rows,),
        in_specs=[pl.BlockSpec((rows, dm), lambda i: (i, 0)), mem_spec, mem_spec],
        out_specs=pl.BlockSpec((rows, dm), lambda i: (i, 0)),
        out_shape=SDS((m, dm), BF16),
        compiler_params=_cparams(1),
        name="mem_attn_cache",
    )(q, cache_k, cache_v)


def xattn_ln(y, yb, wq, mk, mv, wo, g, b, *, bm, rows_per_seq):
    m, d = y.shape
    assert rows_per_seq % bm == 0
    row = lambda i: (i, 0)
    fixed = lambda i: (0, 0)
    mem_spec = pl.BlockSpec((1, N_MEM, D_MEM), lambda i: (i * bm // rows_per_seq, 0, 0))
    return pl.pallas_call(
        functools.partial(_xattn_kernel, n_seq=1),
        grid=(m // bm,),
        in_specs=[
            pl.BlockSpec((bm, d), row),
            pl.BlockSpec((bm, d), row),
            pl.BlockSpec((d, D_MEM), fixed),
            mem_spec,
            mem_spec,
            pl.BlockSpec((D_MEM, d), fixed),
            pl.BlockSpec((1, d), fixed),
            pl.BlockSpec((1, d), fixed),
        ],
        out_specs=[pl.BlockSpec((bm, d), row), pl.BlockSpec((bm, d), row)],
        out_shape=[SDS((m, d), F32), SDS((m, d), BF16)],
        compiler_params=_cparams(1),
        name="xattn_ln",
    )(y, yb, wq, mk, mv, wo, g.reshape(1, d), b.reshape(1, d))


def _s5_discretize(lam_re, lam_im, log_dt, b_re, b_im):
    lr = jnp.minimum(lam_re.astype(F32), -1e-4)
    li = lam_im.astype(F32)
    dt = jnp.exp(log_dt.astype(F32))[:, None]
    mag = jnp.exp(lr * dt)
    a_re = mag * jnp.cos(li * dt)
    a_im = mag * jnp.sin(li * dt)
    den = lr * lr + li * li
    fr = ((a_re - 1.0) * lr + a_im * li) / den
    fi = (a_im * lr - (a_re - 1.0) * li) / den
    br, bi = b_re.astype(F32), b_im.astype(F32)
    bb_re = fr[..., None] * br - fi[..., None] * bi
    bb_im = fr[..., None] * bi + fi[..., None] * br
    return a_re, a_im, bb_re, bb_im


def _s5_block_weights(a_re, a_im, bb_re, bb_im, c_re, c_im, d_skip):
    eye = jnp.eye(S5_GB, dtype=F32)
    shp_b = (S5_NGB, S5_GB, S5_STATE, S5_GROUP)
    shp_c = (S5_NGB, S5_GB, S5_GROUP, S5_STATE)
    blk_b = lambda w: jnp.einsum('bgph,gk->bghkp', w.reshape(shp_b), eye).reshape(S5_NGB, S5_UW, S5_HW)
    blk_c = lambda w: jnp.einsum('bghp,gk->bkpgh', w.reshape(shp_c), eye).reshape(S5_NGB, S5_HW, S5_UW)
    bw = jnp.concatenate([blk_b(bb_re), blk_b(bb_im)], axis=-1).astype(BF16)
    cw = jnp.concatenate([blk_c(c_re.astype(F32)), blk_c(-c_im.astype(F32))], axis=1).astype(BF16)
    return (bw, cw, a_re.reshape(S5_NGB, 1, S5_HW), a_im.reshape(S5_NGB, 1, S5_HW),
            d_skip.astype(F32).reshape(S5_NGB, 1, S5_UW))


def _s5_state_to_blocks(s_re, s_im):
    b = s_re.shape[0]
    return jnp.concatenate([s_re.reshape(b, S5_NGB, S5_HW), s_im.reshape(b, S5_NGB, S5_HW)],
                           axis=-1).reshape(b, S5_NGB * 2 * S5_HW)


def _s5_state_from_blocks(h):
    b = h.shape[0]
    return (h[:, :, :S5_HW].reshape(b, S5_GROUPS, S5_STATE), h[:, :, S5_HW:].reshape(b, S5_GROUPS, S5_STATE))


def kernel(x_prompt, x_sample, cache_mem_k, cache_mem_v, cache_swa_k, cache_swa_v, state_s5_re, state_s5_im, state_hgrn, mem_prompt, rel_bias, w_even_in, s5_lam_re, s5_lam_im, s5_log_dt, s5_b_re, s5_b_im, s5_c_re, s5_c_im, s5_d, s5_w_glu, swa_sinks, w_even_out, hg_lb_logits, w_odd_in, hg_norm_g, w_odd_out, w_mem_q, w_mem_k, w_mem_v, w_mem_o, w_ffn_gate, w_ffn_up, w_ffn_down, ln_g, ln_b):
    bsz, seq, d = x_prompt.shape
    dec_b, dec_t, _ = x_sample.shape
    mp, ms = bsz * seq, dec_b * dec_t
    w_buf = cache_swa_k.shape[2]
    bf = lambda w: w.astype(BF16)

    lb_soft = jax.nn.softmax(hg_lb_logits.astype(F32), axis=0)
    lower_bounds = jnp.cumsum(lb_soft, axis=0) - lb_soft[0]
    table = bias_table(rel_bias.astype(F32))
    bias_s = jnp.tile(table[:, :dec_t], (1, 2, 1))
    mem_b = bf(mem_prompt).reshape(bsz * N_MEM, d)

    yp = x_prompt.reshape(mp, d)
    ys = x_sample.reshape(ms, d)
    ypb, ysb = None, bf(ys)
    p_mem_k, p_mem_v = [], []
    p_swa_k, p_swa_v, p_s5_re, p_s5_im, p_hg = [], [], [], [], []
    s_swa_k, s_swa_v, s_s5_re, s_s5_im, s_hg = [], [], [], [], []

    for l in range(DEPTH):
        j = l // 2
        g0, b0 = ln_g[l, 0], ln_b[l, 0]
        if l % 2 == 0:
            a_re, a_im, bb_re, bb_im = _s5_discretize(s5_lam_re[j], s5_lam_im[j], s5_log_dt[j], s5_b_re[j], s5_b_im[j])
            s5w = _s5_block_weights(a_re, a_im, bb_re, bb_im, s5_c_re[j], s5_c_im[j], s5_d[j])
            w_glu, w_out = cast_layer(s5_w_glu, j), cast_layer(w_even_out, j)
            sinks = swa_sinks[j].astype(F32)
            kv0 = D_S5 + D_SWA_Q

            hs, w_in = matmul_cast(ysb, w_even_in, j, bn=512)
            hp = (matmul_x32(yp, w_in, bm=512, bn=1024) if ypb is None
                  else matmul(ypb, w_in, bm=1024, bn=1024)).reshape(bsz, seq, -1)
            z, hfin = s5_prompt(hp, *s5w, tb=512)
            s5_out = glu(z.reshape(mp, D_S5), w_glu, bm=1024, bn=1024)
            att = swa_prompt(hp, sinks, table).reshape(mp, D_SWA_Q)
            hr, hi = _s5_state_from_blocks(hfin.reshape(bsz, S5_NGB, 2 * S5_HW))
            p_s5_re.append(hr); p_s5_im.append(hi)
            p_swa_k.append(hp[:, seq - w_buf:, kv0:kv0 + D_SWA_KV].reshape(bsz, w_buf, SWA_KV_HEADS, HEAD_DIM))
            p_swa_v.append(hp[:, seq - w_buf:, kv0 + D_SWA_KV:].reshape(bsz, w_buf, SWA_KV_HEADS, HEAD_DIM))
            yp, ypb = proj_res_ln([s5_out, att], w_out, yp, g0, b0, bm=128)

            h0x = jnp.repeat(_s5_state_to_blocks(state_s5_re[j].astype(F32), state_s5_im[j].astype(F32)), dec_t, axis=0)
            z, h_steps = s5_sample(hs, h0x, *s5w, period=dec_t)
            s5_out = glu(z, w_glu, bm=ms, bn=1024)
            hs3 = hs.reshape(dec_b, dec_t, -1)
            kpad = jnp.zeros((dec_b, 2 * WINDOW - w_buf - dec_t, D_SWA_KV), F32)
            kk = jnp.concatenate([cache_swa_k[j].reshape(dec_b, w_buf, D_SWA_KV).astype(F32),
                                  hs3[:, :, kv0:kv0 + D_SWA_KV], kpad], axis=1)
            vv = jnp.concatenate([cache_swa_v[j].reshape(dec_b, w_buf, D_SWA_KV).astype(F32),
                                  hs3[:, :, kv0 + D_SWA_KV:], kpad], axis=1)
            att = swa_sample(hs, kk, vv, sinks, bias_s, t_new=dec_t)
            hr, hi = _s5_state_from_blocks(h_steps.reshape(dec_b, dec_t, S5_NGB, 2 * S5_HW)[:, dec_t - 1])
            s_s5_re.append(hr); s_s5_im.append(hi)
            s_swa_k.append(kk[:, dec_t:dec_t + w_buf].reshape(dec_b, w_buf, SWA_KV_HEADS, HEAD_DIM))
            s_swa_v.append(vv[:, dec_t:dec_t + w_buf].reshape(dec_b, w_buf, SWA_KV_HEADS, HEAD_DIM))
            ys, ysb = proj_res_ln([s5_out, att], w_out, ys, g0, b0, bm=ms)
        else:
            w_out = cast_layer(w_odd_out, j)
            lb = lower_bounds[l]

            hs, w_in = matmul_cast(ysb, w_odd_in, j, bn=512)
            hp = matmul(ypb, w_in, bm=1024, bn=1024)
            o, st = hgrn_prompt(hp.reshape(bsz, seq, -1), lb, tb=512, n_heads=4)
            p_hg.append(jnp.swapaxes(st, -1, -2))
            yp, ypb = hg_out_ln(o.reshape(mp, d), hp, hg_norm_g[j], w_out, yp, g0, b0, bm=128)

            o, s_new = hgrn_sample(hs, lb, state_hgrn[j].astype(F32), t_new=dec_t, n_heads=16)
            s_hg.append(s_new)
            ys, ysb = hg_out_ln(o, hs, hg_norm_g[j], w_out, ys, g0, b0, bm=ms)

        wo = cast_layer(w_mem_o, l)
        mk = matmul_w32(mem_b, w_mem_k, l, bn=D_MEM).reshape(bsz, N_MEM, D_MEM)
        mv = matmul_w32(mem_b, w_mem_v, l, bn=D_MEM).reshape(bsz, N_MEM, D_MEM)
        p_mem_k.append(mk.reshape(bsz, N_MEM, MEM_HEADS, MEM_HEAD_DIM))
        p_mem_v.append(mv.reshape(bsz, N_MEM, MEM_HEADS, MEM_HEAD_DIM))
        qs, wq = matmul_cast(ysb, w_mem_q, l, bn=D_MEM, out_dtype=BF16)
        yp, ypb = xattn_ln(yp, ypb, wq, mk, mv, wo, ln_g[l, 1], ln_b[l, 1], bm=256, rows_per_seq=seq)
        att = mem_attn_cache(qs, cache_mem_k, cache_mem_v, l, rows_per_seq=dec_t)
        ys, ysb = proj_res_ln([att], wo, ys, ln_g[l, 1], ln_b[l, 1], bm=ms)

        ys, ysb, wgu, wd = ffn_ln_cast(ys, w_ffn_gate, w_ffn_up, w_ffn_down, l, ln_g[l, 2], ln_b[l, 2])
        yp, ypb = ffn_ln(yp, wgu, wd, ln_g[l, 2], ln_b[l, 2], bm=512)

    return (yp.reshape(bsz, seq, d), ys.reshape(dec_b, dec_t, d),
            jnp.stack(p_mem_k), jnp.stack(p_mem_v),
            jnp.stack(p_swa_k), jnp.stack(p_swa_v),
            jnp.stack(p_s5_re), jnp.stack(p_s5_im), jnp.stack(p_hg),
            jnp.stack(s_swa_k), jnp.stack(s_swa_v),
            jnp.stack(s_s5_re), jnp.stack(s_s5_im), jnp.stack(s_hg))
```
